```python
import math
import jax, jax.numpy as jnp
from jax import lax
import numpy as np

D_MODEL = 1024
BATCH = 2
SEQ = 8192
DEPTH = 2
DEC_BATCH = 32
DEC_SEQ = 1
PAST_LEN = 8192
PAGE_SIZE = 128

D_MIX = D_MODEL
C_CONV = D_MIX // 4
CONV_W = 31
DH = 64
A_WIDTH = D_MIX // 4
H_A = A_WIDTH // DH
MOBA_BLOCK = 256
MOBA_TOPK = 3
Q_CHUNK = 128
NUM_BUCKETS = 32
REL_MAX_DIST = 128
D_INNER = D_MIX // 2
SSM_P = 64
SSM_H = D_INNER // SSM_P
SSM_G = 2
SSM_N = 128
SSM_CONV_W = 4
SSM_CONV_DIM = D_INNER + 2 * SSM_G * SSM_N
SSD_CHUNK = 256
DT_MIN = 0.001
DT_MAX = 0.1
D_FF = 2816

OFF_Q = 2 * C_CONV
OFF_K = OFF_Q + A_WIDTH
OFF_V = OFF_K + A_WIDTH
OFF_Z = OFF_V + A_WIDTH
OFF_XBC = OFF_Z + D_INNER
OFF_DT = OFF_XBC + SSM_CONV_DIM
N_IN = OFF_DT + SSM_H

kernel_name = 'hymba_conformer_moba_ssd_step'


def rmsnorm(x, g, eps=1e-6):
    xf = x.astype(jnp.float32)
    y = xf * lax.rsqrt(jnp.mean(xf * xf, axis=-1, keepdims=True) + eps)
    return (y * g.astype(jnp.float32)).astype(x.dtype)


def layernorm(x, g, b, eps=1e-5):
    xf = x.astype(jnp.float32)
    xc = xf - jnp.mean(xf, axis=-1, keepdims=True)
    var = jnp.mean(xc * xc, axis=-1, keepdims=True)
    return (xc * lax.rsqrt(var + eps) * g.astype(jnp.float32) + b.astype(jnp.float32)).astype(x.dtype)


def swiglu(h, wg, wu, wd):
    return (jax.nn.silu(h @ wg) * (h @ wu)) @ wd


def causal_dwconv(buf, u, w, b):
    width, ch = w.shape
    full = jnp.concatenate([buf.astype(u.dtype), u], axis=1)
    y = lax.conv_general_dilated(full, w[:, None, :].astype(u.dtype), window_strides=(1,),
                                 padding='VALID', dimension_numbers=('NWC', 'WIO', 'NWC'),
                                 feature_group_count=ch)
    return y + b.astype(u.dtype), full[:, full.shape[1] - (width - 1):]


def rel_bucket(dist):
    n = jnp.maximum(dist, 0)
    max_exact = NUM_BUCKETS // 2
    nf = jnp.maximum(n, 1).astype(jnp.float32)
    large = max_exact + (jnp.log(nf / max_exact) / math.log(REL_MAX_DIST / max_exact)
                         * (NUM_BUCKETS - max_exact)).astype(jnp.int32)
    large = jnp.minimum(large, NUM_BUCKETS - 1)
    return jnp.where(n < max_exact, n, large)


def moba_attend(q, qpos, kt, vt, kmean, rel_bias):
    bsz, nq, nh, dh = q.shape
    nblk = kt.shape[2]
    k_sel = min(MOBA_TOPK, nblk)
    cur = qpos // MOBA_BLOCK
    qt = q.transpose(0, 2, 1, 3)
    gate = jnp.einsum('bhqd,bhnd->bhqn', qt.astype(jnp.float32), kmean)
    eligible = jnp.arange(nblk)[None, :] < cur[:, None]
    gate = jnp.where(eligible[None, None], gate, -jnp.inf)
    _, top = lax.top_k(gate, k_sel)
    idx = jnp.concatenate([top, jnp.broadcast_to(cur[None, None, :, None], (bsz, nh, nq, 1))], axis=-1)
    blk_ok = jnp.concatenate([jnp.arange(k_sel)[None, :] < cur[:, None],
                              jnp.ones((nq, 1), dtype=bool)], axis=-1)
    bi = jnp.arange(bsz)[:, None, None, None]
    hi = jnp.arange(nh)[None, :, None, None]
    kg = kt[bi, hi, idx]
    vg = vt[bi, hi, idx]
    kpos = idx[..., None] * MOBA_BLOCK + jnp.arange(MOBA_BLOCK)
    dist = qpos[None, None, :, None, None] - kpos
    mask = blk_ok[None, None, :, :, None] & (dist >= 0)
    bias = rel_bias[rel_bucket(dist), hi[..., None]].astype(jnp.float32)
    logits = jnp.einsum('bhqd,bhqjsd->bhqjs', qt, kg).astype(jnp.float32) * (1.0 / math.sqrt(dh)) + bias
    logits = jnp.where(mask, logits, -jnp.inf)
    p = jax.nn.softmax(logits.reshape(bsz, nh, nq, -1), axis=-1).reshape(logits.shape)
    return jnp.einsum('bhqjs,bhqjsd->bqhd', p.astype(vg.dtype), vg)


def moba_attention(q, pos0, k_all, v_all, rel_bias):
    bsz, t, nh, dh = q.shape
    tk = k_all.shape[1]
    pad = (-tk) % MOBA_BLOCK
    kt = jnp.pad(k_all, ((0, 0), (0, pad), (0, 0), (0, 0))).reshape(bsz, -1, MOBA_BLOCK, nh, dh).transpose(0, 3, 1, 2, 4)
    vt = jnp.pad(v_all, ((0, 0), (0, pad), (0, 0), (0, 0))).reshape(bsz, -1, MOBA_BLOCK, nh, dh).transpose(0, 3, 1, 2, 4)
    kmean = jnp.mean(kt.astype(jnp.float32), axis=3)
    qpos = pos0 + jnp.arange(t, dtype=jnp.int32)
    qc = Q_CHUNK if t % Q_CHUNK == 0 else t
    nqc = t // qc
    if nqc == 1:
        return moba_attend(q, qpos, kt, vt, kmean, rel_bias)
    qs = q.reshape(bsz, nqc, qc, nh, dh).transpose(1, 0, 2, 3, 4)
    ps = qpos.reshape(nqc, qc)
    out = lax.map(lambda a: moba_attend(a[0], a[1], kt, vt, kmean, rel_bias), (qs, ps))
    return out.transpose(1, 0, 2, 3, 4).reshape(bsz, t, nh, dh)


def ssd_scan(x, dt, a, bm, cm, h0):
    bsz, t, nh, hp = x.shape
    rep = nh // bm.shape[2]
    lc = min(SSD_CHUNK, t)
    pad = (-t) % lc
    if pad:
        x = jnp.pad(x, ((0, 0), (0, pad), (0, 0), (0, 0)))
        dt = jnp.pad(dt, ((0, 0), (0, pad), (0, 0)))
        bm = jnp.pad(bm, ((0, 0), (0, pad), (0, 0), (0, 0)))
        cm = jnp.pad(cm, ((0, 0), (0, pad), (0, 0), (0, 0)))
    nc = (t + pad) // lc
    bh = jnp.repeat(bm, rep, axis=2).reshape(bsz, nc, lc, nh, -1)
    ch = jnp.repeat(cm, rep, axis=2).reshape(bsz, nc, lc, nh, -1)
    xdt = (x * dt[..., None].astype(x.dtype)).reshape(bsz, nc, lc, nh, hp)
    acum = jnp.cumsum((dt * a).reshape(bsz, nc, lc, nh), axis=2)
    seg = acum[:, :, :, None, :] - acum[:, :, None, :, :]
    causal = jnp.tril(jnp.ones((lc, lc), dtype=bool))
    decay = jnp.exp(jnp.where(causal[None, None, :, :, None], seg, -jnp.inf))
    scores = jnp.einsum('bclhn,bcshn->bclsh', ch, bh) * decay
    y_intra = jnp.einsum('bclsh,bcshp->bclhp', scores, xdt)
    to_end = jnp.exp(acum[:, :, -1:, :] - acum)
    states = jnp.einsum('bclhn,bclh,bclhp->bchpn', bh, to_end, xdt).astype(jnp.float32)
    chunk_decay = jnp.exp(acum[:, :, -1, :])

    def step(h, inp):
        s, d = inp
        return h * d[:, :, None, None] + s, h

    h_fin, h_in = lax.scan(step, h0.astype(jnp.float32),
                           (states.transpose(1, 0, 2, 3, 4), chunk_decay.transpose(1, 0, 2)))
    h_in = h_in.transpose(1, 0, 2, 3, 4)
    y_inter = jnp.einsum('bclhn,bchpn,bclh->bclhp', ch, h_in, jnp.exp(acum))
    y = (y_intra + y_inter).reshape(bsz, nc * lc, nh, hp)[:, :t]
    return y, h_fin


def layer(x, conv_buf, ssm_conv_buf, ssm_h0, past_k, past_v, rel_bias, lw):
    (ffn1_norm, ffn1_wg, ffn1_wu, ffn1_wd, mix_norm, w_in, w_out, conv_dw_w, conv_dw_b,
     conv_ln_g, conv_ln_b, q_norm_g, k_norm_g, ssm_conv_w, ssm_conv_b, ssm_dt_bias,
     ssm_a_log, ssm_d, ssm_norm_g, ffn2_norm, ffn2_wg, ffn2_wu, ffn2_wd, final_norm) = lw
    x = x + 0.5 * swiglu(rmsnorm(x, ffn1_norm), ffn1_wg, ffn1_wu, ffn1_wd)
    h = rmsnorm(x, mix_norm)
    bsz, t, _ = h.shape
    proj = h @ w_in
    u = proj[..., :C_CONV] * jax.nn.sigmoid(proj[..., C_CONV:OFF_Q])
    c, conv_new = causal_dwconv(conv_buf, u, conv_dw_w, conv_dw_b)
    conv_out = jax.nn.silu(layernorm(c, conv_ln_g, conv_ln_b))
    q = rmsnorm(proj[..., OFF_Q:OFF_K].reshape(bsz, t, H_A, DH), q_norm_g)
    k = rmsnorm(proj[..., OFF_K:OFF_V].reshape(bsz, t, H_A, DH), k_norm_g)
    v = proj[..., OFF_V:OFF_Z].reshape(bsz, t, H_A, DH)
    if past_k is None:
        pos0, k_all, v_all = 0, k, v
    else:
        pos0 = past_k.shape[1]
        k_all = jnp.concatenate([past_k.astype(k.dtype), k], axis=1)
        v_all = jnp.concatenate([past_v.astype(v.dtype), v], axis=1)
    attn_out = moba_attention(q, pos0, k_all, v_all, rel_bias).reshape(bsz, t, A_WIDTH)
    z = proj[..., OFF_Z:OFF_XBC]
    xbc, ssm_conv_new = causal_dwconv(ssm_conv_buf, proj[..., OFF_XBC:OFF_DT], ssm_conv_w, ssm_conv_b)
    xbc = jax.nn.silu(xbc)
    xs = xbc[..., :D_INNER].reshape(bsz, t, SSM_H, SSM_P)
    bm = xbc[..., D_INNER:D_INNER + SSM_G * SSM_N].reshape(bsz, t, SSM_G, SSM_N)
    cm = xbc[..., D_INNER + SSM_G * SSM_N:].reshape(bsz, t, SSM_G, SSM_N)
    dt = jax.nn.softplus(proj[..., OFF_DT:N_IN].astype(jnp.float32) + ssm_dt_bias.astype(jnp.float32))
    a = -jnp.exp(ssm_a_log.astype(jnp.float32))
    y, h_fin = ssd_scan(xs, dt, a, bm, cm, ssm_h0)
    y = (y + ssm_d[:, None] * xs).reshape(bsz, t, D_INNER)
    ssm_out = rmsnorm(y * jax.nn.silu(z), ssm_norm_g)
    mix = jnp.concatenate([conv_out, attn_out.astype(conv_out.dtype), ssm_out.astype(conv_out.dtype)], axis=-1) @ w_out
    x = x + mix.astype(x.dtype)
    x = x + 0.5 * swiglu(rmsnorm(x, ffn2_norm), ffn2_wg, ffn2_wu, ffn2_wd)
    x = rmsnorm(x, final_norm)
    return x, (k, v, conv_new, ssm_conv_new, h_fin)


def setup_inputs(seed: int = 0) -> dict:
    key = jax.random.key(seed)
    ks = iter(jax.random.split(key, 48))
    f32 = jnp.float32

    def nrm(shape, scale):
        return jax.random.normal(next(ks), shape, f32) * scale

    def gain(shape):
        return 1.0 + nrm(shape, 0.02)

    n_pages = PAST_LEN // PAGE_SIZE
    n_pool = (DEC_BATCH * n_pages * 5) // 4
    page_table = jax.random.permutation(next(ks), n_pool)[:DEC_BATCH * n_pages].reshape(DEC_BATCH, n_pages).astype(jnp.int32)
    dt0 = jnp.exp(jax.random.uniform(next(ks), (DEPTH, SSM_H), f32, math.log(DT_MIN), math.log(DT_MAX)))
    dt_bias = dt0 + jnp.log(-jnp.expm1(-dt0))
    a_log = jnp.log(jax.random.uniform(next(ks), (DEPTH, SSM_H), f32, 1.0, 16.0))
    return {
        'x_prompt': nrm((BATCH, SEQ, D_MODEL), 1.0),
        'x_sample': nrm((DEC_BATCH, DEC_SEQ, D_MODEL), 1.0),
        'cache_k': nrm((DEPTH, n_pool, PAGE_SIZE, H_A, DH), 1.0),
        'cache_v': nrm((DEPTH, n_pool, PAGE_SIZE, H_A, DH), 1.0),
        'state_conv': nrm((DEPTH, DEC_BATCH, CONV_W - 1, C_CONV), 0.5),
        'state_ssm_conv': nrm((DEPTH, DEC_BATCH, SSM_CONV_W - 1, SSM_CONV_DIM), 1.0),
        'state_ssm': nrm((DEPTH, DEC_BATCH, SSM_H, SSM_P, SSM_N), 0.5),
        'page_table': page_table,
        'rel_bias': nrm((NUM_BUCKETS, H_A), 0.2),
        'ffn1_norm': gain((DEPTH, D_MODEL)),
        'ffn1_wg': nrm((DEPTH, D_MODEL, D_FF), D_MODEL ** -0.5),
        'ffn1_wu': nrm((DEPTH, D_MODEL, D_FF), D_MODEL ** -0.5),
        'ffn1_wd': nrm((DEPTH, D_FF, D_MODEL), D_FF ** -0.5),
        'mix_norm': gain((DEPTH, D_MODEL)),
        'w_in': nrm((DEPTH, D_MODEL, N_IN), D_MODEL ** -0.5),
        'w_out': nrm((DEPTH, D_MIX, D_MODEL), D_MIX ** -0.5),
        'conv_dw_w': nrm((DEPTH, CONV_W, C_CONV), CONV_W ** -0.5),
        'conv_dw_b': nrm((DEPTH, C_CONV), 0.02),
        'conv_ln_g': gain((DEPTH, C_CONV)),
        'conv_ln_b': nrm((DEPTH, C_CONV), 0.02),
        'q_norm_g': gain((DEPTH, DH)),
        'k_norm_g': gain((DEPTH, DH)),
        'ssm_conv_w': nrm((DEPTH, SSM_CONV_W, SSM_CONV_DIM), SSM_CONV_W ** -0.5),
        'ssm_conv_b': nrm((DEPTH, SSM_CONV_DIM), 0.02),
        'ssm_dt_bias': dt_bias,
        'ssm_a_log': a_log,
        'ssm_d': gain((DEPTH, SSM_H)),
        'ssm_norm_g': gain((DEPTH, D_INNER)),
        'ffn2_norm': gain((DEPTH, D_MODEL)),
        'ffn2_wg': nrm((DEPTH, D_MODEL, D_FF), D_MODEL ** -0.5),
        'ffn2_wu': nrm((DEPTH, D_MODEL, D_FF), D_MODEL ** -0.5),
        'ffn2_wd': nrm((DEPTH, D_FF, D_MODEL), D_FF ** -0.5),
        'final_norm': gain((DEPTH, D_MODEL)),
    }


def reference(x_prompt, x_sample, cache_k, cache_v, state_conv, state_ssm_conv, state_ssm, page_table,
              rel_bias, ffn1_norm, ffn1_wg, ffn1_wu, ffn1_wd, mix_norm, w_in, w_out, conv_dw_w, conv_dw_b,
              conv_ln_g, conv_ln_b, q_norm_g, k_norm_g, ssm_conv_w, ssm_conv_b, ssm_dt_bias, ssm_a_log,
              ssm_d, ssm_norm_g, ffn2_norm, ffn2_wg, ffn2_wu, ffn2_wd, final_norm):
    stacked = (ffn1_norm, ffn1_wg, ffn1_wu, ffn1_wd, mix_norm, w_in, w_out, conv_dw_w, conv_dw_b,
               conv_ln_g, conv_ln_b, q_norm_g, k_norm_g, ssm_conv_w, ssm_conv_b, ssm_dt_bias,
               ssm_a_log, ssm_d, ssm_norm_g, ffn2_norm, ffn2_wg, ffn2_wu, ffn2_wd, final_norm)
    bp = x_prompt.shape[0]
    zc = jnp.zeros((bp, CONV_W - 1, C_CONV), x_prompt.dtype)
    zsc = jnp.zeros((bp, SSM_CONV_W - 1, SSM_CONV_DIM), x_prompt.dtype)
    zh = jnp.zeros((bp, SSM_H, SSM_P, SSM_N), jnp.float32)
    y_p = x_prompt
    st_p = []
    for l in range(DEPTH):
        lw = tuple(w[l] for w in stacked)
        y_p, st = layer(y_p, zc, zsc, zh, None, None, rel_bias, lw)
        st_p.append(st)
    db = x_sample.shape[0]
    past_len = page_table.shape[1] * PAGE_SIZE
    y_s = x_sample
    st_s = []
    for l in range(DEPTH):
        lw = tuple(w[l] for w in stacked)
        past_k = cache_k[l][page_table].reshape(db, past_len, H_A, DH)
        past_v = cache_v[l][page_table].reshape(db, past_len, H_A, DH)
        y_s, st = layer(y_s, state_conv[l], state_ssm_conv[l], state_ssm[l], past_k, past_v, rel_bias, lw)
        st_s.append(st)
    k_p = jnp.stack([s[0] for s in st_p])
    v_p = jnp.stack([s[1] for s in st_p])
    c_p = jnp.stack([s[2] for s in st_p])
    sc_p = jnp.stack([s[3] for s in st_p])
    h_p = jnp.stack([s[4] for s in st_p])
    k_s = jnp.stack([s[0] for s in st_s])
    v_s = jnp.stack([s[1] for s in st_s])
    c_s = jnp.stack([s[2] for s in st_s])
    sc_s = jnp.stack([s[3] for s in st_s])
    h_s = jnp.stack([s[4] for s in st_s])
    return (y_p, y_s, k_p, v_p, c_p, sc_p, h_p, k_s, v_s, c_s, sc_s, h_s)
```

```python
import functools
import math

import jax
import jax.numpy as jnp
from jax import lax
from jax.experimental import pallas as pl
from jax.experimental.pallas import tpu as pltpu

F32 = jnp.float32
BF16 = jnp.bfloat16

DH = 64
MOBA_BLOCK = 256
MOBA_TOPK = 3
NUM_BUCKETS = 32
REL_MAX_DIST = 128
SSM_P = 64
SSM_G = 2
SSM_N = 128
SSD_CHUNK = 256
PAGE_SIZE = 128

LANES = 128
SUBLANES = 8
VMEM_LIMIT_BYTES = 56 * 1024 * 1024

NEG_INF = float("-inf")


def _params(*semantics):
    return pltpu.CompilerParams(dimension_semantics=semantics, vmem_limit_bytes=VMEM_LIMIT_BYTES)


def _dot(a, b):
    return jnp.dot(a, b, preferred_element_type=F32)


def _dot_t(a, b):
    return lax.dot_general(a, b, (((1,), (1,)), ((), ())), preferred_element_type=F32)


def _split3(x):
    hi = x.astype(BF16)
    r1 = x - hi.astype(F32)
    mid = r1.astype(BF16)
    lo = (r1 - mid.astype(F32)).astype(BF16)
    return hi, mid, lo


def _dot_exact_rhs(x, sel):
    hi, mid, lo = _split3(x)
    return _dot(hi, sel) + _dot(mid, sel) + _dot(lo, sel)


def _dot_exact_lhs(sel, x):
    hi, mid, lo = _split3(x)
    return _dot(sel, hi) + _dot(sel, mid) + _dot(sel, lo)


def _sigmoid(x):
    return 1.0 / (1.0 + jnp.exp(-x))


def _silu(x):
    return x * _sigmoid(x)


def _rmsnorm(x, g, eps=1e-6):
    return x * lax.rsqrt(jnp.mean(x * x, axis=-1, keepdims=True) + eps) * g


def _const_spec(shape):
    nd = len(shape)
    return pl.BlockSpec(shape, lambda *_: (0,) * nd, pipeline_mode=pl.Buffered(1))


def _swiglu(hb, wg_ref, wu_ref, wd_ref, fchunk):
    d_ff = wg_ref.shape[1]
    acc = None
    for c in range(d_ff // fchunk):
        sl = slice(c * fchunk, (c + 1) * fchunk)
        g = _dot(hb, wg_ref[:, sl])
        u = _dot(hb, wu_ref[:, sl])
        a = (_silu(g) * u).astype(BF16)
        part = _dot(a, wd_ref[sl, :])
        acc = part if acc is None else acc + part
    return acc


def _head_sumsq_matrix(width):
    r = lax.broadcasted_iota(jnp.int32, (width, width), 0) // DH
    c = lax.broadcasted_iota(jnp.int32, (width, width), 1) // DH
    return jnp.where(r == c, 1.0, 0.0).astype(BF16)


def _head_rmsnorm(p, g_row, eps=1e-6):
    ss = _dot_exact_rhs(p * p, _head_sumsq_matrix(p.shape[1]))
    return p * lax.rsqrt(ss * (1.0 / DH) + eps) * g_row


def _ffn_proj_kernel(x_ref, n1_ref, wg_ref, wu_ref, wd_ref, n2_ref, win_ref, qg_ref, kg_ref,
                     x1_ref, u_ref, q_ref, k_ref, v_ref, kb_ref, vb_ref, z_ref, xbc_ref, dt_ref,
                     *maybe_kmean_ref, fchunk, offs):
    c_conv, off_q, off_k, off_v, off_z, off_xbc, off_dt, n_pad = offs
    x = x_ref[...]
    hb = _rmsnorm(x, n1_ref[...]).astype(BF16)
    x1 = x + 0.5 * _swiglu(hb, wg_ref, wu_ref, wd_ref, fchunk)
    x1_ref[...] = x1
    h2 = _rmsnorm(x1, n2_ref[...]).astype(BF16)
    pg = _dot(h2, win_ref[:, 0:off_q])
    u_ref[...] = pg[:, :c_conv] * _sigmoid(pg[:, c_conv:])
    q_ref[...] = _head_rmsnorm(_dot(h2, win_ref[:, off_q:off_k]), qg_ref[...])
    k = _head_rmsnorm(_dot(h2, win_ref[:, off_k:off_v]), kg_ref[...])
    k_ref[...] = k
    kb_ref[...] = k.astype(BF16)
    v = _dot(h2, win_ref[:, off_v:off_z])
    v_ref[...] = v
    vb_ref[...] = v.astype(BF16)
    z_ref[...] = _dot(h2, win_ref[:, off_z:off_xbc])
    xbc_ref[...] = _dot(h2, win_ref[:, off_xbc:off_dt])
    dt_ref[...] = _dot(h2, win_ref[:, off_dt:n_pad])
    if maybe_kmean_ref:
        km_ref, = maybe_kmean_ref
        nb = k.shape[0] // MOBA_BLOCK
        km_ref[0] = jnp.mean(k.reshape(nb, MOBA_BLOCK, k.shape[1]), axis=1)


def _ffn_proj(x, lw, *, tm, emit_kmean):
    r, d = x.shape
    d_ff = lw["wg1"].shape[1]
    offs = lw["offs"]
    c_conv, off_q, off_k, off_v, off_z, off_xbc, off_dt, n_pad = offs
    a_w = off_k - off_q
    widths = dict(u=c_conv, q=a_w, k=a_w, v=a_w, z=off_xbc - off_z, xbc=off_dt - off_xbc, dt=n_pad - off_dt)
    row = lambda w: pl.BlockSpec((tm, w), lambda i: (i, 0))
    out_shape = [jax.ShapeDtypeStruct((r, d), F32),
                 jax.ShapeDtypeStruct((r, widths["u"]), F32),
                 jax.ShapeDtypeStruct((r, a_w), F32),
                 jax.ShapeDtypeStruct((r, a_w), F32),
                 jax.ShapeDtypeStruct((r, a_w), F32),
                 jax.ShapeDtypeStruct((r, a_w), BF16),
                 jax.ShapeDtypeStruct((r, a_w), BF16),
                 jax.ShapeDtypeStruct((r, widths["z"]), F32),
                 jax.ShapeDtypeStruct((r, widths["xbc"]), F32),
                 jax.ShapeDtypeStruct((r, widths["dt"]), F32)]
    out_specs = [row(d), row(widths["u"]), row(a_w), row(a_w), row(a_w), row(a_w), row(a_w),
                 row(widths["z"]), row(widths["xbc"]), row(widths["dt"])]
    if emit_kmean:
        nb = tm // MOBA_BLOCK
        out_shape.append(jax.ShapeDtypeStruct((r // tm, nb, a_w), F32))
        out_specs.append(pl.BlockSpec((1, nb, a_w), lambda i: (i, 0, 0)))
    fchunk = 256 if d_ff % 256 == 0 else d_ff
    outs = pl.pallas_call(
        functools.partial(_ffn_proj_kernel, fchunk=fchunk, offs=offs),
        grid=(r // tm,),
        in_specs=[row(d), _const_spec((1, d)), _const_spec((d, d_ff)), _const_spec((d, d_ff)),
                  _const_spec((d_ff, d)), _const_spec((1, d)), _const_spec((d, n_pad)),
                  _const_spec((1, a_w)), _const_spec((1, a_w))],
        out_specs=out_specs,
        out_shape=out_shape,
        compiler_params=_params("arbitrary"),
        name="ffn_proj",
    )(x, lw["n1"], lw["wg1"], lw["wu1"], lw["wd1"], lw["n_mix"], lw["w_in"], lw["qg"], lw["kg"])
    return outs


def _out_ffn_kernel(x1_ref, a_ref, b_ref, c_ref, wo_ref, n_ref, wg_ref, wu_ref, wd_ref, fn_ref, y_ref,
                    *, fchunk):
    wa, wb = a_ref.shape[1], b_ref.shape[1]
    mix = (_dot(a_ref[...], wo_ref[0:wa, :]) + _dot(b_ref[...], wo_ref[wa:wa + wb, :])
           + _dot(c_ref[...], wo_ref[wa + wb:, :]))
    x2 = x1_ref[...] + mix
    hb = _rmsnorm(x2, n_ref[...]).astype(BF16)
    x3 = x2 + 0.5 * _swiglu(hb, wg_ref, wu_ref, wd_ref, fchunk)
    y_ref[...] = _rmsnorm(x3, fn_ref[...])


def _out_ffn(x1, conv_out, attn_out, ssm_out, lw, *, tm):
    r, d = x1.shape
    d_ff = lw["wg2"].shape[1]
    row = lambda w: pl.BlockSpec((tm, w), lambda i: (i, 0))
    fchunk = 256 if d_ff % 256 == 0 else d_ff
    return pl.pallas_call(
        functools.partial(_out_ffn_kernel, fchunk=fchunk),
        grid=(r // tm,),
        in_specs=[row(d), row(conv_out.shape[1]), row(attn_out.shape[1]), row(ssm_out.shape[1]),
                  _const_spec(lw["w_out"].shape), _const_spec((1, d)), _const_spec((d, d_ff)),
                  _const_spec((d, d_ff)), _const_spec((d_ff, d)), _const_spec((1, d))],
        out_specs=row(d),
        out_shape=jax.ShapeDtypeStruct((r, d), F32),
        compiler_params=_params("arbitrary"),
        name="out_ffn",
    )(x1, conv_out, attn_out, ssm_out, lw["w_out"], lw["n2"], lw["wg2"], lw["wu2"], lw["wd2"], lw["n_fin"])


CONV_HALO = 32
CONV_SUB = 64


def _layernorm(c, g, b, eps=1e-5):
    xc = c - jnp.mean(c, axis=-1, keepdims=True)
    var = jnp.mean(xc * xc, axis=-1, keepdims=True)
    return xc * lax.rsqrt(var + eps) * g + b


def _conv_seq_kernel(u_ref, w_ref, b_ref, g_ref, lb_ref, o_ref, buf_ref, *, taps):
    tc = u_ref.shape[1]

    @pl.when(pl.program_id(1) == 0)
    def _():
        buf_ref[0:CONV_HALO, :] = jnp.zeros((CONV_HALO, buf_ref.shape[1]), F32)

    buf_ref[CONV_HALO:CONV_HALO + tc, :] = u_ref[0]
    first = CONV_HALO - (taps - 1)
    for s in range(tc // CONV_SUB):
        acc = None
        for k in range(taps):
            start = first + s * CONV_SUB + k
            term = w_ref[k:k + 1, :] * buf_ref[start:start + CONV_SUB, :]
            acc = term if acc is None else acc + term
        c = acc + b_ref[...]
        o_ref[0, s * CONV_SUB:(s + 1) * CONV_SUB, :] = _silu(_layernorm(c, g_ref[...], lb_ref[...])).astype(o_ref.dtype)
    buf_ref[0:CONV_HALO, :] = buf_ref[tc:tc + CONV_HALO, :]


def _conv_seq(u, lw, *, tc=256):
    b, t, c = u.shape
    taps = lw["conv_w"].shape[0]
    assert taps - 1 <= CONV_HALO and t % tc == 0 and tc % CONV_SUB == 0
    return pl.pallas_call(
        functools.partial(_conv_seq_kernel, taps=taps),
        grid=(b, t // tc),
        in_specs=[pl.BlockSpec((1, tc, c), lambda i, j: (i, j, 0)),
                  _const_spec((taps, c)), _const_spec((1, c)), _const_spec((1, c)), _const_spec((1, c))],
        out_specs=pl.BlockSpec((1, tc, c), lambda i, j: (i, j, 0)),
        out_shape=jax.ShapeDtypeStruct((b, t, c), BF16),
        scratch_shapes=[pltpu.VMEM((CONV_HALO + tc, c), F32)],
        compiler_params=_params("arbitrary", "arbitrary"),
        name="conv_seq",
    )(u, lw["conv_w"], lw["conv_b"], lw["ln_g"], lw["ln_b"])


def _rel_bias_tile(relb_ref, head, dist):
    max_exact = NUM_BUCKETS // 2
    n = jnp.maximum(dist, 0)
    nf = jnp.maximum(n, 1).astype(F32)
    large = max_exact + (jnp.log(nf / max_exact) / math.log(REL_MAX_DIST / max_exact)
                         * (NUM_BUCKETS - max_exact)).astype(jnp.int32)
    large = jnp.minimum(large, NUM_BUCKETS - 1)
    bucket = jnp.where(n < max_exact, n, large)
    out = jnp.zeros(dist.shape, F32)
    for b in range(NUM_BUCKETS):
        out = jnp.where(bucket == b, relb_ref[b, head], out)
    return out


def _moba_select(gate, n_eligible, lane):
    eligible = lane < n_eligible
    gate = jnp.where(eligible, gate, NEG_INF)
    sel = jnp.zeros(gate.shape, F32)
    for _ in range(MOBA_TOPK):
        mx = jnp.max(gate, axis=1, keepdims=True)
        idx = jnp.min(jnp.where(gate == mx, lane, LANES), axis=1, keepdims=True)
        pick = lane == idx
        sel = jnp.where(pick, 1.0, sel)
        gate = jnp.where(pick, NEG_INF, gate)
    return jnp.where(eligible, sel, 0.0)


def _attn_seq_kernel(relb_ref, q_ref, km_ref, k_ref, v_ref, o_ref,
                     bias_own_ref, bias_prev_ref, acc_ref, *, n_heads):
    bs = MOBA_BLOCK
    width = n_heads * DH
    cur = pl.program_id(1)

    @pl.when((pl.program_id(0) == 0) & (cur == 0))
    def _():
        qi = lax.broadcasted_iota(jnp.int32, (bs, bs), 0)
        ki = lax.broadcasted_iota(jnp.int32, (bs, bs), 1)
        for h in range(n_heads):
            own = _rel_bias_tile(relb_ref, h, qi - ki)
            bias_own_ref[h] = jnp.where(qi >= ki, own, NEG_INF)
            bias_prev_ref[h] = _rel_bias_tile(relb_ref, h, qi - ki + bs)

    q = q_ref[0]
    lane_w = lax.broadcasted_iota(jnp.int32, (bs, width), 1)
    lane = lax.broadcasted_iota(jnp.int32, (bs, LANES), 1)
    km_hi, km_mid, km_lo = _split3(km_ref[0])
    out = jnp.zeros((bs, width), F32)
    for h in range(n_heads):
        in_head = (lane_w >= h * DH) & (lane_w < (h + 1) * DH)
        qh = jnp.where(in_head, q, 0.0)
        q_hi, q_mid, q_lo = _split3(qh)
        gate = (_dot_t(q_hi, km_hi) + _dot_t(q_hi, km_mid) + _dot_t(q_mid, km_hi)
                + _dot_t(q_hi, km_lo) + _dot_t(q_lo, km_hi) + _dot_t(q_mid, km_mid))
        sel = _moba_select(gate, cur, lane)
        sel_mask = jnp.where(sel > 0.0, 0.0, NEG_INF)
        far_bias = relb_ref[NUM_BUCKETS - 1, h]
        qs = (qh * (1.0 / math.sqrt(DH))).astype(BF16)

        def attend(s, m, l, kv_start):
            m_new = jnp.maximum(m, jnp.max(s, axis=1, keepdims=True))
            p = jnp.exp(s - m_new)
            alpha = jnp.exp(m - m_new)
            l_new = alpha * l + jnp.sum(p, axis=1, keepdims=True)
            pv = _dot(p.astype(BF16), v_ref[0, pl.ds(kv_start, bs), :])
            acc_ref[...] = alpha * acc_ref[...] + pv
            return m_new, l_new

        def logits(kv_start):
            return _dot_t(qs, k_ref[0, pl.ds(kv_start, bs), :])

        def sel_col(j):
            return jnp.max(jnp.where(lane == j, sel_mask, NEG_INF), axis=1, keepdims=True)

        acc_ref[...] = jnp.zeros((bs, width), F32)
        own_start = pl.multiple_of(cur * bs, bs)
        m0 = jnp.full((bs, 1), NEG_INF, F32)
        l0 = jnp.zeros((bs, 1), F32)
        m, l = attend(logits(own_start) + bias_own_ref[h], m0, l0, own_start)

        def prev_block(ml):
            start = pl.multiple_of((cur - 1) * bs, bs)
            s = logits(start) + bias_prev_ref[h] + sel_col(cur - 1)
            return attend(s, ml[0], ml[1], start)

        m, l = lax.cond(cur >= 1, prev_block, lambda ml: ml, (m, l))

        def far_block(j, ml):
            start = pl.multiple_of(j * bs, bs)
            s = logits(start) + (sel_col(j) + far_bias)
            return attend(s, ml[0], ml[1], start)

        m, l = lax.fori_loop(0, jnp.maximum(cur - 1, 0), far_block, (m, l))
        out = jnp.where(in_head, acc_ref[...] / l, out)
    o_ref[0] = out.astype(o_ref.dtype)


def _attn_seq(q, kb, vb, kmean, rel_bias, *, n_heads):
    b, t, w = q.shape
    nb = t // MOBA_BLOCK
    assert t % MOBA_BLOCK == 0 and nb <= LANES
    km = jnp.pad(kmean, ((0, 0), (0, LANES - nb), (0, 0)))
    return pl.pallas_call(
        functools.partial(_attn_seq_kernel, n_heads=n_heads),
        grid=(b, nb),
        in_specs=[pl.BlockSpec(memory_space=pltpu.SMEM),
                  pl.BlockSpec((1, MOBA_BLOCK, w), lambda i, j: (i, j, 0)),
                  pl.BlockSpec((1, LANES, w), lambda i, j: (i, 0, 0)),
                  pl.BlockSpec((1, t, w), lambda i, j: (i, 0, 0)),
                  pl.BlockSpec((1, t, w), lambda i, j: (i, 0, 0))],
        out_specs=pl.BlockSpec((1, MOBA_BLOCK, w), lambda i, j: (i, j, 0)),
        out_shape=jax.ShapeDtypeStruct((b, t, w), BF16),
        scratch_shapes=[pltpu.VMEM((n_heads, MOBA_BLOCK, MOBA_BLOCK), F32),
                        pltpu.VMEM((n_heads, MOBA_BLOCK, MOBA_BLOCK), F32),
                        pltpu.VMEM((MOBA_BLOCK, w), F32)],
        compiler_params=_params("arbitrary", "arbitrary"),
        name="attn_seq",
    )(rel_bias, q, km, kb, vb)


SSM_HALO = 8


def _softplus(x):
    return jnp.maximum(x, 0.0) + jnp.log1p(jnp.exp(-jnp.abs(x)))


def _ssd_seq_kernel(xbc_ref, z_ref, dt_ref, cw_ref, cb_ref, dtb_ref, alog_ref, dvec_ref, ng_ref,
                    y_ref, hfin_ref, buf_ref, state_ref, *, n_heads):
    lc = SSD_CHUNK
    n = SSM_N
    d_inner = n_heads * SSM_P
    pair_w = 2 * SSM_P
    heads_per_group = n_heads // SSM_G
    chunk = pl.program_id(1)

    @pl.when(chunk == 0)
    def _():
        buf_ref[0:SSM_HALO, :] = jnp.zeros((SSM_HALO, buf_ref.shape[1]), F32)
        state_ref[...] = jnp.zeros(state_ref.shape, F32)

    buf_ref[SSM_HALO:SSM_HALO + lc, :] = xbc_ref[0]
    taps = cw_ref.shape[0]
    first = SSM_HALO - (taps - 1)
    acc = None
    for k in range(taps):
        term = cw_ref[k:k + 1, :] * buf_ref[first + k:first + k + lc, :]
        acc = term if acc is None else acc + term
    xc = _silu(acc + cb_ref[...])
    buf_ref[0:SSM_HALO, :] = buf_ref[lc:lc + SSM_HALO, :]
    xs = xc[:, :d_inner]
    bm = xc[:, d_inner:d_inner + SSM_G * n]
    cm = xc[:, d_inner + SSM_G * n:]

    dt = _softplus(dt_ref[0] + dtb_ref[...])
    dta = dt * (-jnp.exp(alog_ref[...]))
    ri = lax.broadcasted_iota(jnp.int32, (lc, lc), 0)
    ci = lax.broadcasted_iota(jnp.int32, (lc, lc), 1)
    causal = ri >= ci
    acum = _dot_exact_lhs(jnp.where(causal, 1.0, 0.0).astype(BF16), dta)
    acum_t = acum.T
    er = lax.broadcasted_iota(jnp.int32, (LANES, d_inner), 0)
    ec = lax.broadcasted_iota(jnp.int32, (LANES, d_inner), 1) // SSM_P
    dt_x = _dot_exact_rhs(dt, jnp.where(er == ec, 1.0, 0.0).astype(BF16))
    xdt = xs * dt_x
    xdt_b = xdt.astype(BF16)

    cb = []
    for g in range(SSM_G):
        cg = cm[:, g * n:(g + 1) * n].astype(BF16)
        bg = bm[:, g * n:(g + 1) * n].astype(BF16)
        cb.append(_dot_t(cg, bg))

    lane_p = lax.broadcasted_iota(jnp.int32, (lc, pair_w), 1)
    row_p = lax.broadcasted_iota(jnp.int32, (pair_w, n), 0)
    y_tiles = []
    for pair in range(n_heads // 2):
        sl = slice(pair * pair_w, (pair + 1) * pair_w)
        x_pair = xdt_b[:, sl]
        x_pair_t = xdt[:, sl].T.astype(BF16)
        st_old = state_ref[sl, :]
        st_old_b = st_old.astype(BF16)
        y_intra = None
        y_inter = []
        st_new = []
        for half in range(2):
            h = 2 * pair + half
            g = h // heads_per_group
            col = acum[:, h:h + 1]
            row = acum_t[h:h + 1, :]
            last = acum[lc - 1:lc, h:h + 1]
            decay = jnp.exp(jnp.where(causal, col - row, NEG_INF))
            scores = (cb[g] * decay).astype(BF16)
            in_half = (lane_p >= half * SSM_P) & (lane_p < (half + 1) * SSM_P)
            part = _dot(scores, jnp.where(in_half, x_pair, jnp.zeros_like(x_pair)))
            y_intra = part if y_intra is None else y_intra + part
            cw = (cm[:, g * n:(g + 1) * n] * jnp.exp(col)).astype(BF16)
            y_inter.append(_dot_t(cw, st_old_b))
            bw = (bm[:, g * n:(g + 1) * n] * jnp.exp(last - col)).astype(BF16)
            st_new.append(st_old * jnp.exp(last) + _dot(x_pair_t, bw))
        y_tiles.append(y_intra + jnp.where(lane_p < SSM_P, y_inter[0], y_inter[1]))
        state_ref[sl, :] = jnp.where(row_p < SSM_P, st_new[0], st_new[1])
    y = jnp.concatenate(y_tiles, axis=1) + dvec_ref[...] * xs
    gated = y * _silu(z_ref[0])
    y_ref[0] = _rmsnorm(gated, ng_ref[...]).astype(y_ref.dtype)

    @pl.when(chunk == pl.num_programs(1) - 1)
    def _():
        hfin_ref[0] = state_ref[...]


def _ssd_seq(xbc, z, dt, lw, *, n_heads):
    b, t, cd = xbc.shape
    d_inner = z.shape[2]
    lc = SSD_CHUNK
    assert t % lc == 0 and n_heads % 2 == 0 and n_heads % SSM_G == 0 and n_heads <= LANES
    taps = lw["ssm_cw"].shape[0]
    assert taps - 1 <= SSM_HALO
    tile = lambda w: pl.BlockSpec((1, lc, w), lambda i, j: (i, j, 0))
    return pl.pallas_call(
        functools.partial(_ssd_seq_kernel, n_heads=n_heads),
        grid=(b, t // lc),
        in_specs=[tile(cd), tile(d_inner), tile(LANES),
                  _const_spec((taps, cd)), _const_spec((1, cd)), _const_spec((1, LANES)),
                  _const_spec((1, LANES)), _const_spec((1, d_inner)), _const_spec((1, d_inner))],
        out_specs=[tile(d_inner), pl.BlockSpec((1, d_inner, SSM_N), lambda i, j: (i, 0, 0))],
        out_shape=[jax.ShapeDtypeStruct((b, t, d_inner), BF16),
                   jax.ShapeDtypeStruct((b, d_inner, SSM_N), F32)],
        scratch_shapes=[pltpu.VMEM((SSM_HALO + lc, cd), F32), pltpu.VMEM((d_inner, SSM_N), F32)],
        compiler_params=_params("arbitrary", "arbitrary"),
        name="ssd_seq",
    )(xbc, z, dt, lw["ssm_cw"], lw["ssm_cb"], lw["dt_bias"], lw["a_log"], lw["d_vec"], lw["ssm_ng"])


def _layer_weights(l, w):
    d_model = w["w_in"].shape[1]
    c_conv = w["conv_dw_w"].shape[2]
    a_w = (w["w_out"].shape[1] - c_conv - w["ssm_norm_g"].shape[1])
    d_inner = w["ssm_norm_g"].shape[1]
    cd = w["ssm_conv_w"].shape[2]
    n_ssm_heads = w["ssm_dt_bias"].shape[1]
    off_q = 2 * c_conv
    off_k = off_q + a_w
    off_v = off_k + a_w
    off_z = off_v + a_w
    off_xbc = off_z + d_inner
    off_dt = off_xbc + cd
    n_in = off_dt + n_ssm_heads
    assert w["w_in"].shape[2] == n_in
    n_pad = off_dt + LANES
    row = lambda a: a[l].reshape(1, -1).astype(F32)
    pad_row = lambda a: jnp.pad(a[l].astype(F32), (0, LANES - a.shape[1])).reshape(1, LANES)
    return dict(
        offs=(c_conv, off_q, off_k, off_v, off_z, off_xbc, off_dt, n_pad),
        n1=row(w["ffn1_norm"]), wg1=w["ffn1_wg"][l].astype(BF16), wu1=w["ffn1_wu"][l].astype(BF16),
        wd1=w["ffn1_wd"][l].astype(BF16), n_mix=row(w["mix_norm"]),
        w_in=jnp.pad(w["w_in"][l], ((0, 0), (0, n_pad - n_in))).astype(BF16),
        w_out=w["w_out"][l].astype(BF16),
        qg=jnp.tile(w["q_norm_g"][l], a_w // DH).reshape(1, a_w),
        kg=jnp.tile(w["k_norm_g"][l], a_w // DH).reshape(1, a_w),
        conv_w=w["conv_dw_w"][l], conv_b=row(w["conv_dw_b"]), ln_g=row(w["conv_ln_g"]), ln_b=row(w["conv_ln_b"]),
        ssm_cw=w["ssm_conv_w"][l], ssm_cb=row(w["ssm_conv_b"]),
        dt_bias=pad_row(w["ssm_dt_bias"]), a_log=pad_row(w["ssm_a_log"]),
        d_vec=jnp.repeat(w["ssm_d"][l], SSM_P).reshape(1, d_inner), ssm_ng=row(w["ssm_norm_g"]),
        n2=row(w["ffn2_norm"]), wg2=w["ffn2_wg"][l].astype(BF16), wu2=w["ffn2_wu"][l].astype(BF16),
        wd2=w["ffn2_wd"][l].astype(BF16), n_fin=row(w["final_norm"]),
        n_attn_heads=a_w // DH, n_ssm_heads=n_ssm_heads,
    )


def _row_tile(rows, want=512):
    tm = min(want, rows)
    assert rows % tm == 0 and tm % SUBLANES == 0
    return tm


def _layer_seq(x, lw, rel_bias):
    b, t, d = x.shape
    n_ah, n_sh = lw["n_attn_heads"], lw["n_ssm_heads"]
    tm = _row_tile(b * t)
    assert t % tm == 0 and tm % MOBA_BLOCK == 0
    x1, u, q, k, v, kb, vb, z, xbc, dt, km = _ffn_proj(x.reshape(b * t, d), lw, tm=tm, emit_kmean=True)
    seq = lambda a: a.reshape(b, t, a.shape[-1])
    u, xbc = seq(u), seq(xbc)
    conv_out = _conv_seq(u, lw)
    attn_out = _attn_seq(seq(q), seq(kb), seq(vb), km.reshape(b, t // MOBA_BLOCK, -1), rel_bias, n_heads=n_ah)
    ssm_out, h_fin = _ssd_seq(xbc, seq(z), seq(dt), lw, n_heads=n_sh)
    flat = lambda a: a.reshape(b * t, a.shape[-1])
    y = _out_ffn(x1, flat(conv_out), flat(attn_out), flat(ssm_out), lw, tm=tm)
    conv_taps = lw["conv_w"].shape[0]
    ssm_taps = lw["ssm_cw"].shape[0]
    states = (k.reshape(b, t, n_ah, DH), v.reshape(b, t, n_ah, DH), u[:, t - (conv_taps - 1):],
              xbc[:, t - (ssm_taps - 1):], h_fin.reshape(b, n_sh, SSM_P, SSM_N))
    return y.reshape(b, t, d), states


def _conv_step_kernel(u_ref, st_ref, w_ref, b_ref, g_ref, lb_ref, o_ref):
    past = st_ref.shape[0]
    acc = w_ref[past:past + 1, :] * u_ref[...]
    for k in range(past):
        acc = acc + w_ref[k:k + 1, :] * st_ref[k]
    o_ref[...] = _silu(_layernorm(acc + b_ref[...], g_ref[...], lb_ref[...])).astype(o_ref.dtype)


def _conv_step(u, state_t, lw):
    return pl.pallas_call(
        _conv_step_kernel,
        out_shape=jax.ShapeDtypeStruct(u.shape, BF16),
        compiler_params=pltpu.CompilerParams(vmem_limit_bytes=VMEM_LIMIT_BYTES),
        name="conv_step",
    )(u, state_t, lw["conv_w"], lw["conv_b"], lw["ln_g"], lw["ln_b"])


SSD_STEP_ROWS = 8


def _ssd_step_kernel(xn_ref, st_ref, z_ref, dt_ref, h0_ref, cw_ref, cb_ref, dtb_ref, alog_ref, dvec_ref, ng_ref,
                     y_ref, h1_ref, *, n_heads):
    rows = xn_ref.shape[0]
    n = SSM_N
    d_inner = n_heads * SSM_P
    group_rows = (n_heads // SSM_G) * SSM_P
    past = st_ref.shape[0]
    acc = cw_ref[past:past + 1, :] * xn_ref[...]
    for k in range(past):
        acc = acc + cw_ref[k:k + 1, :] * st_ref[k]
    xc = _silu(acc + cb_ref[...])
    xs = xc[:, :d_inner]
    bm = xc[:, d_inner:d_inner + SSM_G * n]
    cm = xc[:, d_inner + SSM_G * n:]
    dt = _softplus(dt_ref[...] + dtb_ref[...])
    dta = dt * (-jnp.exp(alog_ref[...]))
    er = lax.broadcasted_iota(jnp.int32, (LANES, d_inner), 0)
    ec = lax.broadcasted_iota(jnp.int32, (LANES, d_inner), 1) // SSM_P
    expand = jnp.where(er == ec, 1.0, 0.0).astype(BF16)
    xdt = xs * _dot_exact_rhs(dt, expand)
    dec = jnp.exp(_dot_exact_rhs(dta, expand))
    pad = jnp.zeros((LANES - rows, d_inner), F32)
    xdt_t = jnp.concatenate([xdt, pad], axis=0).T
    dec_t = jnp.concatenate([dec, pad], axis=0).T
    lane = lax.broadcasted_iota(jnp.int32, (d_inner, LANES), 1)
    y_t = jnp.zeros((d_inner, LANES), F32)
    for b in range(rows):
        xcol = xdt_t[:, b:b + 1]
        dcol = dec_t[:, b:b + 1]
        ycols = []
        for g in range(SSM_G):
            rs = slice(g * group_rows, (g + 1) * group_rows)
            h0 = h0_ref[b, rs, :]
            brow = bm[b:b + 1, g * n:(g + 1) * n]
            crow = cm[b:b + 1, g * n:(g + 1) * n]
            cb = jnp.sum(crow * brow, axis=1, keepdims=True)
            h1_ref[b, rs, :] = h0 * dcol[rs] + xcol[rs] * brow
            ycols.append(jnp.sum(h0 * crow, axis=1, keepdims=True) * dcol[rs] + cb * xcol[rs])
        y_t = jnp.where(lane == b, jnp.concatenate(ycols, axis=0), y_t)
    y = y_t.T[:rows] + dvec_ref[...] * xs
    gated = y * _silu(z_ref[...])
    y_ref[...] = _rmsnorm(gated, ng_ref[...]).astype(y_ref.dtype)


def _ssd_step(xn, state_t, z, dt, h0, lw, *, n_heads):
    db, cd = xn.shape
    d_inner = z.shape[1]
    rows = SSD_STEP_ROWS
    assert db % rows == 0
    past = state_t.shape[0]
    row = lambda w: pl.BlockSpec((rows, w), lambda i: (i, 0))
    return pl.pallas_call(
        functools.partial(_ssd_step_kernel, n_heads=n_heads),
        grid=(db // rows,),
        in_specs=[row(cd), pl.BlockSpec((past, rows, cd), lambda i: (0, i, 0)), row(d_inner), row(LANES),
                  pl.BlockSpec((rows, d_inner, SSM_N), lambda i: (i, 0, 0)),
                  _const_spec((past + 1, cd)), _const_spec((1, cd)), _const_spec((1, LANES)),
                  _const_spec((1, LANES)), _const_spec((1, d_inner)), _const_spec((1, d_inner))],
        out_specs=[row(d_inner), pl.BlockSpec((rows, d_inner, SSM_N), lambda i: (i, 0, 0))],
        out_shape=[jax.ShapeDtypeStruct((db, d_inner), BF16), jax.ShapeDtypeStruct(h0.shape, F32)],
        compiler_params=_params("arbitrary"),
        name="ssd_step",
    )(xn, state_t, z, dt, h0, lw["ssm_cw"], lw["ssm_cb"], lw["dt_bias"], lw["a_log"], lw["d_vec"], lw["ssm_ng"])


KMEAN_PAGES = 16


def _kmean_pages_kernel(pt_ref, *refs):
    del pt_ref
    page_refs, o_ref = refs[:-1], refs[-1]
    per_block = MOBA_BLOCK // PAGE_SIZE
    for blk in range(len(page_refs) // per_block):
        s = None
        for r in range(per_block):
            part = jnp.sum(page_refs[blk * per_block + r][0], axis=0, keepdims=True)
            s = part if s is None else s + part
        o_ref[0, blk:blk + 1, :] = s * (1.0 / MOBA_BLOCK)


def _kmean_pages(cache, page_table_flat, db, n_pages):
    w = cache.shape[2]
    pg = KMEAN_PAGES
    per_block = MOBA_BLOCK // PAGE_SIZE
    assert n_pages % pg == 0 and pg % per_block == 0 and (pg // per_block) % SUBLANES == 0

    def page_spec(j):
        return pl.BlockSpec((1, PAGE_SIZE, w), lambda b, i, pt: (pt[b * n_pages + i * pg + j], 0, 0))

    return pl.pallas_call(
        _kmean_pages_kernel,
        grid_spec=pltpu.PrefetchScalarGridSpec(
            num_scalar_prefetch=1, grid=(db, n_pages // pg),
            in_specs=[page_spec(j) for j in range(pg)],
            out_specs=pl.BlockSpec((1, pg // per_block, w), lambda b, i, pt: (b, i, 0))),
        out_shape=jax.ShapeDtypeStruct((db, n_pages // per_block, w), F32),
        compiler_params=_params("arbitrary", "arbitrary"),
        name="kmean_pages",
    )(page_table_flat, *([cache] * pg))


def _gate_step_kernel(q_ref, km_ref, idx_ref):
    db, nb, w = km_ref.shape
    prod = (km_ref[...] * q_ref[...]).reshape(db * nb, w)
    sr = lax.broadcasted_iota(jnp.int32, (w, LANES), 0) // DH
    sc = lax.broadcasted_iota(jnp.int32, (w, LANES), 1)
    gate = _dot_exact_rhs(prod, jnp.where(sr == sc, 1.0, 0.0).astype(BF16)).reshape(db, nb, LANES)
    blk = lax.broadcasted_iota(jnp.int32, (db, nb, LANES), 1)
    picks = []
    for _ in range(MOBA_TOPK):
        mx = jnp.max(gate, axis=1, keepdims=True)
        idx = jnp.min(jnp.where(gate == mx, blk, nb), axis=1, keepdims=True)
        picks.append(idx)
        gate = jnp.where(blk == idx, NEG_INF, gate)
    picks += [jnp.zeros((db, 1, LANES), jnp.int32)] * (SUBLANES - MOBA_TOPK)
    idx_ref[...] = jnp.concatenate(picks, axis=1)


def _gate_step(q3, kmean):
    db = q3.shape[0]
    return pl.pallas_call(
        _gate_step_kernel,
        out_shape=jax.ShapeDtypeStruct((db, SUBLANES, LANES), jnp.int32),
        compiler_params=pltpu.CompilerParams(vmem_limit_bytes=VMEM_LIMIT_BYTES),
        name="gate_step",
    )(q3, kmean)


def _attn_step_kernel(pt_ref, sel_ref, relb_ref, q_ref, kn_ref, vn_ref, *refs, n_heads, past_len):
    del pt_ref
    per_block = MOBA_BLOCK // PAGE_SIZE
    n_pg = MOBA_TOPK * per_block
    k_pages, v_pages, o_ref = refs[:n_pg], refs[n_pg:2 * n_pg], refs[2 * n_pg]
    b, h = pl.program_id(0), pl.program_id(1)
    w = q_ref.shape[2]
    lane_w = lax.broadcasted_iota(jnp.int32, (1, w), 1)
    in_head = (lane_w >= h * DH) & (lane_w < (h + 1) * DH)
    qh = jnp.where(in_head, q_ref[0], 0.0) * (1.0 / math.sqrt(DH))
    q8 = jnp.broadcast_to(qh, (SUBLANES, w)).astype(BF16)
    key_off = lax.broadcasted_iota(jnp.int32, (1, MOBA_BLOCK), 1)
    s_own = jnp.sum(qh * kn_ref[0], axis=1, keepdims=True) + relb_ref[0, h]
    logits = []
    for j in range(MOBA_TOPK):
        blk = sel_ref[(b * n_heads + h) * MOBA_TOPK + j]
        kj = jnp.concatenate([k_pages[j * per_block + r][0] for r in range(per_block)], axis=0).astype(BF16)
        dist = past_len - (blk * MOBA_BLOCK + key_off)
        logits.append(_dot_t(q8, kj)[0:1, :] + _rel_bias_tile(relb_ref, h, dist))
    m = s_own
    for s in logits:
        m = jnp.maximum(m, jnp.max(s, axis=1, keepdims=True))
    p_own = jnp.exp(s_own - m)
    l = p_own
    acc = p_own * vn_ref[0]
    for j, s in enumerate(logits):
        p = jnp.exp(s - m)
        l = l + jnp.sum(p, axis=1, keepdims=True)
        vj = jnp.concatenate([v_pages[j * per_block + r][0] for r in range(per_block)], axis=0).astype(BF16)
        acc = acc + _dot(jnp.broadcast_to(p, (SUBLANES, MOBA_BLOCK)).astype(BF16), vj)[0:1, :]
    o_ref[0, 0] = jnp.broadcast_to(acc / l, (SUBLANES, w))


def _attn_step(q3, kn3, vn3, cache_k, cache_v, page_table_flat, sel_flat, rel_bias, *, n_heads, n_pages):
    db, _, w = q3.shape
    per_block = MOBA_BLOCK // PAGE_SIZE
    past_len = n_pages * PAGE_SIZE
    assert past_len % MOBA_BLOCK == 0 and past_len // MOBA_BLOCK >= MOBA_TOPK

    def page_spec(j, r):
        def index_map(b, h, pt, sel):
            blk = sel[(b * n_heads + h) * MOBA_TOPK + j]
            return (pt[b * n_pages + blk * per_block + r], 0, 0)
        return pl.BlockSpec((1, PAGE_SIZE, w), index_map)

    page_specs = [page_spec(j, r) for j in range(MOBA_TOPK) for r in range(per_block)]
    tok = pl.BlockSpec((1, 1, w), lambda b, h, pt, sel: (b, 0, 0))
    n_pg = len(page_specs)
    out = pl.pallas_call(
        functools.partial(_attn_step_kernel, n_heads=n_heads, past_len=past_len),
        grid_spec=pltpu.PrefetchScalarGridSpec(
            num_scalar_prefetch=2, grid=(db, n_heads),
            in_specs=[pl.BlockSpec(memory_space=pltpu.SMEM), tok, tok, tok] + page_specs + page_specs,
            out_specs=pl.BlockSpec((1, 1, SUBLANES, w), lambda b, h, pt, sel: (b, h, 0, 0))),
        out_shape=jax.ShapeDtypeStruct((db, n_heads, SUBLANES, w), F32),
        compiler_params=_params("arbitrary", "arbitrary"),
        name="attn_step",
    )(page_table_flat, sel_flat, rel_bias, q3, kn3, vn3, *([cache_k] * n_pg), *([cache_v] * n_pg))
    heads = [out[:, h, 0, h * DH:(h + 1) * DH] for h in range(n_heads)]
    return jnp.concatenate(heads, axis=1)


def _layer_step(x, conv_state, ssm_conv_state, ssm_state, cache_k, cache_v, page_table, lw, rel_bias):
    db, t, d = x.shape
    assert t == 1
    n_ah, n_sh = lw["n_attn_heads"], lw["n_ssm_heads"]
    tm = _row_tile(db)
    x1, u, q, k, v, _, _, z, xbc, dt = _ffn_proj(x.reshape(db, d), lw, tm=tm, emit_kmean=False)
    conv_out = _conv_step(u, conv_state.transpose(1, 0, 2), lw)
    n_pages = page_table.shape[1]
    w = q.shape[1]
    pt_flat = page_table.reshape(-1)
    pool_k = cache_k.reshape(cache_k.shape[0], PAGE_SIZE, w)
    pool_v = cache_v.reshape(cache_v.shape[0], PAGE_SIZE, w)
    kmean = _kmean_pages(pool_k, pt_flat, db, n_pages)
    q3 = q.reshape(db, 1, w)
    picks = _gate_step(q3, kmean)
    sel_flat = picks[:, :MOBA_TOPK, :n_ah].transpose(0, 2, 1).reshape(-1)
    attn_out = _attn_step(q3, k.reshape(db, 1, w), v.reshape(db, 1, w), pool_k, pool_v, pt_flat, sel_flat,
                          rel_bias, n_heads=n_ah, n_pages=n_pages)
    ssm_out, h_new = _ssd_step(xbc, ssm_conv_state.transpose(1, 0, 2), z, dt,
                               ssm_state.reshape(db, n_sh * SSM_P, SSM_N), lw, n_heads=n_sh)
    y = _out_ffn(x1, conv_out, attn_out.astype(BF16), ssm_out, lw, tm=tm)
    states = (k.reshape(db, 1, n_ah, DH), v.reshape(db, 1, n_ah, DH),
              jnp.concatenate([conv_state[:, 1:], u[:, None, :]], axis=1),
              jnp.concatenate([ssm_conv_state[:, 1:], xbc[:, None, :]], axis=1),
              h_new.reshape(ssm_state.shape))
    return y.reshape(db, 1, d), states


def kernel(x_prompt, x_sample, cache_k, cache_v, state_conv, state_ssm_conv, state_ssm, page_table, rel_bias, ffn1_norm, ffn1_wg, ffn1_wu, ffn1_wd, mix_norm, w_in, w_out, conv_dw_w, conv_dw_b, conv_ln_g, conv_ln_b, q_norm_g, k_norm_g, ssm_conv_w, ssm_conv_b, ssm_dt_bias, ssm_a_log, ssm_d, ssm_norm_g, ffn2_norm, ffn2_wg, ffn2_wu, ffn2_wd, final_norm):
    w = dict(ffn1_norm=ffn1_norm, ffn1_wg=ffn1_wg, ffn1_wu=ffn1_wu, ffn1_wd=ffn1_wd, mix_norm=mix_norm,
             w_in=w_in, w_out=w_out, conv_dw_w=conv_dw_w, conv_dw_b=conv_dw_b, conv_ln_g=conv_ln_g,
             conv_ln_b=conv_ln_b, q_norm_g=q_norm_g, k_norm_g=k_norm_g, ssm_conv_w=ssm_conv_w,
             ssm_conv_b=ssm_conv_b, ssm_dt_bias=ssm_dt_bias, ssm_a_log=ssm_a_log, ssm_d=ssm_d,
             ssm_norm_g=ssm_norm_g, ffn2_norm=ffn2_norm, ffn2_wg=ffn2_wg, ffn2_wu=ffn2_wu, ffn2_wd=ffn2_wd,
             final_norm=final_norm)
    depth = w_in.shape[0]
    y_p, y_s = x_prompt, x_sample
    st_p, st_s = [], []
    for l in range(depth):
        lw = _layer_weights(l, w)
        y_p, st = _layer_seq(y_p, lw, rel_bias)
        st_p.append(st)
        y_s, st = _layer_step(y_s, state_conv[l], state_ssm_conv[l], state_ssm[l], cache_k[l], cache_v[l],
                              page_table, lw, rel_bias)
        st_s.append(st)
    stack = lambda sts, i: jnp.stack([s[i] for s in sts])
    return (y_p, y_s) + tuple(stack(st_p, i) for i in range(5)) + tuple(stack(st_s, i) for i in range(5))
```

```python
import functools
import math

import jax
import jax.numpy as jnp
from jax import lax
from jax.experimental import pallas as pl
from jax.experimental.pallas import tpu as pltpu

F32 = jnp.float32
BF16 = jnp.bfloat16

DH = 64
MOBA_BLOCK = 256
MOBA_TOPK = 3
NUM_BUCKETS = 32
REL_MAX_DIST = 128
SSM_P = 64
SSM_G = 2
SSM_N = 128
SSD_CHUNK = 256
PAGE_SIZE = 128

LANES = 128
SUBLANES = 8
VMEM_LIMIT_BYTES = 56 * 1024 * 1024

NEG_INF = float("-inf")


def _params(*semantics):
    return pltpu.CompilerParams(dimension_semantics=semantics, vmem_limit_bytes=VMEM_LIMIT_BYTES)


def _dot(a, b):
    return jnp.dot(a, b, preferred_element_type=F32)


def _dot_t(a, b):
    return lax.dot_general(a, b, (((1,), (1,)), ((), ())), preferred_element_type=F32)


def _split3(x):
    hi = x.astype(BF16)
    r1 = x - hi.astype(F32)
    mid = r1.astype(BF16)
    lo = (r1 - mid.astype(F32)).astype(BF16)
    return hi, mid, lo


def _dot_exact_rhs(x, sel):
    hi, mid, lo = _split3(x)
    return _dot(hi, sel) + _dot(mid, sel) + _dot(lo, sel)


def _dot_exact_lhs(sel, x):
    hi, mid, lo = _split3(x)
    return _dot(sel, hi) + _dot(sel, mid) + _dot(sel, lo)


def _sigmoid(x):
    return 1.0 / (1.0 + jnp.exp(-x))


def _silu(x):
    return x * _sigmoid(x)


def _rmsnorm(x, g, eps=1e-6):
    return x * lax.rsqrt(jnp.mean(x * x, axis=-1, keepdims=True) + eps) * g


def _const_spec(shape):
    nd = len(shape)
    return pl.BlockSpec(shape, lambda *_: (0,) * nd, pipeline_mode=pl.Buffered(1))


def _swiglu(hb, wg_ref, wu_ref, wd_ref, fchunk):
    d_ff = wg_ref.shape[1]
    acc = None
    for c in range(d_ff // fchunk):
        sl = slice(c * fchunk, (c + 1) * fchunk)
        g = _dot(hb, wg_ref[:, sl])
        u = _dot(hb, wu_ref[:, sl])
        a = (_silu(g) * u).astype(BF16)
        part = _dot(a, wd_ref[sl, :])
        acc = part if acc is None else acc + part
    return acc


def _head_sumsq_matrix(width):
    r = lax.broadcasted_iota(jnp.int32, (width, width), 0) // DH
    c = lax.broadcasted_iota(jnp.int32, (width, width), 1) // DH
    return jnp.where(r == c, 1.0, 0.0).astype(BF16)


def _head_rmsnorm(p, g_row, eps=1e-6):
    ss = _dot_exact_rhs(p * p, _head_sumsq_matrix(p.shape[1]))
    return p * lax.rsqrt(ss * (1.0 / DH) + eps) * g_row


def _ffn_proj_kernel(x_ref, n1_ref, wg_ref, wu_ref, wd_ref, n2_ref, win_ref, qg_ref, kg_ref,
                     x1_ref, u_ref, q_ref, k_ref, v_ref, z_ref, xbc_ref, dt_ref,
                     *maybe_attn_refs, fchunk, offs):
    c_conv, off_q, off_k, off_v, off_z, off_xbc, off_dt, n_pad = offs
    x = x_ref[...]
    hb = _rmsnorm(x, n1_ref[...]).astype(BF16)
    x1 = x + 0.5 * _swiglu(hb, wg_ref, wu_ref, wd_ref, fchunk)
    x1_ref[...] = x1
    h2 = _rmsnorm(x1, n2_ref[...]).astype(BF16)
    pg = _dot(h2, win_ref[:, 0:off_q])
    u_ref[...] = pg[:, :c_conv] * _sigmoid(pg[:, c_conv:])
    q_ref[...] = _head_rmsnorm(_dot(h2, win_ref[:, off_q:off_k]), qg_ref[...])
    k = _head_rmsnorm(_dot(h2, win_ref[:, off_k:off_v]), kg_ref[...])
    k_ref[...] = k
    v = _dot(h2, win_ref[:, off_v:off_z])
    v_ref[...] = v
    z_ref[...] = _dot(h2, win_ref[:, off_z:off_xbc])
    xbc_ref[...] = _dot(h2, win_ref[:, off_xbc:off_dt])
    dt_ref[...] = _dot(h2, win_ref[:, off_dt:n_pad])
    if maybe_attn_refs:
        kb_ref, vt_ref, km_ref = maybe_attn_refs
        kb_ref[...] = k.astype(BF16)
        vt_ref[...] = v.T.astype(BF16)
        nb = k.shape[0] // MOBA_BLOCK
        km_ref[0] = jnp.mean(k.reshape(nb, MOBA_BLOCK, k.shape[1]), axis=1)


def _ffn_proj(x, lw, *, tm, for_seq_attn):
    r, d = x.shape
    d_ff = lw["wg1"].shape[1]
    offs = lw["offs"]
    c_conv, off_q, off_k, off_v, off_z, off_xbc, off_dt, n_pad = offs
    a_w = off_k - off_q
    widths = dict(u=c_conv, q=a_w, k=a_w, v=a_w, z=off_xbc - off_z, xbc=off_dt - off_xbc, dt=n_pad - off_dt)
    row = lambda w: pl.BlockSpec((tm, w), lambda i: (i, 0))
    out_shape = [jax.ShapeDtypeStruct((r, d), F32),
                 jax.ShapeDtypeStruct((r, widths["u"]), F32),
                 jax.ShapeDtypeStruct((r, a_w), F32),
                 jax.ShapeDtypeStruct((r, a_w), F32),
                 jax.ShapeDtypeStruct((r, a_w), F32),
                 jax.ShapeDtypeStruct((r, widths["z"]), F32),
                 jax.ShapeDtypeStruct((r, widths["xbc"]), F32),
                 jax.ShapeDtypeStruct((r, widths["dt"]), F32)]
    out_specs = [row(d), row(widths["u"]), row(a_w), row(a_w), row(a_w),
                 row(widths["z"]), row(widths["xbc"]), row(widths["dt"])]
    if for_seq_attn:
        nb = tm // MOBA_BLOCK
        out_shape += [jax.ShapeDtypeStruct((r, a_w), BF16), jax.ShapeDtypeStruct((a_w, r), BF16),
                      jax.ShapeDtypeStruct((r // tm, nb, a_w), F32)]
        out_specs += [row(a_w), pl.BlockSpec((a_w, tm), lambda i: (0, i)),
                      pl.BlockSpec((1, nb, a_w), lambda i: (i, 0, 0))]
    fchunk = 256 if d_ff % 256 == 0 else d_ff
    outs = pl.pallas_call(
        functools.partial(_ffn_proj_kernel, fchunk=fchunk, offs=offs),
        grid=(r // tm,),
        in_specs=[row(d), _const_spec((1, d)), _const_spec((d, d_ff)), _const_spec((d, d_ff)),
                  _const_spec((d_ff, d)), _const_spec((1, d)), _const_spec((d, n_pad)),
                  _const_spec((1, a_w)), _const_spec((1, a_w))],
        out_specs=out_specs,
        out_shape=out_shape,
        compiler_params=_params("arbitrary"),
        name="ffn_proj",
    )(x, lw["n1"], lw["wg1"], lw["wu1"], lw["wd1"], lw["n_mix"], lw["w_in"], lw["qg"], lw["kg"])
    return outs


def _out_ffn_kernel(x1_ref, a_ref, b_ref, c_ref, wo_ref, n_ref, wg_ref, wu_ref, wd_ref, fn_ref, y_ref,
                    *, fchunk):
    wa, wb = a_ref.shape[1], b_ref.shape[1]
    mix = (_dot(a_ref[...], wo_ref[0:wa, :]) + _dot(b_ref[...], wo_ref[wa:wa + wb, :])
           + _dot(c_ref[...], wo_ref[wa + wb:, :]))
    x2 = x1_ref[...] + mix
    hb = _rmsnorm(x2, n_ref[...]).astype(BF16)
    x3 = x2 + 0.5 * _swiglu(hb, wg_ref, wu_ref, wd_ref, fchunk)
    y_ref[...] = _rmsnorm(x3, fn_ref[...])


def _out_ffn(x1, conv_out, attn_out, ssm_out, lw, *, tm):
    r, d = x1.shape
    d_ff = lw["wg2"].shape[1]
    row = lambda w: pl.BlockSpec((tm, w), lambda i: (i, 0))
    fchunk = 256 if d_ff % 256 == 0 else d_ff
    return pl.pallas_call(
        functools.partial(_out_ffn_kernel, fchunk=fchunk),
        grid=(r // tm,),
        in_specs=[row(d), row(conv_out.shape[1]), row(attn_out.shape[1]), row(ssm_out.shape[1]),
                  _const_spec(lw["w_out"].shape), _const_spec((1, d)), _const_spec((d, d_ff)),
                  _const_spec((d, d_ff)), _const_spec((d_ff, d)), _const_spec((1, d))],
        out_specs=row(d),
        out_shape=jax.ShapeDtypeStruct((r, d), F32),
        compiler_params=_params("arbitrary"),
        name="out_ffn",
    )(x1, conv_out, attn_out, ssm_out, lw["w_out"], lw["n2"], lw["wg2"], lw["wu2"], lw["wd2"], lw["n_fin"])


CONV_HALO = 32
CONV_SUB = 64


def _layernorm(c, g, b, eps=1e-5):
    xc = c - jnp.mean(c, axis=-1, keepdims=True)
    var = jnp.mean(xc * xc, axis=-1, keepdims=True)
    return xc * lax.rsqrt(var + eps) * g + b


def _conv_seq_kernel(u_ref, w_ref, b_ref, g_ref, lb_ref, o_ref, buf_ref, *, taps):
    tc = u_ref.shape[1]

    @pl.when(pl.program_id(1) == 0)
    def _():
        buf_ref[0:CONV_HALO, :] = jnp.zeros((CONV_HALO, buf_ref.shape[1]), F32)

    buf_ref[CONV_HALO:CONV_HALO + tc, :] = u_ref[0]
    first = CONV_HALO - (taps - 1)
    for s in range(tc // CONV_SUB):
        acc = None
        for k in range(taps):
            start = first + s * CONV_SUB + k
            term = w_ref[k:k + 1, :] * buf_ref[start:start + CONV_SUB, :]
            acc = term if acc is None else acc + term
        c = acc + b_ref[...]
        o_ref[0, s * CONV_SUB:(s + 1) * CONV_SUB, :] = _silu(_layernorm(c, g_ref[...], lb_ref[...])).astype(o_ref.dtype)
    buf_ref[0:CONV_HALO, :] = buf_ref[tc:tc + CONV_HALO, :]


def _conv_seq(u, lw, *, tc=256):
    b, t, c = u.shape
    taps = lw["conv_w"].shape[0]
    assert taps - 1 <= CONV_HALO and t % tc == 0 and tc % CONV_SUB == 0
    return pl.pallas_call(
        functools.partial(_conv_seq_kernel, taps=taps),
        grid=(b, t // tc),
        in_specs=[pl.BlockSpec((1, tc, c), lambda i, j: (i, j, 0)),
                  _const_spec((taps, c)), _const_spec((1, c)), _const_spec((1, c)), _const_spec((1, c))],
        out_specs=pl.BlockSpec((1, tc, c), lambda i, j: (i, j, 0)),
        out_shape=jax.ShapeDtypeStruct((b, t, c), BF16),
        scratch_shapes=[pltpu.VMEM((CONV_HALO + tc, c), F32)],
        compiler_params=_params("arbitrary", "arbitrary"),
        name="conv_seq",
    )(u, lw["conv_w"], lw["conv_b"], lw["ln_g"], lw["ln_b"])


def _rel_bias_tile(relb_ref, head, dist):
    max_exact = NUM_BUCKETS // 2
    n = jnp.maximum(dist, 0)
    nf = jnp.maximum(n, 1).astype(F32)
    large = max_exact + (jnp.log(nf / max_exact) / math.log(REL_MAX_DIST / max_exact)
                         * (NUM_BUCKETS - max_exact)).astype(jnp.int32)
    large = jnp.minimum(large, NUM_BUCKETS - 1)
    bucket = jnp.where(n < max_exact, n, large)
    out = jnp.zeros(dist.shape, F32)
    for b in range(NUM_BUCKETS):
        out = jnp.where(bucket == b, relb_ref[b, head], out)
    return out


def _moba_select(gate, first, n_eligible, lane, sel_mask):
    eligible = (lane >= first) & (lane < first + n_eligible)
    gate = jnp.where(eligible, gate, NEG_INF)
    for _ in range(MOBA_TOPK):
        mx = jnp.max(gate, axis=1, keepdims=True)
        idx = jnp.min(jnp.where(gate == mx, lane, LANES), axis=1, keepdims=True)
        pick = lane == idx
        sel_mask = jnp.where(pick, jnp.where(eligible, 0.0, sel_mask), sel_mask)
        gate = jnp.where(pick, NEG_INF, gate)
    return sel_mask


def _attn_seq_kernel(relb_ref, q_ref, km_ref, k_ref, vt_ref, o_ref,
                     bias_own_ref, bias_prev_ref, kmbd_ref, selt_ref, acc_ref, *, n_heads, n_blocks):
    bs = MOBA_BLOCK
    width = n_heads * DH
    cur = pl.program_id(1)

    @pl.when((pl.program_id(0) == 0) & (cur == 0))
    def _():
        ki = lax.broadcasted_iota(jnp.int32, (bs, bs), 0)
        qi = lax.broadcasted_iota(jnp.int32, (bs, bs), 1)
        for h in range(n_heads):
            own = _rel_bias_tile(relb_ref, h, qi - ki)
            bias_own_ref[h] = jnp.where(qi >= ki, own, NEG_INF)
            bias_prev_ref[h] = _rel_bias_tile(relb_ref, h, qi - ki + bs)

    lane = lax.broadcasted_iota(jnp.int32, (bs, LANES), 1)

    @pl.when(cur == 0)
    def _():
        kmbd_ref[...] = jnp.zeros(kmbd_ref.shape, F32)
        km = km_ref[0]
        km_lane = lax.broadcasted_iota(jnp.int32, km.shape, 1)
        for h in range(n_heads):
            kmbd_ref[h * n_blocks:(h + 1) * n_blocks, :] = jnp.where(
                (km_lane >= h * DH) & (km_lane < (h + 1) * DH), km, 0.0)

    q = q_ref[0]
    q_hi, q_mid, q_lo = _split3(q)
    km_hi, km_mid, km_lo = _split3(kmbd_ref[...])
    gate = (_dot_t(q_hi, km_hi) + _dot_t(q_hi, km_mid) + _dot_t(q_mid, km_hi)
            + _dot_t(q_hi, km_lo) + _dot_t(q_lo, km_hi) + _dot_t(q_mid, km_mid))
    sel_mask = jnp.full((bs, LANES), NEG_INF, F32)
    for h in range(n_heads):
        sel_mask = _moba_select(gate, h * n_blocks, cur, lane, sel_mask)
    selt_ref[...] = sel_mask.T
    qt = (q * (1.0 / math.sqrt(DH))).T
    row_w = lax.broadcasted_iota(jnp.int32, (width, bs), 0)
    qht = [jnp.where((row_w >= h * DH) & (row_w < (h + 1) * DH), qt, 0.0).astype(BF16) for h in range(n_heads)]
    far_bias = [relb_ref[NUM_BUCKETS - 1, h] for h in range(n_heads)]

    def attend(kv_start, ml, bias):
        start = pl.multiple_of(kv_start, bs)
        kb = k_ref[0, pl.ds(start, bs), :]
        scores = [_dot(kb, qht[h]) for h in range(n_heads)]
        out = []
        for h in range(n_heads):
            m, l = ml[2 * h], ml[2 * h + 1]
            s = scores[h]
            bh = bias(h)
            if bh.shape[0] == 1:
                m_new = jnp.maximum(m, jnp.max(s, axis=0, keepdims=True) + bh)
                p = jnp.exp(s + (bh - m_new))
            else:
                s = s + bh
                m_new = jnp.maximum(m, jnp.max(s, axis=0, keepdims=True))
                p = jnp.exp(s - m_new)
            alpha = jnp.exp(m - m_new)
            vth = vt_ref[h * DH:(h + 1) * DH, pl.ds(start, bs)]
            acc_ref[h] = alpha * acc_ref[h] + _dot(vth, p.astype(BF16))
            out += [m_new, alpha * l + jnp.sum(p, axis=0, keepdims=True)]
        return tuple(out)

    def sel_row(h, j):
        return selt_ref[pl.ds(h * n_blocks + j, 1), :]

    acc_ref[...] = jnp.zeros(acc_ref.shape, F32)
    ml = (jnp.full((1, bs), NEG_INF, F32), jnp.zeros((1, bs), F32)) * n_heads
    ml = attend(cur * bs, ml, lambda h: bias_own_ref[h])
    ml = lax.cond(cur >= 1,
                  lambda ml: attend((cur - 1) * bs, ml, lambda h: bias_prev_ref[h] + sel_row(h, cur - 1)),
                  lambda ml: ml, ml)
    ml = lax.fori_loop(0, jnp.maximum(cur - 1, 0),
                       lambda j, ml: attend(j * bs, ml, lambda h: sel_row(h, j) + far_bias[h]), ml)
    out_t = jnp.concatenate([acc_ref[h] * (1.0 / ml[2 * h + 1]) for h in range(n_heads)], axis=0)
    o_ref[0] = out_t.T.astype(o_ref.dtype)


def _attn_seq(q, kb, vt, kmean, rel_bias, *, n_heads):
    b, t, w = q.shape
    nb = t // MOBA_BLOCK
    assert t % MOBA_BLOCK == 0 and nb * n_heads <= LANES and nb % SUBLANES == 0
    return pl.pallas_call(
        functools.partial(_attn_seq_kernel, n_heads=n_heads, n_blocks=nb),
        grid=(b, nb),
        in_specs=[pl.BlockSpec(memory_space=pltpu.SMEM),
                  pl.BlockSpec((1, MOBA_BLOCK, w), lambda i, j: (i, j, 0)),
                  pl.BlockSpec((1, nb, w), lambda i, j: (i, 0, 0)),
                  pl.BlockSpec((1, t, w), lambda i, j: (i, 0, 0)),
                  pl.BlockSpec((w, t), lambda i, j: (0, i))],
        out_specs=pl.BlockSpec((1, MOBA_BLOCK, w), lambda i, j: (i, j, 0)),
        out_shape=jax.ShapeDtypeStruct((b, t, w), BF16),
        scratch_shapes=[pltpu.VMEM((n_heads, MOBA_BLOCK, MOBA_BLOCK), F32),
                        pltpu.VMEM((n_heads, MOBA_BLOCK, MOBA_BLOCK), F32),
                        pltpu.VMEM((LANES, w), F32),
                        pltpu.VMEM((LANES, MOBA_BLOCK), F32),
                        pltpu.VMEM((n_heads, DH, MOBA_BLOCK), F32)],
        compiler_params=_params("arbitrary", "arbitrary"),
        name="attn_seq",
    )(rel_bias, q, kmean, kb, vt)


SSM_HALO = 8


def _softplus(x):
    return jnp.maximum(x, 0.0) + jnp.log1p(jnp.exp(-jnp.abs(x)))


def _ssd_seq_kernel(xbc_ref, z_ref, dt_ref, cw_ref, cb_ref, dtb_ref, alog_ref, dvec_ref, ng_ref,
                    y_ref, hfin_ref, buf_ref, state_ref, *, n_heads):
    lc = SSD_CHUNK
    n = SSM_N
    d_inner = n_heads * SSM_P
    pair_w = 2 * SSM_P
    heads_per_group = n_heads // SSM_G
    chunk = pl.program_id(1)

    @pl.when(chunk == 0)
    def _():
        buf_ref[0:SSM_HALO, :] = jnp.zeros((SSM_HALO, buf_ref.shape[1]), F32)
        state_ref[...] = jnp.zeros(state_ref.shape, F32)

    buf_ref[SSM_HALO:SSM_HALO + lc, :] = xbc_ref[0]
    taps = cw_ref.shape[0]
    first = SSM_HALO - (taps - 1)
    acc = None
    for k in range(taps):
        term = cw_ref[k:k + 1, :] * buf_ref[first + k:first + k + lc, :]
        acc = term if acc is None else acc + term
    xc = _silu(acc + cb_ref[...])
    buf_ref[0:SSM_HALO, :] = buf_ref[lc:lc + SSM_HALO, :]
    xs = xc[:, :d_inner]
    bm = xc[:, d_inner:d_inner + SSM_G * n]
    cm = xc[:, d_inner + SSM_G * n:]

    dt = _softplus(dt_ref[0] + dtb_ref[...])
    dta = dt * (-jnp.exp(alog_ref[...]))
    ri = lax.broadcasted_iota(jnp.int32, (lc, lc), 0)
    ci = lax.broadcasted_iota(jnp.int32, (lc, lc), 1)
    causal = ri >= ci
    acum = _dot_exact_lhs(jnp.where(causal, 1.0, 0.0).astype(BF16), dta)
    acum_t = acum.T
    er = lax.broadcasted_iota(jnp.int32, (LANES, d_inner), 0)
    ec = lax.broadcasted_iota(jnp.int32, (LANES, d_inner), 1) // SSM_P
    dt_x = _dot_exact_rhs(dt, jnp.where(er == ec, 1.0, 0.0).astype(BF16))
    xdt = xs * dt_x
    xdt_b = xdt.astype(BF16)

    cb = []
    for g in range(SSM_G):
        cg = cm[:, g * n:(g + 1) * n].astype(BF16)
        bg = bm[:, g * n:(g + 1) * n].astype(BF16)
        cb.append(_dot_t(cg, bg))

    lane_p = lax.broadcasted_iota(jnp.int32, (lc, pair_w), 1)
    row_p = lax.broadcasted_iota(jnp.int32, (pair_w, n), 0)
    y_tiles = []
    for pair in range(n_heads // 2):
        sl = slice(pair * pair_w, (pair + 1) * pair_w)
        x_pair = xdt_b[:, sl]
        x_pair_t = xdt[:, sl].T.astype(BF16)
        st_old = state_ref[sl, :]
        st_old_b = st_old.astype(BF16)
        y_intra = None
        y_inter = []
        st_new = []
        for half in range(2):
            h = 2 * pair + half
            g = h // heads_per_group
            col = acum[:, h:h + 1]
            row = acum_t[h:h + 1, :]
            last = acum[lc - 1:lc, h:h + 1]
            decay = jnp.exp(jnp.where(causal, col - row, NEG_INF))
            scores = (cb[g] * decay).astype(BF16)
            in_half = (lane_p >= half * SSM_P) & (lane_p < (half + 1) * SSM_P)
            part = _dot(scores, jnp.where(in_half, x_pair, jnp.zeros_like(x_pair)))
            y_intra = part if y_intra is None else y_intra + part
            cw = (cm[:, g * n:(g + 1) * n] * jnp.exp(col)).astype(BF16)
            y_inter.append(_dot_t(cw, st_old_b))
            bw = (bm[:, g * n:(g + 1) * n] * jnp.exp(last - col)).astype(BF16)
            st_new.append(st_old * jnp.exp(last) + _dot(x_pair_t, bw))
        y_tiles.append(y_intra + jnp.where(lane_p < SSM_P, y_inter[0], y_inter[1]))
        state_ref[sl, :] = jnp.where(row_p < SSM_P, st_new[0], st_new[1])
    y = jnp.concatenate(y_tiles, axis=1) + dvec_ref[...] * xs
    gated = y * _silu(z_ref[0])
    y_ref[0] = _rmsnorm(gated, ng_ref[...]).astype(y_ref.dtype)

    @pl.when(chunk == pl.num_programs(1) - 1)
    def _():
        hfin_ref[0] = state_ref[...]


def _ssd_seq(xbc, z, dt, lw, *, n_heads):
    b, t, cd = xbc.shape
    d_inner = z.shape[2]
    lc = SSD_CHUNK
    assert t % lc == 0 and n_heads % 2 == 0 and n_heads % SSM_G == 0 and n_heads <= LANES
    taps = lw["ssm_cw"].shape[0]
    assert taps - 1 <= SSM_HALO
    tile = lambda w: pl.BlockSpec((1, lc, w), lambda i, j: (i, j, 0))
    return pl.pallas_call(
        functools.partial(_ssd_seq_kernel, n_heads=n_heads),
        grid=(b, t // lc),
        in_specs=[tile(cd), tile(d_inner), tile(LANES),
                  _const_spec((taps, cd)), _const_spec((1, cd)), _const_spec((1, LANES)),
                  _const_spec((1, LANES)), _const_spec((1, d_inner)), _const_spec((1, d_inner))],
        out_specs=[tile(d_inner), pl.BlockSpec((1, d_inner, SSM_N), lambda i, j: (i, 0, 0))],
        out_shape=[jax.ShapeDtypeStruct((b, t, d_inner), BF16),
                   jax.ShapeDtypeStruct((b, d_inner, SSM_N), F32)],
        scratch_shapes=[pltpu.VMEM((SSM_HALO + lc, cd), F32), pltpu.VMEM((d_inner, SSM_N), F32)],
        compiler_params=_params("arbitrary", "arbitrary"),
        name="ssd_seq",
    )(xbc, z, dt, lw["ssm_cw"], lw["ssm_cb"], lw["dt_bias"], lw["a_log"], lw["d_vec"], lw["ssm_ng"])


def _layer_weights(l, w):
    d_model = w["w_in"].shape[1]
    c_conv = w["conv_dw_w"].shape[2]
    a_w = (w["w_out"].shape[1] - c_conv - w["ssm_norm_g"].shape[1])
    d_inner = w["ssm_norm_g"].shape[1]
    cd = w["ssm_conv_w"].shape[2]
    n_ssm_heads = w["ssm_dt_bias"].shape[1]
    off_q = 2 * c_conv
    off_k = off_q + a_w
    off_v = off_k + a_w
    off_z = off_v + a_w
    off_xbc = off_z + d_inner
    off_dt = off_xbc + cd
    n_in = off_dt + n_ssm_heads
    assert w["w_in"].shape[2] == n_in
    n_pad = off_dt + LANES
    row = lambda a: a[l].reshape(1, -1).astype(F32)
    pad_row = lambda a: jnp.pad(a[l].astype(F32), (0, LANES - a.shape[1])).reshape(1, LANES)
    return dict(
        offs=(c_conv, off_q, off_k, off_v, off_z, off_xbc, off_dt, n_pad),
        n1=row(w["ffn1_norm"]), wg1=w["ffn1_wg"][l].astype(BF16), wu1=w["ffn1_wu"][l].astype(BF16),
        wd1=w["ffn1_wd"][l].astype(BF16), n_mix=row(w["mix_norm"]),
        w_in=jnp.pad(w["w_in"][l], ((0, 0), (0, n_pad - n_in))).astype(BF16),
        w_out=w["w_out"][l].astype(BF16),
        qg=jnp.tile(w["q_norm_g"][l], a_w // DH).reshape(1, a_w),
        kg=jnp.tile(w["k_norm_g"][l], a_w // DH).reshape(1, a_w),
        conv_w=w["conv_dw_w"][l], conv_b=row(w["conv_dw_b"]), ln_g=row(w["conv_ln_g"]), ln_b=row(w["conv_ln_b"]),
        ssm_cw=w["ssm_conv_w"][l], ssm_cb=row(w["ssm_conv_b"]),
        dt_bias=pad_row(w["ssm_dt_bias"]), a_log=pad_row(w["ssm_a_log"]),
        d_vec=jnp.repeat(w["ssm_d"][l], SSM_P).reshape(1, d_inner), ssm_ng=row(w["ssm_norm_g"]),
        n2=row(w["ffn2_norm"]), wg2=w["ffn2_wg"][l].astype(BF16), wu2=w["ffn2_wu"][l].astype(BF16),
        wd2=w["ffn2_wd"][l].astype(BF16), n_fin=row(w["final_norm"]),
        n_attn_heads=a_w // DH, n_ssm_heads=n_ssm_heads,
    )


def _row_tile(rows, want=512):
    tm = min(want, rows)
    assert rows % tm == 0 and tm % SUBLANES == 0
    return tm


def _layer_seq(x, lw, rel_bias):
    b, t, d = x.shape
    n_ah, n_sh = lw["n_attn_heads"], lw["n_ssm_heads"]
    tm = _row_tile(b * t)
    assert t % tm == 0 and tm % MOBA_BLOCK == 0
    x1, u, q, k, v, z, xbc, dt, kb, vt, km = _ffn_proj(x.reshape(b * t, d), lw, tm=tm, for_seq_attn=True)
    seq = lambda a: a.reshape(b, t, a.shape[-1])
    u, xbc = seq(u), seq(xbc)
    conv_out = _conv_seq(u, lw)
    attn_out = _attn_seq(seq(q), seq(kb), vt, km.reshape(b, t // MOBA_BLOCK, -1), rel_bias, n_heads=n_ah)
    ssm_out, h_fin = _ssd_seq(xbc, seq(z), seq(dt), lw, n_heads=n_sh)
    flat = lambda a: a.reshape(b * t, a.shape[-1])
    y = _out_ffn(x1, flat(conv_out), flat(attn_out), flat(ssm_out), lw, tm=tm)
    conv_taps = lw["conv_w"].shape[0]
    ssm_taps = lw["ssm_cw"].shape[0]
    states = (k.reshape(b, t, n_ah, DH), v.reshape(b, t, n_ah, DH), u[:, t - (conv_taps - 1):],
              xbc[:, t - (ssm_taps - 1):], h_fin.reshape(b, n_sh, SSM_P, SSM_N))
    return y.reshape(b, t, d), states


def _conv_step_kernel(u_ref, st_ref, w_ref, b_ref, g_ref, lb_ref, o_ref):
    past = st_ref.shape[0]
    acc = w_ref[past:past + 1, :] * u_ref[...]
    for k in range(past):
        acc = acc + w_ref[k:k + 1, :] * st_ref[k]
    o_ref[...] = _silu(_layernorm(acc + b_ref[...], g_ref[...], lb_ref[...])).astype(o_ref.dtype)


def _conv_step(u, state_t, lw):
    return pl.pallas_call(
        _conv_step_kernel,
        out_shape=jax.ShapeDtypeStruct(u.shape, BF16),
        compiler_params=pltpu.CompilerParams(vmem_limit_bytes=VMEM_LIMIT_BYTES),
        name="conv_step",
    )(u, state_t, lw["conv_w"], lw["conv_b"], lw["ln_g"], lw["ln_b"])


SSD_STEP_ROWS = 8


def _ssd_step_kernel(xn_ref, st_ref, z_ref, dt_ref, h0_ref, cw_ref, cb_ref, dtb_ref, alog_ref, dvec_ref, ng_ref,
                     y_ref, h1_ref, *, n_heads):
    rows = xn_ref.shape[0]
    n = SSM_N
    d_inner = n_heads * SSM_P
    group_rows = (n_heads // SSM_G) * SSM_P
    past = st_ref.shape[0]
    acc = cw_ref[past:past + 1, :] * xn_ref[...]
    for k in range(past):
        acc = acc + cw_ref[k:k + 1, :] * st_ref[k]
    xc = _silu(acc + cb_ref[...])
    xs = xc[:, :d_inner]
    bm = xc[:, d_inner:d_inner + SSM_G * n]
    cm = xc[:, d_inner + SSM_G * n:]
    dt = _softplus(dt_ref[...] + dtb_ref[...])
    dta = dt * (-jnp.exp(alog_ref[...]))
    er = lax.broadcasted_iota(jnp.int32, (LANES, d_inner), 0)
    ec = lax.broadcasted_iota(jnp.int32, (LANES, d_inner), 1) // SSM_P
    expand = jnp.where(er == ec, 1.0, 0.0).astype(BF16)
    xdt = xs * _dot_exact_rhs(dt, expand)
    dec = jnp.exp(_dot_exact_rhs(dta, expand))
    pad = jnp.zeros((LANES - rows, d_inner), F32)
    xdt_t = jnp.concatenate([xdt, pad], axis=0).T
    dec_t = jnp.concatenate([dec, pad], axis=0).T
    lane = lax.broadcasted_iota(jnp.int32, (d_inner, LANES), 1)
    y_t = jnp.zeros((d_inner, LANES), F32)
    for b in range(rows):
        xcol = xdt_t[:, b:b + 1]
        dcol = dec_t[:, b:b + 1]
        ycols = []
        for g in range(SSM_G):
            rs = slice(g * group_rows, (g + 1) * group_rows)
            h0 = h0_ref[b, rs, :]
            brow = bm[b:b + 1, g * n:(g + 1) * n]
            crow = cm[b:b + 1, g * n:(g + 1) * n]
            cb = jnp.sum(crow * brow, axis=1, keepdims=True)
            h1_ref[b, rs, :] = h0 * dcol[rs] + xcol[rs] * brow
            ycols.append(jnp.sum(h0 * crow, axis=1, keepdims=True) * dcol[rs] + cb * xcol[rs])
        y_t = jnp.where(lane == b, jnp.concatenate(ycols, axis=0), y_t)
    y = y_t.T[:rows] + dvec_ref[...] * xs
    gated = y * _silu(z_ref[...])
    y_ref[...] = _rmsnorm(gated, ng_ref[...]).astype(y_ref.dtype)


def _ssd_step(xn, state_t, z, dt, h0, lw, *, n_heads):
    db, cd = xn.shape
    d_inner = z.shape[1]
    rows = SSD_STEP_ROWS
    assert db % rows == 0
    past = state_t.shape[0]
    row = lambda w: pl.BlockSpec((rows, w), lambda i: (i, 0))
    return pl.pallas_call(
        functools.partial(_ssd_step_kernel, n_heads=n_heads),
        grid=(db // rows,),
        in_specs=[row(cd), pl.BlockSpec((past, rows, cd), lambda i: (0, i, 0)), row(d_inner), row(LANES),
                  pl.BlockSpec((rows, d_inner, SSM_N), lambda i: (i, 0, 0)),
                  _const_spec((past + 1, cd)), _const_spec((1, cd)), _const_spec((1, LANES)),
                  _const_spec((1, LANES)), _const_spec((1, d_inner)), _const_spec((1, d_inner))],
        out_specs=[row(d_inner), pl.BlockSpec((rows, d_inner, SSM_N), lambda i: (i, 0, 0))],
        out_shape=[jax.ShapeDtypeStruct((db, d_inner), BF16), jax.ShapeDtypeStruct(h0.shape, F32)],
        compiler_params=_params("arbitrary"),
        name="ssd_step",
    )(xn, state_t, z, dt, h0, lw["ssm_cw"], lw["ssm_cb"], lw["dt_bias"], lw["a_log"], lw["d_vec"], lw["ssm_ng"])


GATE_PAGES = 32


def _gate_pages_kernel(pt_ref, q_ref, *refs, n_blocks):
    del pt_ref
    page_refs, idx_ref, kmean_ref = refs[:-2], refs[-2], refs[-1]
    per_block = MOBA_BLOCK // PAGE_SIZE
    blocks = len(page_refs) // per_block
    chunk = pl.program_id(1)
    n_heads = kmean_ref.shape[0]

    @pl.when(chunk == 0)
    def _():
        kmean_ref[...] = jnp.zeros(kmean_ref.shape, F32)

    lane3 = lax.broadcasted_iota(jnp.int32, kmean_ref.shape, 2)
    km = kmean_ref[...]
    for blk in range(blocks):
        s = None
        for r in range(per_block):
            page = page_refs[blk * per_block + r][...]
            s = page if s is None else s + page
        col = jnp.sum(s, axis=2, keepdims=True) * (1.0 / MOBA_BLOCK)
        km = jnp.where(lane3 == chunk * blocks + blk, col, km)
    kmean_ref[...] = km

    @pl.when(chunk == pl.num_programs(1) - 1)
    def _():
        lane = lax.broadcasted_iota(jnp.int32, (SUBLANES, LANES), 1)
        for h in range(n_heads):
            q_hi, q_mid, q_lo = _split3(jnp.broadcast_to(q_ref[0, h], (SUBLANES, DH)))
            k_hi, k_mid, k_lo = _split3(kmean_ref[h])
            gate = (_dot(q_hi, k_hi) + _dot(q_hi, k_mid) + _dot(q_mid, k_hi)
                    + _dot(q_hi, k_lo) + _dot(q_lo, k_hi) + _dot(q_mid, k_mid))
            gate = jnp.where(lane < n_blocks, gate, NEG_INF)
            picks = jnp.zeros((SUBLANES, LANES), jnp.int32)
            for rank in range(MOBA_TOPK):
                mx = jnp.max(gate, axis=1, keepdims=True)
                idx = jnp.min(jnp.where(gate == mx, lane, LANES), axis=1, keepdims=True)
                picks = jnp.where(lane == rank, idx, picks)
                gate = jnp.where(lane == idx, NEG_INF, gate)
            idx_ref[0, h] = picks


def _gate_pages(q4, cache_t, layer, page_table_flat, n_pages):
    db, n_heads = q4.shape[:2]
    per_block = MOBA_BLOCK // PAGE_SIZE
    n_blocks = n_pages // per_block
    pg = min(GATE_PAGES, n_pages)
    assert n_pages % pg == 0 and pg % per_block == 0 and MOBA_TOPK <= n_blocks <= LANES

    def page_spec(j):
        return pl.BlockSpec((None, None, n_heads, DH, PAGE_SIZE),
                            lambda b, i, pt: (layer, pt[b * n_pages + i * pg + j], 0, 0, 0))

    return pl.pallas_call(
        functools.partial(_gate_pages_kernel, n_blocks=n_blocks),
        grid_spec=pltpu.PrefetchScalarGridSpec(
            num_scalar_prefetch=1, grid=(db, n_pages // pg),
            in_specs=[pl.BlockSpec((1, n_heads, 1, DH), lambda b, i, pt: (b, 0, 0, 0))]
            + [page_spec(j) for j in range(pg)],
            out_specs=pl.BlockSpec((1, n_heads, SUBLANES, LANES), lambda b, i, pt: (b, 0, 0, 0)),
            scratch_shapes=[pltpu.VMEM((n_heads, DH, LANES), F32)]),
        out_shape=jax.ShapeDtypeStruct((db, n_heads, SUBLANES, LANES), jnp.int32),
        compiler_params=_params("arbitrary", "arbitrary"),
        name="gate_pages",
    )(page_table_flat, q4, *([cache_t] * pg))


def _attn_step_kernel(pt_ref, sel_ref, relb_ref, q_ref, kn_ref, vn_ref, *refs, n_heads, past_len):
    del pt_ref
    per_block = MOBA_BLOCK // PAGE_SIZE
    n_pg = MOBA_TOPK * per_block
    k_pages, v_pages, o_ref = refs[:n_pg], refs[n_pg:2 * n_pg], refs[2 * n_pg]
    b, h = pl.program_id(0), pl.program_id(1)
    qh = q_ref[...] * (1.0 / math.sqrt(DH))
    q8 = jnp.broadcast_to(qh, (SUBLANES, DH)).astype(BF16)
    key_off = lax.broadcasted_iota(jnp.int32, (1, PAGE_SIZE), 1)
    s_own = jnp.sum(qh * kn_ref[...], axis=1, keepdims=True) + relb_ref[0, h]
    logits = []
    for j in range(MOBA_TOPK):
        blk = sel_ref[(b * n_heads + h) * MOBA_TOPK + j]
        for r in range(per_block):
            dist = past_len - (blk * MOBA_BLOCK + r * PAGE_SIZE + key_off)
            kt = k_pages[j * per_block + r][...].astype(BF16)
            logits.append(_dot(q8, kt)[0:1, :] + _rel_bias_tile(relb_ref, h, dist))
    m = s_own
    for s in logits:
        m = jnp.maximum(m, jnp.max(s, axis=1, keepdims=True))
    p_own = jnp.exp(s_own - m)
    l = p_own
    acc = p_own * vn_ref[...]
    for page, s in enumerate(logits):
        p = jnp.exp(s - m)
        l = l + jnp.sum(p, axis=1, keepdims=True)
        vt = v_pages[page][...].astype(BF16)
        acc = acc + _dot_t(jnp.broadcast_to(p, (SUBLANES, PAGE_SIZE)).astype(BF16), vt)[0:1, :]
    o_ref[...] = jnp.broadcast_to(acc / l, (SUBLANES, DH))


def _attn_step(q4, kn4, vn4, cache_kt, cache_vt, layer, page_table_flat, sel_flat, rel_bias, *, n_pages):
    db, n_heads = q4.shape[:2]
    per_block = MOBA_BLOCK // PAGE_SIZE
    past_len = n_pages * PAGE_SIZE
    assert past_len % MOBA_BLOCK == 0 and past_len // MOBA_BLOCK >= MOBA_TOPK

    def page_spec(j, r):
        def index_map(b, h, pt, sel):
            blk = sel[(b * n_heads + h) * MOBA_TOPK + j]
            return (layer, pt[b * n_pages + blk * per_block + r], h, 0, 0)
        return pl.BlockSpec((None, None, None, DH, PAGE_SIZE), index_map)

    page_specs = [page_spec(j, r) for j in range(MOBA_TOPK) for r in range(per_block)]
    tok = pl.BlockSpec((None, None, 1, DH), lambda b, h, pt, sel: (b, h, 0, 0))
    n_pg = len(page_specs)
    return pl.pallas_call(
        functools.partial(_attn_step_kernel, n_heads=n_heads, past_len=past_len),
        grid_spec=pltpu.PrefetchScalarGridSpec(
            num_scalar_prefetch=2, grid=(db, n_heads),
            in_specs=[pl.BlockSpec(memory_space=pltpu.SMEM), tok, tok, tok] + page_specs + page_specs,
            out_specs=pl.BlockSpec((None, None, SUBLANES, DH), lambda b, h, pt, sel: (b, h, 0, 0))),
        out_shape=jax.ShapeDtypeStruct((db, n_heads, SUBLANES, DH), F32),
        compiler_params=_params("arbitrary", "arbitrary"),
        name="attn_step",
    )(page_table_flat, sel_flat, rel_bias, q4, kn4, vn4, *([cache_kt] * n_pg), *([cache_vt] * n_pg))


def _layer_step(x, conv_state, ssm_conv_state, ssm_state, cache_kt, cache_vt, layer, page_table, lw, rel_bias):
    db, t, d = x.shape
    assert t == 1
    n_ah, n_sh = lw["n_attn_heads"], lw["n_ssm_heads"]
    tm = _row_tile(db)
    x1, u, q, k, v, z, xbc, dt = _ffn_proj(x.reshape(db, d), lw, tm=tm, for_seq_attn=False)
    conv_out = _conv_step(u, conv_state.transpose(1, 0, 2), lw)
    n_pages = page_table.shape[1]
    pt_flat = page_table.reshape(-1)
    q4 = q.reshape(db, n_ah, 1, DH)
    picks = _gate_pages(q4, cache_kt, layer, pt_flat, n_pages)
    sel_flat = picks[:, :, 0, :MOBA_TOPK].reshape(-1)
    attn = _attn_step(q4, k.reshape(db, n_ah, 1, DH), v.reshape(db, n_ah, 1, DH), cache_kt, cache_vt, layer,
                      pt_flat, sel_flat, rel_bias, n_pages=n_pages)
    attn_out = attn[:, :, 0, :].reshape(db, n_ah * DH)
    ssm_out, h_new = _ssd_step(xbc, ssm_conv_state.transpose(1, 0, 2), z, dt,
                               ssm_state.reshape(db, n_sh * SSM_P, SSM_N), lw, n_heads=n_sh)
    y = _out_ffn(x1, conv_out, attn_out.astype(BF16), ssm_out, lw, tm=tm)
    states = (k.reshape(db, 1, n_ah, DH), v.reshape(db, 1, n_ah, DH),
              jnp.concatenate([conv_state[:, 1:], u[:, None, :]], axis=1),
              jnp.concatenate([ssm_conv_state[:, 1:], xbc[:, None, :]], axis=1),
              h_new.reshape(ssm_state.shape))
    return y.reshape(db, 1, d), states


def kernel(x_prompt, x_sample, cache_k, cache_v, state_conv, state_ssm_conv, state_ssm, page_table, rel_bias, ffn1_norm, ffn1_wg, ffn1_wu, ffn1_wd, mix_norm, w_in, w_out, conv_dw_w, conv_dw_b, conv_ln_g, conv_ln_b, q_norm_g, k_norm_g, ssm_conv_w, ssm_conv_b, ssm_dt_bias, ssm_a_log, ssm_d, ssm_norm_g, ffn2_norm, ffn2_wg, ffn2_wu, ffn2_wd, final_norm):
    w = dict(ffn1_norm=ffn1_norm, ffn1_wg=ffn1_wg, ffn1_wu=ffn1_wu, ffn1_wd=ffn1_wd, mix_norm=mix_norm,
             w_in=w_in, w_out=w_out, conv_dw_w=conv_dw_w, conv_dw_b=conv_dw_b, conv_ln_g=conv_ln_g,
             conv_ln_b=conv_ln_b, q_norm_g=q_norm_g, k_norm_g=k_norm_g, ssm_conv_w=ssm_conv_w,
             ssm_conv_b=ssm_conv_b, ssm_dt_bias=ssm_dt_bias, ssm_a_log=ssm_a_log, ssm_d=ssm_d,
             ssm_norm_g=ssm_norm_g, ffn2_norm=ffn2_norm, ffn2_wg=ffn2_wg, ffn2_wu=ffn2_wu, ffn2_wd=ffn2_wd,
             final_norm=final_norm)
    depth = w_in.shape[0]
    y_p, y_s = x_prompt, x_sample
    st_p, st_s = [], []
    cache_kt = cache_k.transpose(0, 1, 3, 4, 2)
    cache_vt = cache_v.transpose(0, 1, 3, 4, 2)
    for l in range(depth):
        lw = _layer_weights(l, w)
        y_p, st = _layer_seq(y_p, lw, rel_bias)
        st_p.append(st)
        y_s, st = _layer_step(y_s, state_conv[l], state_ssm_conv[l], state_ssm[l], cache_kt, cache_vt, l,
                              page_table, lw, rel_bias)
        st_s.append(st)
    stack = lambda sts, i: jnp.stack([s[i] for s in sts])
    return (y_p, y_s) + tuple(stack(st_p, i) for i in range(5)) + tuple(stack(st_s, i) for i in range(5))
```

```python
import functools
import math

import jax
import jax.numpy as jnp
from jax import lax
from jax.experimental import pallas as pl
from jax.experimental.pallas import tpu as pltpu

F32 = jnp.float32
BF16 = jnp.bfloat16

DH = 64
MOBA_BLOCK = 256
MOBA_TOPK = 3
NUM_BUCKETS = 32
REL_MAX_DIST = 128
SSM_P = 64
SSM_G = 2
SSM_N = 128
SSD_CHUNK = 256
PAGE_SIZE = 128

LANES = 128
SUBLANES = 8
VMEM_LIMIT_BYTES = 56 * 1024 * 1024
BF16_SUBLANES = 16

VT_ONES = BF16_SUBLANES
VT_HEAD_ROWS = DH + VT_ONES
LOG2E = math.log2(math.e)

NEG_INF = float("-inf")


def _params(*semantics):
    return pltpu.CompilerParams(dimension_semantics=semantics, vmem_limit_bytes=VMEM_LIMIT_BYTES)


def _dot(a, b):
    return jnp.dot(a, b, preferred_element_type=F32)


def _dot_t(a, b):
    return lax.dot_general(a, b, (((1,), (1,)), ((), ())), preferred_element_type=F32)


def _split3(x):
    hi = x.astype(BF16)
    r1 = x - hi.astype(F32)
    mid = r1.astype(BF16)
    lo = (r1 - mid.astype(F32)).astype(BF16)
    return hi, mid, lo


def _dot_exact_rhs(x, sel):
    hi, mid, lo = _split3(x)
    return _dot(hi, sel) + _dot(mid, sel) + _dot(lo, sel)


def _dot_exact_lhs(sel, x):
    hi, mid, lo = _split3(x)
    return _dot(sel, hi) + _dot(sel, mid) + _dot(sel, lo)


def _sigmoid(x):
    return 1.0 / (1.0 + jnp.exp(-x))


def _silu(x):
    return x * _sigmoid(x)


def _rmsnorm(x, g, eps=1e-6):
    return x * lax.rsqrt(jnp.mean(x * x, axis=-1, keepdims=True) + eps) * g


def _const_spec(shape):
    nd = len(shape)
    return pl.BlockSpec(shape, lambda *_: (0,) * nd, pipeline_mode=pl.Buffered(1))


def _swiglu(hb, wg_ref, wu_ref, wd_ref, fchunk):
    d_ff = wg_ref.shape[1]
    acc = None
    for c in range(d_ff // fchunk):
        sl = slice(c * fchunk, (c + 1) * fchunk)
        g = _dot(hb, wg_ref[:, sl])
        u = _dot(hb, wu_ref[:, sl])
        a = (_silu(g) * u).astype(BF16)
        part = _dot(a, wd_ref[sl, :])
        acc = part if acc is None else acc + part
    return acc


def _head_sumsq_matrix(width):
    r = lax.broadcasted_iota(jnp.int32, (width, width), 0) // DH
    c = lax.broadcasted_iota(jnp.int32, (width, width), 1) // DH
    return jnp.where(r == c, 1.0, 0.0).astype(BF16)


def _head_rmsnorm(p, g_row, eps=1e-6):
    ss = _dot_exact_rhs(p * p, _head_sumsq_matrix(p.shape[1]))
    return p * lax.rsqrt(ss * (1.0 / DH) + eps) * g_row


def _ffn_proj_kernel(x_ref, n1_ref, wg_ref, wu_ref, wd_ref, n2_ref, win_ref, qg_ref, kg_ref,
                     x1_ref, u_ref, q_ref, k_ref, v_ref, z_ref, xbc_ref, dt_ref,
                     *maybe_attn_refs, fchunk, offs):
    c_conv, off_q, off_k, off_v, off_z, off_xbc, off_dt, n_pad = offs
    x = x_ref[...]
    hb = _rmsnorm(x, n1_ref[...]).astype(BF16)
    x1 = x + 0.5 * _swiglu(hb, wg_ref, wu_ref, wd_ref, fchunk)
    x1_ref[...] = x1
    h2 = _rmsnorm(x1, n2_ref[...]).astype(BF16)
    pg = _dot(h2, win_ref[:, 0:off_q])
    u_ref[...] = pg[:, :c_conv] * _sigmoid(pg[:, c_conv:])
    q_ref[...] = _head_rmsnorm(_dot(h2, win_ref[:, off_q:off_k]), qg_ref[...])
    k = _head_rmsnorm(_dot(h2, win_ref[:, off_k:off_v]), kg_ref[...])
    k_ref[...] = k
    v = _dot(h2, win_ref[:, off_v:off_z])
    v_ref[...] = v
    z_ref[...] = _dot(h2, win_ref[:, off_z:off_xbc])
    xbc_ref[...] = _dot(h2, win_ref[:, off_xbc:off_dt])
    dt_ref[...] = _dot(h2, win_ref[:, off_dt:n_pad])
    if maybe_attn_refs:
        kb_ref, vt_ref, km_ref = maybe_attn_refs
        kb_ref[...] = k.astype(BF16)
        vt = v.T.astype(BF16)
        ones = jnp.ones((VT_ONES, vt.shape[1]), BF16)
        vt_ref[...] = jnp.concatenate(
            [part for h in range(vt.shape[0] // DH) for part in (vt[h * DH:(h + 1) * DH], ones)], axis=0)
        nb = k.shape[0] // MOBA_BLOCK
        km_ref[0] = jnp.mean(k.reshape(nb, MOBA_BLOCK, k.shape[1]), axis=1)


def _ffn_proj(x, lw, *, tm, for_seq_attn):
    r, d = x.shape
    d_ff = lw["wg1"].shape[1]
    offs = lw["offs"]
    c_conv, off_q, off_k, off_v, off_z, off_xbc, off_dt, n_pad = offs
    a_w = off_k - off_q
    widths = dict(u=c_conv, q=a_w, k=a_w, v=a_w, z=off_xbc - off_z, xbc=off_dt - off_xbc, dt=n_pad - off_dt)
    row = lambda w: pl.BlockSpec((tm, w), lambda i: (i, 0))
    out_shape = [jax.ShapeDtypeStruct((r, d), F32),
                 jax.ShapeDtypeStruct((r, widths["u"]), F32),
                 jax.ShapeDtypeStruct((r, a_w), F32),
                 jax.ShapeDtypeStruct((r, a_w), F32),
                 jax.ShapeDtypeStruct((r, a_w), F32),
                 jax.ShapeDtypeStruct((r, widths["z"]), F32),
                 jax.ShapeDtypeStruct((r, widths["xbc"]), F32),
                 jax.ShapeDtypeStruct((r, widths["dt"]), F32)]
    out_specs = [row(d), row(widths["u"]), row(a_w), row(a_w), row(a_w),
                 row(widths["z"]), row(widths["xbc"]), row(widths["dt"])]
    if for_seq_attn:
        nb = tm // MOBA_BLOCK
        vt_rows = (a_w // DH) * VT_HEAD_ROWS
        out_shape += [jax.ShapeDtypeStruct((r, a_w), BF16), jax.ShapeDtypeStruct((vt_rows, r), BF16),
                      jax.ShapeDtypeStruct((r // tm, nb, a_w), F32)]
        out_specs += [row(a_w), pl.BlockSpec((vt_rows, tm), lambda i: (0, i)),
                      pl.BlockSpec((1, nb, a_w), lambda i: (i, 0, 0))]
    fchunk = 256 if d_ff % 256 == 0 else d_ff
    outs = pl.pallas_call(
        functools.partial(_ffn_proj_kernel, fchunk=fchunk, offs=offs),
        grid=(r // tm,),
        in_specs=[row(d), _const_spec((1, d)), _const_spec((d, d_ff)), _const_spec((d, d_ff)),
                  _const_spec((d_ff, d)), _const_spec((1, d)), _const_spec((d, n_pad)),
                  _const_spec((1, a_w)), _const_spec((1, a_w))],
        out_specs=out_specs,
        out_shape=out_shape,
        compiler_params=_params("arbitrary"),
        name="ffn_proj",
    )(x, lw["n1"], lw["wg1"], lw["wu1"], lw["wd1"], lw["n_mix"], lw["w_in"], lw["qg"], lw["kg"])
    return outs


def _out_ffn_kernel(x1_ref, a_ref, b_ref, c_ref, wo_ref, n_ref, wg_ref, wu_ref, wd_ref, fn_ref, y_ref,
                    *, fchunk):
    wa, wb = a_ref.shape[1], b_ref.shape[1]
    mix = (_dot(a_ref[...], wo_ref[0:wa, :]) + _dot(b_ref[...], wo_ref[wa:wa + wb, :])
           + _dot(c_ref[...], wo_ref[wa + wb:, :]))
    x2 = x1_ref[...] + mix
    hb = _rmsnorm(x2, n_ref[...]).astype(BF16)
    x3 = x2 + 0.5 * _swiglu(hb, wg_ref, wu_ref, wd_ref, fchunk)
    y_ref[...] = _rmsnorm(x3, fn_ref[...])


def _out_ffn(x1, conv_out, attn_out, ssm_out, lw, *, tm):
    r, d = x1.shape
    d_ff = lw["wg2"].shape[1]
    row = lambda w: pl.BlockSpec((tm, w), lambda i: (i, 0))
    fchunk = 256 if d_ff % 256 == 0 else d_ff
    return pl.pallas_call(
        functools.partial(_out_ffn_kernel, fchunk=fchunk),
        grid=(r // tm,),
        in_specs=[row(d), row(conv_out.shape[1]), row(attn_out.shape[1]), row(ssm_out.shape[1]),
                  _const_spec(lw["w_out"].shape), _const_spec((1, d)), _const_spec((d, d_ff)),
                  _const_spec((d, d_ff)), _const_spec((d_ff, d)), _const_spec((1, d))],
        out_specs=row(d),
        out_shape=jax.ShapeDtypeStruct((r, d), F32),
        compiler_params=_params("arbitrary"),
        name="out_ffn",
    )(x1, conv_out, attn_out, ssm_out, lw["w_out"], lw["n2"], lw["wg2"], lw["wu2"], lw["wd2"], lw["n_fin"])


CONV_HALO = 32
CONV_SUB = 64


def _layernorm(c, g, b, eps=1e-5):
    xc = c - jnp.mean(c, axis=-1, keepdims=True)
    var = jnp.mean(xc * xc, axis=-1, keepdims=True)
    return xc * lax.rsqrt(var + eps) * g + b


def _conv_seq_kernel(u_ref, w_ref, b_ref, g_ref, lb_ref, o_ref, buf_ref, *, taps):
    tc = u_ref.shape[1]

    @pl.when(pl.program_id(1) == 0)
    def _():
        buf_ref[0:CONV_HALO, :] = jnp.zeros((CONV_HALO, buf_ref.shape[1]), F32)

    buf_ref[CONV_HALO:CONV_HALO + tc, :] = u_ref[0]
    first = CONV_HALO - (taps - 1)
    for s in range(tc // CONV_SUB):
        acc = None
        for k in range(taps):
            start = first + s * CONV_SUB + k
            term = w_ref[k:k + 1, :] * buf_ref[start:start + CONV_SUB, :]
            acc = term if acc is None else acc + term
        c = acc + b_ref[...]
        o_ref[0, s * CONV_SUB:(s + 1) * CONV_SUB, :] = _silu(_layernorm(c, g_ref[...], lb_ref[...])).astype(o_ref.dtype)
    buf_ref[0:CONV_HALO, :] = buf_ref[tc:tc + CONV_HALO, :]


def _conv_seq(u, lw, *, tc=256):
    b, t, c = u.shape
    taps = lw["conv_w"].shape[0]
    assert taps - 1 <= CONV_HALO and t % tc == 0 and tc % CONV_SUB == 0
    return pl.pallas_call(
        functools.partial(_conv_seq_kernel, taps=taps),
        grid=(b, t // tc),
        in_specs=[pl.BlockSpec((1, tc, c), lambda i, j: (i, j, 0)),
                  _const_spec((taps, c)), _const_spec((1, c)), _const_spec((1, c)), _const_spec((1, c))],
        out_specs=pl.BlockSpec((1, tc, c), lambda i, j: (i, j, 0)),
        out_shape=jax.ShapeDtypeStruct((b, t, c), BF16),
        scratch_shapes=[pltpu.VMEM((CONV_HALO + tc, c), F32)],
        compiler_params=_params("arbitrary", "arbitrary"),
        name="conv_seq",
    )(u, lw["conv_w"], lw["conv_b"], lw["ln_g"], lw["ln_b"])


def _rel_bias_tile(relb_ref, head, dist):
    max_exact = NUM_BUCKETS // 2
    n = jnp.maximum(dist, 0)
    nf = jnp.maximum(n, 1).astype(F32)
    large = max_exact + (jnp.log(nf / max_exact) / math.log(REL_MAX_DIST / max_exact)
                         * (NUM_BUCKETS - max_exact)).astype(jnp.int32)
    large = jnp.minimum(large, NUM_BUCKETS - 1)
    bucket = jnp.where(n < max_exact, n, large)
    out = jnp.zeros(dist.shape, F32)
    for b in range(NUM_BUCKETS):
        out = jnp.where(bucket == b, relb_ref[b, head], out)
    return out


def _moba_select_t(gate_t, n_eligible):
    nb = gate_t.shape[0]
    row = lax.broadcasted_iota(jnp.int32, gate_t.shape, 0).astype(F32)
    eligible = row < n_eligible.astype(F32)
    gate_t = jnp.where(eligible, gate_t, NEG_INF)
    sel = jnp.full(gate_t.shape, NEG_INF, F32)
    for _ in range(MOBA_TOPK):
        mx = jnp.max(gate_t, axis=0, keepdims=True)
        idx = jnp.min(jnp.where(gate_t == mx, row, float(nb)), axis=0, keepdims=True)
        pick = row == idx
        sel = jnp.where(pick, jnp.where(eligible, 0.0, sel), sel)
        gate_t = jnp.where(pick, NEG_INF, gate_t)
    return sel


def _attn_seq_kernel(relb_ref, q_ref, km_ref, k_ref, vt_ref, o_ref,
                     bias_own_ref, bias_prev_ref, kmbd_ref, selt_ref, acc_ref, s0_ref, s1_ref, *, n_heads, n_blocks):
    bs = MOBA_BLOCK
    width = n_heads * DH
    cur = pl.program_id(1)

    @pl.when((pl.program_id(0) == 0) & (cur == 0))
    def _():
        ki = lax.broadcasted_iota(jnp.int32, (bs, bs), 0)
        qi = lax.broadcasted_iota(jnp.int32, (bs, bs), 1)
        for h in range(n_heads):
            own = _rel_bias_tile(relb_ref, h, qi - ki) * LOG2E
            bias_own_ref[h] = jnp.where(qi >= ki, own, NEG_INF)
            bias_prev_ref[h] = _rel_bias_tile(relb_ref, h, qi - ki + bs) * LOG2E

    @pl.when(cur == 0)
    def _():
        kmbd_ref[...] = jnp.zeros(kmbd_ref.shape, F32)
        km = km_ref[0]
        km_lane = lax.broadcasted_iota(jnp.int32, km.shape, 1)
        for h in range(n_heads):
            kmbd_ref[h * n_blocks:(h + 1) * n_blocks, :] = jnp.where(
                (km_lane >= h * DH) & (km_lane < (h + 1) * DH), km, 0.0)

    qt = q_ref[0].T
    q_hi, q_mid, q_lo = _split3(qt)
    km_hi, km_mid, km_lo = _split3(kmbd_ref[...])
    gate_t = (_dot(km_hi, q_hi) + _dot(km_mid, q_hi) + _dot(km_hi, q_mid)
              + _dot(km_lo, q_hi) + _dot(km_hi, q_lo) + _dot(km_mid, q_mid))
    for h in range(n_heads):
        rows = slice(h * n_blocks, (h + 1) * n_blocks)
        selt_ref[rows, :] = _moba_select_t(gate_t[rows], cur)
    qs = qt * (LOG2E / math.sqrt(DH))
    row_w = lax.broadcasted_iota(jnp.int32, (width, bs), 0)
    qht = [jnp.where((row_w >= h * DH) & (row_w < (h + 1) * DH), qs, 0.0).astype(BF16) for h in range(n_heads)]
    far_bias = [relb_ref[NUM_BUCKETS - 1, h] * LOG2E for h in range(n_heads)]

    def scores(blk):
        kb = k_ref[0, pl.ds(pl.multiple_of(blk * bs, bs), bs), :]
        return [_dot(kb, qht[h]) for h in range(n_heads)]

    def softmax_pv(blk, s_all, ms, bias):
        start = pl.multiple_of(blk * bs, bs)
        out = []
        for h in range(n_heads):
            m = ms[h]
            s = s_all[h]()
            bh = bias(h)
            if bh.shape[0] == 1:
                m_new = jnp.maximum(m, jnp.max(s, axis=0, keepdims=True) + bh)
                p = jnp.exp2(s + (bh - m_new))
            else:
                s = s + bh
                m_new = jnp.maximum(m, jnp.max(s, axis=0, keepdims=True))
                p = jnp.exp2(s - m_new)
            alpha = jnp.exp2(m - m_new)
            vth = vt_ref[h * VT_HEAD_ROWS:(h + 1) * VT_HEAD_ROWS, pl.ds(start, bs)]
            acc_ref[h] = alpha * acc_ref[h] + _dot(vth, p.astype(BF16))
            out.append(m_new)
        return tuple(out)

    def sel_row(h, j):
        return selt_ref[pl.ds(h * n_blocks + j, 1), :]

    def attend(blk, ms, bias):
        s_all = scores(blk)
        return softmax_pv(blk, [lambda s=s: s for s in s_all], ms, bias)

    acc_ref[...] = jnp.zeros(acc_ref.shape, F32)
    ms = (jnp.full((1, bs), NEG_INF, F32),) * n_heads
    ms = attend(cur, ms, lambda h: bias_own_ref[h])
    ms = lax.cond(cur >= 1,
                  lambda ms: attend(cur - 1, ms, lambda h: bias_prev_ref[h] + sel_row(h, cur - 1)),
                  lambda ms: ms, ms)

    n_far = jnp.maximum(cur - 1, 0)
    last_far = jnp.maximum(n_far - 1, 0)

    def put_scores(s_ref, blk):
        for h, s in enumerate(scores(jnp.minimum(blk, last_far))):
            s_ref[h] = s

    def far_bias_row(h, blk):
        row = sel_row(h, jnp.minimum(blk, last_far)) + far_bias[h]
        return jnp.where(blk < n_far, row, NEG_INF)

    def far_pair(i, ms):
        b0, b1 = 2 * i, 2 * i + 1
        put_scores(s1_ref, b1)
        ms = softmax_pv(b0, [lambda h=h: s0_ref[h] for h in range(n_heads)], ms, lambda h: far_bias_row(h, b0))
        put_scores(s0_ref, b1 + 1)
        b1c = jnp.minimum(b1, last_far)
        return softmax_pv(b1c, [lambda h=h: s1_ref[h] for h in range(n_heads)], ms, lambda h: far_bias_row(h, b1))

    put_scores(s0_ref, 0)
    lax.fori_loop(0, (n_far + 1) // 2, far_pair, ms)
    out_t = jnp.concatenate([acc_ref[h, :DH] * (1.0 / acc_ref[h, DH:DH + 1]) for h in range(n_heads)], axis=0)
    o_ref[0] = out_t.T.astype(o_ref.dtype)


def _attn_seq(q, kb, vt, kmean, rel_bias, *, n_heads):
    b, t, w = q.shape
    nb = t // MOBA_BLOCK
    assert vt.shape == (n_heads * VT_HEAD_ROWS, b * t)
    assert t % MOBA_BLOCK == 0 and nb * n_heads <= LANES and nb % SUBLANES == 0
    return pl.pallas_call(
        functools.partial(_attn_seq_kernel, n_heads=n_heads, n_blocks=nb),
        grid=(b, nb),
        in_specs=[pl.BlockSpec(memory_space=pltpu.SMEM),
                  pl.BlockSpec((1, MOBA_BLOCK, w), lambda i, j: (i, j, 0)),
                  pl.BlockSpec((1, nb, w), lambda i, j: (i, 0, 0)),
                  pl.BlockSpec((1, t, w), lambda i, j: (i, 0, 0)),
                  pl.BlockSpec((n_heads * VT_HEAD_ROWS, t), lambda i, j: (0, i))],
        out_specs=pl.BlockSpec((1, MOBA_BLOCK, w), lambda i, j: (i, j, 0)),
        out_shape=jax.ShapeDtypeStruct((b, t, w), BF16),
        scratch_shapes=[pltpu.VMEM((n_heads, MOBA_BLOCK, MOBA_BLOCK), F32),
                        pltpu.VMEM((n_heads, MOBA_BLOCK, MOBA_BLOCK), F32),
                        pltpu.VMEM((LANES, w), F32),
                        pltpu.VMEM((LANES, MOBA_BLOCK), F32),
                        pltpu.VMEM((n_heads, VT_HEAD_ROWS, MOBA_BLOCK), F32),
                        pltpu.VMEM((n_heads, MOBA_BLOCK, MOBA_BLOCK), F32),
                        pltpu.VMEM((n_heads, MOBA_BLOCK, MOBA_BLOCK), F32)],
        compiler_params=_params("arbitrary", "arbitrary"),
        name="attn_seq",
    )(rel_bias, q, kmean, kb, vt)


SSM_HALO = 8


def _softplus(x):
    return jnp.maximum(x, 0.0) + jnp.log1p(jnp.exp(-jnp.abs(x)))


def _ssd_seq_kernel(xbc_ref, z_ref, dt_ref, cw_ref, cb_ref, dtb_ref, alog_ref, dvec_ref, ng_ref,
                    y_ref, hfin_ref, buf_ref, state_ref, *, n_heads):
    lc = SSD_CHUNK
    n = SSM_N
    d_inner = n_heads * SSM_P
    pair_w = 2 * SSM_P
    heads_per_group = n_heads // SSM_G
    chunk = pl.program_id(1)

    @pl.when(chunk == 0)
    def _():
        buf_ref[0:SSM_HALO, :] = jnp.zeros((SSM_HALO, buf_ref.shape[1]), F32)
        state_ref[...] = jnp.zeros(state_ref.shape, F32)

    buf_ref[SSM_HALO:SSM_HALO + lc, :] = xbc_ref[0]
    taps = cw_ref.shape[0]
    first = SSM_HALO - (taps - 1)
    acc = None
    for k in range(taps):
        term = cw_ref[k:k + 1, :] * buf_ref[first + k:first + k + lc, :]
        acc = term if acc is None else acc + term
    xc = _silu(acc + cb_ref[...])
    buf_ref[0:SSM_HALO, :] = buf_ref[lc:lc + SSM_HALO, :]
    xs = xc[:, :d_inner]
    bm = xc[:, d_inner:d_inner + SSM_G * n]
    cm = xc[:, d_inner + SSM_G * n:]

    dt = _softplus(dt_ref[0] + dtb_ref[...])
    dta = dt * (-jnp.exp(alog_ref[...]))
    ri = lax.broadcasted_iota(jnp.int32, (lc, lc), 0)
    ci = lax.broadcasted_iota(jnp.int32, (lc, lc), 1)
    causal = ri >= ci
    acum = _dot_exact_lhs(jnp.where(causal, 1.0, 0.0).astype(BF16), dta)
    acum_t = acum.T
    er = lax.broadcasted_iota(jnp.int32, (LANES, d_inner), 0)
    ec = lax.broadcasted_iota(jnp.int32, (LANES, d_inner), 1) // SSM_P
    dt_x = _dot_exact_rhs(dt, jnp.where(er == ec, 1.0, 0.0).astype(BF16))
    xdt = xs * dt_x
    xdt_b = xdt.astype(BF16)

    cb = []
    for g in range(SSM_G):
        cg = cm[:, g * n:(g + 1) * n].astype(BF16)
        bg = bm[:, g * n:(g + 1) * n].astype(BF16)
        cb.append(_dot_t(cg, bg))

    lane_p = lax.broadcasted_iota(jnp.int32, (lc, pair_w), 1)
    row_p = lax.broadcasted_iota(jnp.int32, (pair_w, n), 0)
    y_tiles = []
    for pair in range(n_heads // 2):
        sl = slice(pair * pair_w, (pair + 1) * pair_w)
        x_pair = xdt_b[:, sl]
        x_pair_t = xdt[:, sl].T.astype(BF16)
        st_old = state_ref[sl, :]
        st_old_b = st_old.astype(BF16)
        y_intra = None
        y_inter = []
        st_new = []
        for half in range(2):
            h = 2 * pair + half
            g = h // heads_per_group
            col = acum[:, h:h + 1]
            row = acum_t[h:h + 1, :]
            last = acum[lc - 1:lc, h:h + 1]
            decay = jnp.exp(jnp.where(causal, col - row, NEG_INF))
            scores = (cb[g] * decay).astype(BF16)
            in_half = (lane_p >= half * SSM_P) & (lane_p < (half + 1) * SSM_P)
            part = _dot(scores, jnp.where(in_half, x_pair, jnp.zeros_like(x_pair)))
            y_intra = part if y_intra is None else y_intra + part
            cw = (cm[:, g * n:(g + 1) * n] * jnp.exp(col)).astype(BF16)
            y_inter.append(_dot_t(cw, st_old_b))
            bw = (bm[:, g * n:(g + 1) * n] * jnp.exp(last - col)).astype(BF16)
            st_new.append(st_old * jnp.exp(last) + _dot(x_pair_t, bw))
        y_tiles.append(y_intra + jnp.where(lane_p < SSM_P, y_inter[0], y_inter[1]))
        state_ref[sl, :] = jnp.where(row_p < SSM_P, st_new[0], st_new[1])
    y = jnp.concatenate(y_tiles, axis=1) + dvec_ref[...] * xs
    gated = y * _silu(z_ref[0])
    y_ref[0] = _rmsnorm(gated, ng_ref[...]).astype(y_ref.dtype)

    @pl.when(chunk == pl.num_programs(1) - 1)
    def _():
        hfin_ref[0] = state_ref[...]


def _ssd_seq(xbc, z, dt, lw, *, n_heads):
    b, t, cd = xbc.shape
    d_inner = z.shape[2]
    lc = SSD_CHUNK
    assert t % lc == 0 and n_heads % 2 == 0 and n_heads % SSM_G == 0 and n_heads <= LANES
    taps = lw["ssm_cw"].shape[0]
    assert taps - 1 <= SSM_HALO
    tile = lambda w: pl.BlockSpec((1, lc, w), lambda i, j: (i, j, 0))
    return pl.pallas_call(
        functools.partial(_ssd_seq_kernel, n_heads=n_heads),
        grid=(b, t // lc),
        in_specs=[tile(cd), tile(d_inner), tile(LANES),
                  _const_spec((taps, cd)), _const_spec((1, cd)), _const_spec((1, LANES)),
                  _const_spec((1, LANES)), _const_spec((1, d_inner)), _const_spec((1, d_inner))],
        out_specs=[tile(d_inner), pl.BlockSpec((1, d_inner, SSM_N), lambda i, j: (i, 0, 0))],
        out_shape=[jax.ShapeDtypeStruct((b, t, d_inner), BF16),
                   jax.ShapeDtypeStruct((b, d_inner, SSM_N), F32)],
        scratch_shapes=[pltpu.VMEM((SSM_HALO + lc, cd), F32), pltpu.VMEM((d_inner, SSM_N), F32)],
        compiler_params=_params("arbitrary", "arbitrary"),
        name="ssd_seq",
    )(xbc, z, dt, lw["ssm_cw"], lw["ssm_cb"], lw["dt_bias"], lw["a_log"], lw["d_vec"], lw["ssm_ng"])


def _layer_weights(l, w):
    d_model = w["w_in"].shape[1]
    c_conv = w["conv_dw_w"].shape[2]
    a_w = (w["w_out"].shape[1] - c_conv - w["ssm_norm_g"].shape[1])
    d_inner = w["ssm_norm_g"].shape[1]
    cd = w["ssm_conv_w"].shape[2]
    n_ssm_heads = w["ssm_dt_bias"].shape[1]
    off_q = 2 * c_conv
    off_k = off_q + a_w
    off_v = off_k + a_w
    off_z = off_v + a_w
    off_xbc = off_z + d_inner
    off_dt = off_xbc + cd
    n_in = off_dt + n_ssm_heads
    assert w["w_in"].shape[2] == n_in
    n_pad = off_dt + LANES
    row = lambda a: a[l].reshape(1, -1).astype(F32)
    pad_row = lambda a: jnp.pad(a[l].astype(F32), (0, LANES - a.shape[1])).reshape(1, LANES)
    return dict(
        offs=(c_conv, off_q, off_k, off_v, off_z, off_xbc, off_dt, n_pad),
        n1=row(w["ffn1_norm"]), wg1=w["ffn1_wg"][l].astype(BF16), wu1=w["ffn1_wu"][l].astype(BF16),
        wd1=w["ffn1_wd"][l].astype(BF16), n_mix=row(w["mix_norm"]),
        w_in=jnp.pad(w["w_in"][l], ((0, 0), (0, n_pad - n_in))).astype(BF16),
        w_out=w["w_out"][l].astype(BF16),
        qg=jnp.tile(w["q_norm_g"][l], a_w // DH).reshape(1, a_w),
        kg=jnp.tile(w["k_norm_g"][l], a_w // DH).reshape(1, a_w),
        conv_w=w["conv_dw_w"][l], conv_b=row(w["conv_dw_b"]), ln_g=row(w["conv_ln_g"]), ln_b=row(w["conv_ln_b"]),
        ssm_cw=w["ssm_conv_w"][l], ssm_cb=row(w["ssm_conv_b"]),
        dt_bias=pad_row(w["ssm_dt_bias"]), a_log=pad_row(w["ssm_a_log"]),
        d_vec=jnp.repeat(w["ssm_d"][l], SSM_P).reshape(1, d_inner), ssm_ng=row(w["ssm_norm_g"]),
        n2=row(w["ffn2_norm"]), wg2=w["ffn2_wg"][l].astype(BF16), wu2=w["ffn2_wu"][l].astype(BF16),
        wd2=w["ffn2_wd"][l].astype(BF16), n_fin=row(w["final_norm"]),
        n_attn_heads=a_w // DH, n_ssm_heads=n_ssm_heads,
    )


def _row_tile(rows, want=512):
    tm = min(want, rows)
    assert rows % tm == 0 and tm % SUBLANES == 0
    return tm


def _layer_seq(x, lw, rel_bias):
    b, t, d = x.shape
    n_ah, n_sh = lw["n_attn_heads"], lw["n_ssm_heads"]
    tm = _row_tile(b * t)
    assert t % tm == 0 and tm % MOBA_BLOCK == 0
    x1, u, q, k, v, z, xbc, dt, kb, vt, km = _ffn_proj(x.reshape(b * t, d), lw, tm=tm, for_seq_attn=True)
    seq = lambda a: a.reshape(b, t, a.shape[-1])
    u, xbc = seq(u), seq(xbc)
    conv_out = _conv_seq(u, lw)
    attn_out = _attn_seq(seq(q), seq(kb), vt, km.reshape(b, t // MOBA_BLOCK, -1), rel_bias, n_heads=n_ah)
    ssm_out, h_fin = _ssd_seq(xbc, seq(z), seq(dt), lw, n_heads=n_sh)
    flat = lambda a: a.reshape(b * t, a.shape[-1])
    y = _out_ffn(x1, flat(conv_out), flat(attn_out), flat(ssm_out), lw, tm=tm)
    conv_taps = lw["conv_w"].shape[0]
    ssm_taps = lw["ssm_cw"].shape[0]
    states = (k.reshape(b, t, n_ah, DH), v.reshape(b, t, n_ah, DH), u[:, t - (conv_taps - 1):],
              xbc[:, t - (ssm_taps - 1):], h_fin.reshape(b, n_sh, SSM_P, SSM_N))
    return y.reshape(b, t, d), states


def _conv_step_kernel(u_ref, st_ref, w_ref, b_ref, g_ref, lb_ref, o_ref):
    past = st_ref.shape[0]
    acc = w_ref[past:past + 1, :] * u_ref[...]
    for k in range(past):
        acc = acc + w_ref[k:k + 1, :] * st_ref[k]
    o_ref[...] = _silu(_layernorm(acc + b_ref[...], g_ref[...], lb_ref[...])).astype(o_ref.dtype)


def _conv_step(u, state_t, lw):
    return pl.pallas_call(
        _conv_step_kernel,
        out_shape=jax.ShapeDtypeStruct(u.shape, BF16),
        compiler_params=pltpu.CompilerParams(vmem_limit_bytes=VMEM_LIMIT_BYTES),
        name="conv_step",
    )(u, state_t, lw["conv_w"], lw["conv_b"], lw["ln_g"], lw["ln_b"])


SSD_STEP_ROWS = 8


def _ssd_step_kernel(xn_ref, st_ref, z_ref, dt_ref, h0_ref, cw_ref, cb_ref, dtb_ref, alog_ref, dvec_ref, ng_ref,
                     y_ref, h1_ref, *, n_heads):
    rows = xn_ref.shape[0]
    n = SSM_N
    d_inner = n_heads * SSM_P
    group_rows = (n_heads // SSM_G) * SSM_P
    past = st_ref.shape[0]
    acc = cw_ref[past:past + 1, :] * xn_ref[...]
    for k in range(past):
        acc = acc + cw_ref[k:k + 1, :] * st_ref[k]
    xc = _silu(acc + cb_ref[...])
    xs = xc[:, :d_inner]
    bm = xc[:, d_inner:d_inner + SSM_G * n]
    cm = xc[:, d_inner + SSM_G * n:]
    dt = _softplus(dt_ref[...] + dtb_ref[...])
    dta = dt * (-jnp.exp(alog_ref[...]))
    er = lax.broadcasted_iota(jnp.int32, (LANES, d_inner), 0)
    ec = lax.broadcasted_iota(jnp.int32, (LANES, d_inner), 1) // SSM_P
    expand = jnp.where(er == ec, 1.0, 0.0).astype(BF16)
    xdt = xs * _dot_exact_rhs(dt, expand)
    dec = jnp.exp(_dot_exact_rhs(dta, expand))
    pad = jnp.zeros((LANES - rows, d_inner), F32)
    xdt_t = jnp.concatenate([xdt, pad], axis=0).T
    dec_t = jnp.concatenate([dec, pad], axis=0).T
    lane = lax.broadcasted_iota(jnp.int32, (d_inner, LANES), 1)
    y_t = jnp.zeros((d_inner, LANES), F32)
    for b in range(rows):
        xcol = xdt_t[:, b:b + 1]
        dcol = dec_t[:, b:b + 1]
        ycols = []
        for g in range(SSM_G):
            rs = slice(g * group_rows, (g + 1) * group_rows)
            h0 = h0_ref[b, rs, :]
            brow = bm[b:b + 1, g * n:(g + 1) * n]
            crow = cm[b:b + 1, g * n:(g + 1) * n]
            cb = jnp.sum(crow * brow, axis=1, keepdims=True)
            h1_ref[b, rs, :] = h0 * dcol[rs] + xcol[rs] * brow
            ycols.append(jnp.sum(h0 * crow, axis=1, keepdims=True) * dcol[rs] + cb * xcol[rs])
        y_t = jnp.where(lane == b, jnp.concatenate(ycols, axis=0), y_t)
    y = y_t.T[:rows] + dvec_ref[...] * xs
    gated = y * _silu(z_ref[...])
    y_ref[...] = _rmsnorm(gated, ng_ref[...]).astype(y_ref.dtype)


def _ssd_step(xn, state_t, z, dt, h0, lw, *, n_heads):
    db, cd = xn.shape
    d_inner = z.shape[1]
    rows = SSD_STEP_ROWS
    assert db % rows == 0
    past = state_t.shape[0]
    row = lambda w: pl.BlockSpec((rows, w), lambda i: (i, 0))
    return pl.pallas_call(
        functools.partial(_ssd_step_kernel, n_heads=n_heads),
        grid=(db // rows,),
        in_specs=[row(cd), pl.BlockSpec((past, rows, cd), lambda i: (0, i, 0)), row(d_inner), row(LANES),
                  pl.BlockSpec((rows, d_inner, SSM_N), lambda i: (i, 0, 0)),
                  _const_spec((past + 1, cd)), _const_spec((1, cd)), _const_spec((1, LANES)),
                  _const_spec((1, LANES)), _const_spec((1, d_inner)), _const_spec((1, d_inner))],
        out_specs=[row(d_inner), pl.BlockSpec((rows, d_inner, SSM_N), lambda i: (i, 0, 0))],
        out_shape=[jax.ShapeDtypeStruct((db, d_inner), BF16), jax.ShapeDtypeStruct(h0.shape, F32)],
        compiler_params=_params("arbitrary"),
        name="ssd_step",
    )(xn, state_t, z, dt, h0, lw["ssm_cw"], lw["ssm_cb"], lw["dt_bias"], lw["a_log"], lw["d_vec"], lw["ssm_ng"])


GATE_PAGES = 32


def _gate_pages_kernel(pt_ref, q_ref, *refs, n_blocks):
    del pt_ref
    page_refs, idx_ref, kmean_ref = refs[:-2], refs[-2], refs[-1]
    per_block = MOBA_BLOCK // PAGE_SIZE
    blocks = len(page_refs) // per_block
    chunk = pl.program_id(1)
    n_heads = kmean_ref.shape[0]

    @pl.when(chunk == 0)
    def _():
        kmean_ref[...] = jnp.zeros(kmean_ref.shape, F32)

    lane3 = lax.broadcasted_iota(jnp.int32, kmean_ref.shape, 2)
    km = kmean_ref[...]
    for blk in range(blocks):
        s = None
        for r in range(per_block):
            page = page_refs[blk * per_block + r][...]
            s = page if s is None else s + page
        col = jnp.sum(s, axis=2, keepdims=True) * (1.0 / MOBA_BLOCK)
        km = jnp.where(lane3 == chunk * blocks + blk, col, km)
    kmean_ref[...] = km

    @pl.when(chunk == pl.num_programs(1) - 1)
    def _():
        lane = lax.broadcasted_iota(jnp.int32, (SUBLANES, LANES), 1)
        for h in range(n_heads):
            q_hi, q_mid, q_lo = _split3(jnp.broadcast_to(q_ref[0, h], (SUBLANES, DH)))
            k_hi, k_mid, k_lo = _split3(kmean_ref[h])
            gate = (_dot(q_hi, k_hi) + _dot(q_hi, k_mid) + _dot(q_mid, k_hi)
                    + _dot(q_hi, k_lo) + _dot(q_lo, k_hi) + _dot(q_mid, k_mid))
            gate = jnp.where(lane < n_blocks, gate, NEG_INF)
            picks = jnp.zeros((SUBLANES, LANES), jnp.int32)
            for rank in range(MOBA_TOPK):
                mx = jnp.max(gate, axis=1, keepdims=True)
                idx = jnp.min(jnp.where(gate == mx, lane, LANES), axis=1, keepdims=True)
                picks = jnp.where(lane == rank, idx, picks)
                gate = jnp.where(lane == idx, NEG_INF, gate)
            idx_ref[0, h] = picks


def _gate_pages(q4, cache_t, layer, page_table_flat, n_pages):
    db, n_heads = q4.shape[:2]
    per_block = MOBA_BLOCK // PAGE_SIZE
    n_blocks = n_pages // per_block
    pg = min(GATE_PAGES, n_pages)
    assert n_pages % pg == 0 and pg % per_block == 0 and MOBA_TOPK <= n_blocks <= LANES

    def page_spec(j):
        return pl.BlockSpec((None, None, n_heads, DH, PAGE_SIZE),
                            lambda b, i, pt: (layer, pt[b * n_pages + i * pg + j], 0, 0, 0))

    return pl.pallas_call(
        functools.partial(_gate_pages_kernel, n_blocks=n_blocks),
        grid_spec=pltpu.PrefetchScalarGridSpec(
            num_scalar_prefetch=1, grid=(db, n_pages // pg),
            in_specs=[pl.BlockSpec((1, n_heads, 1, DH), lambda b, i, pt: (b, 0, 0, 0))]
            + [page_spec(j) for j in range(pg)],
            out_specs=pl.BlockSpec((1, n_heads, SUBLANES, LANES), lambda b, i, pt: (b, 0, 0, 0)),
            scratch_shapes=[pltpu.VMEM((n_heads, DH, LANES), F32)]),
        out_shape=jax.ShapeDtypeStruct((db, n_heads, SUBLANES, LANES), jnp.int32),
        compiler_params=_params("arbitrary", "arbitrary"),
        name="gate_pages",
    )(page_table_flat, q4, *([cache_t] * pg))


def _attn_step_kernel(pt_ref, sel_ref, relb_ref, q_ref, kn_ref, vn_ref, *refs, n_heads, past_len):
    del pt_ref
    per_block = MOBA_BLOCK // PAGE_SIZE
    n_pg = MOBA_TOPK * per_block
    k_pages, v_pages, o_ref = refs[:n_pg], refs[n_pg:2 * n_pg], refs[2 * n_pg]
    b, h = pl.program_id(0), pl.program_id(1)
    qh = q_ref[...] * (1.0 / math.sqrt(DH))
    q8 = jnp.broadcast_to(qh, (SUBLANES, DH)).astype(BF16)
    key_off = lax.broadcasted_iota(jnp.int32, (1, PAGE_SIZE), 1)
    s_own = jnp.sum(qh * kn_ref[...], axis=1, keepdims=True) + relb_ref[0, h]
    logits = []
    for j in range(MOBA_TOPK):
        blk = sel_ref[(b * n_heads + h) * MOBA_TOPK + j]
        for r in range(per_block):
            dist = past_len - (blk * MOBA_BLOCK + r * PAGE_SIZE + key_off)
            kt = k_pages[j * per_block + r][...].astype(BF16)
            logits.append(_dot(q8, kt)[0:1, :] + _rel_bias_tile(relb_ref, h, dist))
    m = s_own
    for s in logits:
        m = jnp.maximum(m, jnp.max(s, axis=1, keepdims=True))
    p_own = jnp.exp(s_own - m)
    l = p_own
    acc = p_own * vn_ref[...]
    for page, s in enumerate(logits):
        p = jnp.exp(s - m)
        l = l + jnp.sum(p, axis=1, keepdims=True)
        vt = v_pages[page][...].astype(BF16)
        acc = acc + _dot_t(jnp.broadcast_to(p, (SUBLANES, PAGE_SIZE)).astype(BF16), vt)[0:1, :]
    o_ref[...] = jnp.broadcast_to(acc / l, (SUBLANES, DH))


def _attn_step(q4, kn4, vn4, cache_kt, cache_vt, layer, page_table_flat, sel_flat, rel_bias, *, n_pages):
    db, n_heads = q4.shape[:2]
    per_block = MOBA_BLOCK // PAGE_SIZE
    past_len = n_pages * PAGE_SIZE
    assert past_len % MOBA_BLOCK == 0 and past_len // MOBA_BLOCK >= MOBA_TOPK

    def page_spec(j, r):
        def index_map(b, h, pt, sel):
            blk = sel[(b * n_heads + h) * MOBA_TOPK + j]
            return (layer, pt[b * n_pages + blk * per_block + r], h, 0, 0)
        return pl.BlockSpec((None, None, None, DH, PAGE_SIZE), index_map)

    page_specs = [page_spec(j, r) for j in range(MOBA_TOPK) for r in range(per_block)]
    tok = pl.BlockSpec((None, None, 1, DH), lambda b, h, pt, sel: (b, h, 0, 0))
    n_pg = len(page_specs)
    return pl.pallas_call(
        functools.partial(_attn_step_kernel, n_heads=n_heads, past_len=past_len),
        grid_spec=pltpu.PrefetchScalarGridSpec(
            num_scalar_prefetch=2, grid=(db, n_heads),
            in_specs=[pl.BlockSpec(memory_space=pltpu.SMEM), tok, tok, tok] + page_specs + page_specs,
            out_specs=pl.BlockSpec((None, None, SUBLANES, DH), lambda b, h, pt, sel: (b, h, 0, 0))),
        out_shape=jax.ShapeDtypeStruct((db, n_heads, SUBLANES, DH), F32),
        compiler_params=_params("arbitrary", "arbitrary"),
        name="attn_step",
    )(page_table_flat, sel_flat, rel_bias, q4, kn4, vn4, *([cache_kt] * n_pg), *([cache_vt] * n_pg))


def _layer_step(x, conv_state, ssm_conv_state, ssm_state, cache_kt, cache_vt, layer, page_table, lw, rel_bias):
    db, t, d = x.shape
    assert t == 1
    n_ah, n_sh = lw["n_attn_heads"], lw["n_ssm_heads"]
    tm = _row_tile(db)
    x1, u, q, k, v, z, xbc, dt = _ffn_proj(x.reshape(db, d), lw, tm=tm, for_seq_attn=False)
    conv_out = _conv_step(u, conv_state.transpose(1, 0, 2), lw)
    n_pages = page_table.shape[1]
    pt_flat = page_table.reshape(-1)
    q4 = q.reshape(db, n_ah, 1, DH)
    picks = _gate_pages(q4, cache_kt, layer, pt_flat, n_pages)
    sel_flat = picks[:, :, 0, :MOBA_TOPK].reshape(-1)
    attn = _attn_step(q4, k.reshape(db, n_ah, 1, DH), v.reshape(db, n_ah, 1, DH), cache_kt, cache_vt, layer,
                      pt_flat, sel_flat, rel_bias, n_pages=n_pages)
    attn_out = attn[:, :, 0, :].reshape(db, n_ah * DH)
    ssm_out, h_new = _ssd_step(xbc, ssm_conv_state.transpose(1, 0, 2), z, dt,
                               ssm_state.reshape(db, n_sh * SSM_P, SSM_N), lw, n_heads=n_sh)
    y = _out_ffn(x1, conv_out, attn_out.astype(BF16), ssm_out, lw, tm=tm)
    states = (k.reshape(db, 1, n_ah, DH), v.reshape(db, 1, n_ah, DH),
              jnp.concatenate([conv_state[:, 1:], u[:, None, :]], axis=1),
              jnp.concatenate([ssm_conv_state[:, 1:], xbc[:, None, :]], axis=1),
              h_new.reshape(ssm_state.shape))
    return y.reshape(db, 1, d), states


def kernel(x_prompt, x_sample, cache_k, cache_v, state_conv, state_ssm_conv, state_ssm, page_table, rel_bias, ffn1_norm, ffn1_wg, ffn1_wu, ffn1_wd, mix_norm, w_in, w_out, conv_dw_w, conv_dw_b, conv_ln_g, conv_ln_b, q_norm_g, k_norm_g, ssm_conv_w, ssm_conv_b, ssm_dt_bias, ssm_a_log, ssm_d, ssm_norm_g, ffn2_norm, ffn2_wg, ffn2_wu, ffn2_wd, final_norm):
    w = dict(ffn1_norm=ffn1_norm, ffn1_wg=ffn1_wg, ffn1_wu=ffn1_wu, ffn1_wd=ffn1_wd, mix_norm=mix_norm,
             w_in=w_in, w_out=w_out, conv_dw_w=conv_dw_w, conv_dw_b=conv_dw_b, conv_ln_g=conv_ln_g,
             conv_ln_b=conv_ln_b, q_norm_g=q_norm_g, k_norm_g=k_norm_g, ssm_conv_w=ssm_conv_w,
             ssm_conv_b=ssm_conv_b, ssm_dt_bias=ssm_dt_bias, ssm_a_log=ssm_a_log, ssm_d=ssm_d,
             ssm_norm_g=ssm_norm_g, ffn2_norm=ffn2_norm, ffn2_wg=ffn2_wg, ffn2_wu=ffn2_wu, ffn2_wd=ffn2_wd,
             final_norm=final_norm)
    depth = w_in.shape[0]
    y_p, y_s = x_prompt, x_sample
    st_p, st_s = [], []
    cache_kt = cache_k.transpose(0, 1, 3, 4, 2)
    cache_vt = cache_v.transpose(0, 1, 3, 4, 2)
    for l in range(depth):
        lw = _layer_weights(l, w)
        y_p, st = _layer_seq(y_p, lw, rel_bias)
        st_p.append(st)
        y_s, st = _layer_step(y_s, state_conv[l], state_ssm_conv[l], state_ssm[l], cache_kt, cache_vt, l,
                              page_table, lw, rel_bias)
        st_s.append(st)
    stack = lambda sts, i: jnp.stack([s[i] for s in sts])
    return (y_p, y_s) + tuple(stack(st_p, i) for i in range(5)) + tuple(stack(st_s, i) for i in range(5))
```

```python
import functools
import math

import jax
import jax.numpy as jnp
from jax import lax
from jax.experimental import pallas as pl
from jax.experimental.pallas import tpu as pltpu

F32 = jnp.float32
BF16 = jnp.bfloat16

DH = 64
MOBA_BLOCK = 256
MOBA_TOPK = 3
NUM_BUCKETS = 32
REL_MAX_DIST = 128
SSM_P = 64
SSM_G = 2
SSM_N = 128
SSD_CHUNK = 256
PAGE_SIZE = 128

LANES = 128
SUBLANES = 8
VMEM_LIMIT_BYTES = 56 * 1024 * 1024
BF16_SUBLANES = 16

VT_ONES = BF16_SUBLANES
VT_HEAD_ROWS = DH + VT_ONES
LOG2E = math.log2(math.e)

NEG_INF = float("-inf")


def _params(*semantics):
    return pltpu.CompilerParams(dimension_semantics=semantics, vmem_limit_bytes=VMEM_LIMIT_BYTES)


def _dot(a, b):
    return jnp.dot(a, b, preferred_element_type=F32)


def _dot_t(a, b):
    return lax.dot_general(a, b, (((1,), (1,)), ((), ())), preferred_element_type=F32)


def _split3(x):
    hi = x.astype(BF16)
    r1 = x - hi.astype(F32)
    mid = r1.astype(BF16)
    lo = (r1 - mid.astype(F32)).astype(BF16)
    return hi, mid, lo


def _dot_exact_rhs(x, sel):
    hi, mid, lo = _split3(x)
    return _dot(hi, sel) + _dot(mid, sel) + _dot(lo, sel)


def _dot_exact_lhs(sel, x):
    hi, mid, lo = _split3(x)
    return _dot(sel, hi) + _dot(sel, mid) + _dot(sel, lo)


def _sigmoid(x):
    return 1.0 / (1.0 + jnp.exp(-x))


def _silu(x):
    return x * _sigmoid(x)


def _rmsnorm(x, g, eps=1e-6):
    return x * lax.rsqrt(jnp.mean(x * x, axis=-1, keepdims=True) + eps) * g


def _const_spec(shape):
    nd = len(shape)
    return pl.BlockSpec(shape, lambda *_: (0,) * nd, pipeline_mode=pl.Buffered(1))


def _layer_spec(stacked, layer):
    return pl.BlockSpec((None,) + stacked.shape[1:], lambda *_: (layer, 0, 0), pipeline_mode=pl.Buffered(1))


def _swiglu(hb, wg_ref, wu_ref, wd_ref, fchunk):
    d_ff = wg_ref.shape[1]
    acc = None
    for c in range(d_ff // fchunk):
        sl = slice(c * fchunk, (c + 1) * fchunk)
        g = _dot(hb, wg_ref[:, sl])
        u = _dot(hb, wu_ref[:, sl])
        a = (_silu(g) * u).astype(BF16)
        part = _dot(a, wd_ref[sl, :])
        acc = part if acc is None else acc + part
    return acc


def _head_sumsq_matrix(width):
    r = lax.broadcasted_iota(jnp.int32, (width, width), 0) // DH
    c = lax.broadcasted_iota(jnp.int32, (width, width), 1) // DH
    return jnp.where(r == c, 1.0, 0.0).astype(BF16)


def _head_rmsnorm(p, g_row, eps=1e-6):
    ss = _dot_exact_rhs(p * p, _head_sumsq_matrix(p.shape[1]))
    return p * lax.rsqrt(ss * (1.0 / DH) + eps) * g_row


def _ffn_proj_kernel(x_ref, n1_ref, wg_ref, wu_ref, wd_ref, n2_ref, win_ref, qg_ref, kg_ref,
                     x1_ref, u_ref, q_ref, k_ref, v_ref, z_ref, xbc_ref, dt_ref,
                     *maybe_attn_refs, fchunk, offs):
    c_conv, off_q, off_k, off_v, off_z, off_xbc, off_dt, n_pad = offs
    x = x_ref[...]
    hb = _rmsnorm(x, n1_ref[...]).astype(BF16)
    x1 = x + 0.5 * _swiglu(hb, wg_ref, wu_ref, wd_ref, fchunk)
    x1_ref[...] = x1
    h2 = _rmsnorm(x1, n2_ref[...]).astype(BF16)
    pg = _dot(h2, win_ref[:, 0:off_q])
    u_ref[...] = pg[:, :c_conv] * _sigmoid(pg[:, c_conv:])
    q_ref[...] = _head_rmsnorm(_dot(h2, win_ref[:, off_q:off_k]), qg_ref[...])
    k = _head_rmsnorm(_dot(h2, win_ref[:, off_k:off_v]), kg_ref[...])
    k_ref[...] = k
    v = _dot(h2, win_ref[:, off_v:off_z])
    v_ref[...] = v
    z_ref[...] = _dot(h2, win_ref[:, off_z:off_xbc])
    xbc_ref[...] = _dot(h2, win_ref[:, off_xbc:off_dt])
    dt_ref[...] = _dot(h2, win_ref[:, off_dt:n_pad])
    if maybe_attn_refs:
        kb_ref, vt_ref, km_ref = maybe_attn_refs
        kb_ref[...] = k.astype(BF16)
        vt = v.T.astype(BF16)
        ones = jnp.ones((VT_ONES, vt.shape[1]), BF16)
        vt_ref[...] = jnp.concatenate(
            [part for h in range(vt.shape[0] // DH) for part in (vt[h * DH:(h + 1) * DH], ones)], axis=0)
        nb = k.shape[0] // MOBA_BLOCK
        km_ref[0] = jnp.mean(k.reshape(nb, MOBA_BLOCK, k.shape[1]), axis=1)


def _ffn_proj(x, lw, *, tm, for_seq_attn):
    r, d = x.shape
    d_ff = lw["wg1"].shape[2]
    layer = lw["layer"]
    offs = lw["offs"]
    c_conv, off_q, off_k, off_v, off_z, off_xbc, off_dt, n_pad = offs
    a_w = off_k - off_q
    widths = dict(u=c_conv, q=a_w, k=a_w, v=a_w, z=off_xbc - off_z, xbc=off_dt - off_xbc, dt=n_pad - off_dt)
    row = lambda w: pl.BlockSpec((tm, w), lambda i: (i, 0))
    out_shape = [jax.ShapeDtypeStruct((r, d), F32),
                 jax.ShapeDtypeStruct((r, widths["u"]), F32),
                 jax.ShapeDtypeStruct((r, a_w), F32),
                 jax.ShapeDtypeStruct((r, a_w), F32),
                 jax.ShapeDtypeStruct((r, a_w), F32),
                 jax.ShapeDtypeStruct((r, widths["z"]), F32),
                 jax.ShapeDtypeStruct((r, widths["xbc"]), F32),
                 jax.ShapeDtypeStruct((r, widths["dt"]), F32)]
    out_specs = [row(d), row(widths["u"]), row(a_w), row(a_w), row(a_w),
                 row(widths["z"]), row(widths["xbc"]), row(widths["dt"])]
    if for_seq_attn:
        nb = tm // MOBA_BLOCK
        vt_rows = (a_w // DH) * VT_HEAD_ROWS
        out_shape += [jax.ShapeDtypeStruct((r, a_w), BF16), jax.ShapeDtypeStruct((vt_rows, r), BF16),
                      jax.ShapeDtypeStruct((r // tm, nb, a_w), F32)]
        out_specs += [row(a_w), pl.BlockSpec((vt_rows, tm), lambda i: (0, i)),
                      pl.BlockSpec((1, nb, a_w), lambda i: (i, 0, 0))]
    fchunk = 256 if d_ff % 256 == 0 else d_ff
    outs = pl.pallas_call(
        functools.partial(_ffn_proj_kernel, fchunk=fchunk, offs=offs),
        grid=(r // tm,),
        in_specs=[row(d), _const_spec((1, d)), _layer_spec(lw["wg1"], layer), _layer_spec(lw["wu1"], layer),
                  _layer_spec(lw["wd1"], layer), _const_spec((1, d)), _layer_spec(lw["w_in"], layer),
                  _const_spec((1, a_w)), _const_spec((1, a_w))],
        out_specs=out_specs,
        out_shape=out_shape,
        compiler_params=_params("arbitrary"),
        name="ffn_proj",
    )(x, lw["n1"], lw["wg1"], lw["wu1"], lw["wd1"], lw["n_mix"], lw["w_in"], lw["qg"], lw["kg"])
    return outs


def _out_ffn_kernel(x1_ref, a_ref, b_ref, c_ref, wo_ref, n_ref, wg_ref, wu_ref, wd_ref, fn_ref, y_ref,
                    *, fchunk):
    wa, wb = a_ref.shape[1], b_ref.shape[1]
    mix = (_dot(a_ref[...], wo_ref[0:wa, :]) + _dot(b_ref[...], wo_ref[wa:wa + wb, :])
           + _dot(c_ref[...], wo_ref[wa + wb:, :]))
    x2 = x1_ref[...] + mix
    hb = _rmsnorm(x2, n_ref[...]).astype(BF16)
    x3 = x2 + 0.5 * _swiglu(hb, wg_ref, wu_ref, wd_ref, fchunk)
    y_ref[...] = _rmsnorm(x3, fn_ref[...])


def _out_ffn(x1, conv_out, attn_out, ssm_out, lw, *, tm):
    r, d = x1.shape
    d_ff = lw["wg2"].shape[2]
    layer = lw["layer"]
    row = lambda w: pl.BlockSpec((tm, w), lambda i: (i, 0))
    fchunk = 256 if d_ff % 256 == 0 else d_ff
    return pl.pallas_call(
        functools.partial(_out_ffn_kernel, fchunk=fchunk),
        grid=(r // tm,),
        in_specs=[row(d), row(conv_out.shape[1]), row(attn_out.shape[1]), row(ssm_out.shape[1]),
                  _layer_spec(lw["w_out"], layer), _const_spec((1, d)), _layer_spec(lw["wg2"], layer),
                  _layer_spec(lw["wu2"], layer), _layer_spec(lw["wd2"], layer), _const_spec((1, d))],
        out_specs=row(d),
        out_shape=jax.ShapeDtypeStruct((r, d), F32),
        compiler_params=_params("arbitrary"),
        name="out_ffn",
    )(x1, conv_out, attn_out, ssm_out, lw["w_out"], lw["n2"], lw["wg2"], lw["wu2"], lw["wd2"], lw["n_fin"])


CONV_HALO = 32
CONV_SUB = 128


def _layernorm(c, g, b, eps=1e-5):
    xc = c - jnp.mean(c, axis=-1, keepdims=True)
    var = jnp.mean(xc * xc, axis=-1, keepdims=True)
    return xc * lax.rsqrt(var + eps) * g + b


def _conv_seq_kernel(u_ref, w_ref, b_ref, g_ref, lb_ref, o_ref, buf_ref, shift_ref, *, taps):
    tc = u_ref.shape[1]

    @pl.when(pl.program_id(1) == 0)
    def _():
        buf_ref[0:CONV_HALO, :] = jnp.zeros((CONV_HALO, buf_ref.shape[1]), F32)

    buf_ref[CONV_HALO:CONV_HALO + tc, :] = u_ref[0]
    first = CONV_HALO - (taps - 1)
    residues = sorted({(first + k) % SUBLANES for k in range(taps)})
    for r in residues:
        rows = max(first + k - r for k in range(taps) if (first + k) % SUBLANES == r) + tc
        shift_ref[r, 0:rows, :] = buf_ref[r:r + rows, :]
    for s in range(tc // CONV_SUB):
        acc = None
        for k in range(taps):
            r = (first + k) % SUBLANES
            off = s * CONV_SUB + first + k - r
            term = w_ref[k:k + 1, :] * shift_ref[r, off:off + CONV_SUB, :]
            acc = term if acc is None else acc + term
        c = acc + b_ref[...]
        o_ref[0, s * CONV_SUB:(s + 1) * CONV_SUB, :] = _silu(_layernorm(c, g_ref[...], lb_ref[...])).astype(o_ref.dtype)
    buf_ref[0:CONV_HALO, :] = buf_ref[tc:tc + CONV_HALO, :]


def _conv_seq(u, lw, *, tc=256):
    b, t, c = u.shape
    taps = lw["conv_w"].shape[0]
    assert taps - 1 <= CONV_HALO and t % tc == 0 and tc % CONV_SUB == 0
    return pl.pallas_call(
        functools.partial(_conv_seq_kernel, taps=taps),
        grid=(b, t // tc),
        in_specs=[pl.BlockSpec((1, tc, c), lambda i, j: (i, j, 0)),
                  _const_spec((taps, c)), _const_spec((1, c)), _const_spec((1, c)), _const_spec((1, c))],
        out_specs=pl.BlockSpec((1, tc, c), lambda i, j: (i, j, 0)),
        out_shape=jax.ShapeDtypeStruct((b, t, c), BF16),
        scratch_shapes=[pltpu.VMEM((CONV_HALO + tc, c), F32), pltpu.VMEM((SUBLANES, CONV_HALO + tc, c), F32)],
        compiler_params=_params("arbitrary", "arbitrary"),
        name="conv_seq",
    )(u, lw["conv_w"], lw["conv_b"], lw["ln_g"], lw["ln_b"])


def _rel_bias_tile(relb_ref, head, dist):
    max_exact = NUM_BUCKETS // 2
    n = jnp.maximum(dist, 0)
    nf = jnp.maximum(n, 1).astype(F32)
    large = max_exact + (jnp.log(nf / max_exact) / math.log(REL_MAX_DIST / max_exact)
                         * (NUM_BUCKETS - max_exact)).astype(jnp.int32)
    large = jnp.minimum(large, NUM_BUCKETS - 1)
    bucket = jnp.where(n < max_exact, n, large)
    out = jnp.zeros(dist.shape, F32)
    for b in range(NUM_BUCKETS):
        out = jnp.where(bucket == b, relb_ref[b, head], out)
    return out


def _moba_select_t(gate_t, n_eligible):
    nb = gate_t.shape[0]
    row = lax.broadcasted_iota(jnp.int32, gate_t.shape, 0).astype(F32)
    eligible = row < n_eligible.astype(F32)
    gate_t = jnp.where(eligible, gate_t, NEG_INF)
    sel = jnp.full(gate_t.shape, NEG_INF, F32)
    for _ in range(MOBA_TOPK):
        mx = jnp.max(gate_t, axis=0, keepdims=True)
        idx = jnp.min(jnp.where(gate_t == mx, row, float(nb)), axis=0, keepdims=True)
        pick = row == idx
        sel = jnp.where(pick, jnp.where(eligible, 0.0, sel), sel)
        gate_t = jnp.where(pick, NEG_INF, gate_t)
    return sel


def _attn_seq_kernel(relb_ref, q_ref, km_ref, k_ref, vt_ref, o_ref,
                     bias_own_ref, bias_prev_ref, kmbd_ref, selt_ref, acc_ref, s0_ref, s1_ref, *, n_heads, n_blocks):
    bs = MOBA_BLOCK
    width = n_heads * DH
    cur = pl.program_id(1)

    @pl.when((pl.program_id(0) == 0) & (cur == 0))
    def _():
        ki = lax.broadcasted_iota(jnp.int32, (bs, bs), 0)
        qi = lax.broadcasted_iota(jnp.int32, (bs, bs), 1)
        for h in range(n_heads):
            own = _rel_bias_tile(relb_ref, h, qi - ki) * LOG2E
            bias_own_ref[h] = jnp.where(qi >= ki, own, NEG_INF)
            bias_prev_ref[h] = _rel_bias_tile(relb_ref, h, qi - ki + bs) * LOG2E

    @pl.when(cur == 0)
    def _():
        kmbd_ref[...] = jnp.zeros(kmbd_ref.shape, F32)
        km = km_ref[0]
        km_lane = lax.broadcasted_iota(jnp.int32, km.shape, 1)
        for h in range(n_heads):
            kmbd_ref[h * n_blocks:(h + 1) * n_blocks, :] = jnp.where(
                (km_lane >= h * DH) & (km_lane < (h + 1) * DH), km, 0.0)

    qt = q_ref[0].T
    q_hi, q_mid, q_lo = _split3(qt)
    km_hi, km_mid, km_lo = _split3(kmbd_ref[...])
    gate_t = (_dot(km_hi, q_hi) + _dot(km_mid, q_hi) + _dot(km_hi, q_mid)
              + _dot(km_lo, q_hi) + _dot(km_hi, q_lo) + _dot(km_mid, q_mid))
    for h in range(n_heads):
        rows = slice(h * n_blocks, (h + 1) * n_blocks)
        selt_ref[rows, :] = _moba_select_t(gate_t[rows], cur)
    qs = qt * (LOG2E / math.sqrt(DH))
    row_w = lax.broadcasted_iota(jnp.int32, (width, bs), 0)
    qht = [jnp.where((row_w >= h * DH) & (row_w < (h + 1) * DH), qs, 0.0).astype(BF16) for h in range(n_heads)]
    far_bias = [relb_ref[NUM_BUCKETS - 1, h] * LOG2E for h in range(n_heads)]

    def scores(blk):
        kb = k_ref[0, pl.ds(pl.multiple_of(blk * bs, bs), bs), :]
        return [_dot(kb, qht[h]) for h in range(n_heads)]

    def softmax_pv(blk, s_all, ms, bias):
        start = pl.multiple_of(blk * bs, bs)
        out = []
        for h in range(n_heads):
            m = ms[h]
            s = s_all[h]()
            bh = bias(h)
            if bh.shape[0] == 1:
                m_new = jnp.maximum(m, jnp.max(s, axis=0, keepdims=True) + bh)
                p = jnp.exp2(s + (bh - m_new))
            else:
                s = s + bh
                m_new = jnp.maximum(m, jnp.max(s, axis=0, keepdims=True))
                p = jnp.exp2(s - m_new)
            alpha = jnp.exp2(m - m_new)
            vth = vt_ref[h * VT_HEAD_ROWS:(h + 1) * VT_HEAD_ROWS, pl.ds(start, bs)]
            acc_ref[h] = alpha * acc_ref[h] + _dot(vth, p.astype(BF16))
            out.append(m_new)
        return tuple(out)

    def sel_row(h, j):
        return selt_ref[pl.ds(h * n_blocks + j, 1), :]

    def attend(blk, ms, bias):
        s_all = scores(blk)
        return softmax_pv(blk, [lambda s=s: s for s in s_all], ms, bias)

    acc_ref[...] = jnp.zeros(acc_ref.shape, F32)
    ms = (jnp.full((1, bs), NEG_INF, F32),) * n_heads
    ms = attend(cur, ms, lambda h: bias_own_ref[h])
    ms = lax.cond(cur >= 1,
                  lambda ms: attend(cur - 1, ms, lambda h: bias_prev_ref[h] + sel_row(h, cur - 1)),
                  lambda ms: ms, ms)

    n_far = jnp.maximum(cur - 1, 0)
    last_far = jnp.maximum(n_far - 1, 0)

    def put_scores(s_ref, blk):
        for h, s in enumerate(scores(jnp.minimum(blk, last_far))):
            s_ref[h] = s

    def far_bias_row(h, blk):
        row = sel_row(h, jnp.minimum(blk, last_far)) + far_bias[h]
        return jnp.where(blk < n_far, row, NEG_INF)

    def far_pair(i, ms):
        b0, b1 = 2 * i, 2 * i + 1
        put_scores(s1_ref, b1)
        ms = softmax_pv(b0, [lambda h=h: s0_ref[h] for h in range(n_heads)], ms, lambda h: far_bias_row(h, b0))
        put_scores(s0_ref, b1 + 1)
        b1c = jnp.minimum(b1, last_far)
        return softmax_pv(b1c, [lambda h=h: s1_ref[h] for h in range(n_heads)], ms, lambda h: far_bias_row(h, b1))

    put_scores(s0_ref, 0)
    lax.fori_loop(0, (n_far + 1) // 2, far_pair, ms)
    out_t = jnp.concatenate([acc_ref[h, :DH] * (1.0 / acc_ref[h, DH:DH + 1]) for h in range(n_heads)], axis=0)
    o_ref[0] = out_t.T.astype(o_ref.dtype)


def _attn_seq(q, kb, vt, kmean, rel_bias, *, n_heads):
    b, t, w = q.shape
    nb = t // MOBA_BLOCK
    assert vt.shape == (n_heads * VT_HEAD_ROWS, b * t)
    assert t % MOBA_BLOCK == 0 and nb * n_heads <= LANES and nb % SUBLANES == 0
    return pl.pallas_call(
        functools.partial(_attn_seq_kernel, n_heads=n_heads, n_blocks=nb),
        grid=(b, nb),
        in_specs=[pl.BlockSpec(memory_space=pltpu.SMEM),
                  pl.BlockSpec((1, MOBA_BLOCK, w), lambda i, j: (i, j, 0)),
                  pl.BlockSpec((1, nb, w), lambda i, j: (i, 0, 0)),
                  pl.BlockSpec((1, t, w), lambda i, j: (i, 0, 0)),
                  pl.BlockSpec((n_heads * VT_HEAD_ROWS, t), lambda i, j: (0, i))],
        out_specs=pl.BlockSpec((1, MOBA_BLOCK, w), lambda i, j: (i, j, 0)),
        out_shape=jax.ShapeDtypeStruct((b, t, w), BF16),
        scratch_shapes=[pltpu.VMEM((n_heads, MOBA_BLOCK, MOBA_BLOCK), F32),
                        pltpu.VMEM((n_heads, MOBA_BLOCK, MOBA_BLOCK), F32),
                        pltpu.VMEM((LANES, w), F32),
                        pltpu.VMEM((LANES, MOBA_BLOCK), F32),
                        pltpu.VMEM((n_heads, VT_HEAD_ROWS, MOBA_BLOCK), F32),
                        pltpu.VMEM((n_heads, MOBA_BLOCK, MOBA_BLOCK), F32),
                        pltpu.VMEM((n_heads, MOBA_BLOCK, MOBA_BLOCK), F32)],
        compiler_params=_params("arbitrary", "arbitrary"),
        name="attn_seq",
    )(rel_bias, q, kmean, kb, vt)


SSM_HALO = 8


def _softplus(x):
    return jnp.maximum(x, 0.0) + jnp.log1p(jnp.exp(-jnp.abs(x)))


def _ssd_seq_kernel(xbc_ref, z_ref, dt_ref, cw_ref, cb_ref, dtb_ref, alog_ref, dvec_ref, ng_ref,
                    y_ref, hfin_ref, buf_ref, state_ref, *, n_heads):
    lc = SSD_CHUNK
    n = SSM_N
    d_inner = n_heads * SSM_P
    pair_w = 2 * SSM_P
    heads_per_group = n_heads // SSM_G
    chunk = pl.program_id(1)

    @pl.when(chunk == 0)
    def _():
        buf_ref[0:SSM_HALO, :] = jnp.zeros((SSM_HALO, buf_ref.shape[1]), F32)
        state_ref[...] = jnp.zeros(state_ref.shape, F32)

    buf_ref[SSM_HALO:SSM_HALO + lc, :] = xbc_ref[0]
    taps = cw_ref.shape[0]
    first = SSM_HALO - (taps - 1)
    acc = None
    for k in range(taps):
        term = cw_ref[k:k + 1, :] * buf_ref[first + k:first + k + lc, :]
        acc = term if acc is None else acc + term
    xc = _silu(acc + cb_ref[...])
    buf_ref[0:SSM_HALO, :] = buf_ref[lc:lc + SSM_HALO, :]
    xs = xc[:, :d_inner]
    bm = xc[:, d_inner:d_inner + SSM_G * n]
    cm = xc[:, d_inner + SSM_G * n:]

    dt = _softplus(dt_ref[0] + dtb_ref[...])
    dta = dt * (-jnp.exp(alog_ref[...]))
    ri = lax.broadcasted_iota(jnp.int32, (lc, lc), 0)
    ci = lax.broadcasted_iota(jnp.int32, (lc, lc), 1)
    causal = ri >= ci
    acum = _dot_exact_lhs(jnp.where(causal, 1.0, 0.0).astype(BF16), dta)
    acum_t = acum.T
    er = lax.broadcasted_iota(jnp.int32, (LANES, d_inner), 0)
    ec = lax.broadcasted_iota(jnp.int32, (LANES, d_inner), 1) // SSM_P
    dt_x = _dot_exact_rhs(dt, jnp.where(er == ec, 1.0, 0.0).astype(BF16))
    xdt = xs * dt_x
    xdt_b = xdt.astype(BF16)

    cb = []
    for g in range(SSM_G):
        cg = cm[:, g * n:(g + 1) * n].astype(BF16)
        bg = bm[:, g * n:(g + 1) * n].astype(BF16)
        cb.append(_dot_t(cg, bg))

    lane_p = lax.broadcasted_iota(jnp.int32, (lc, pair_w), 1)
    row_p = lax.broadcasted_iota(jnp.int32, (pair_w, n), 0)
    y_tiles = []
    for pair in range(n_heads // 2):
        sl = slice(pair * pair_w, (pair + 1) * pair_w)
        x_pair = xdt_b[:, sl]
        x_pair_t = xdt[:, sl].T.astype(BF16)
        st_old = state_ref[sl, :]
        st_old_b = st_old.astype(BF16)
        y_intra = None
        y_inter = []
        st_new = []
        for half in range(2):
            h = 2 * pair + half
            g = h // heads_per_group
            col = acum[:, h:h + 1]
            row = acum_t[h:h + 1, :]
            last = acum[lc - 1:lc, h:h + 1]
            decay = jnp.exp(jnp.where(causal, col - row, NEG_INF))
            scores = (cb[g] * decay).astype(BF16)
            in_half = (lane_p >= half * SSM_P) & (lane_p < (half + 1) * SSM_P)
            part = _dot(scores, jnp.where(in_half, x_pair, jnp.zeros_like(x_pair)))
            y_intra = part if y_intra is None else y_intra + part
            cw = (cm[:, g * n:(g + 1) * n] * jnp.exp(col)).astype(BF16)
            y_inter.append(_dot_t(cw, st_old_b))
            bw = (bm[:, g * n:(g + 1) * n] * jnp.exp(last - col)).astype(BF16)
            st_new.append(st_old * jnp.exp(last) + _dot(x_pair_t, bw))
        y_tiles.append(y_intra + jnp.where(lane_p < SSM_P, y_inter[0], y_inter[1]))
        state_ref[sl, :] = jnp.where(row_p < SSM_P, st_new[0], st_new[1])
    y = jnp.concatenate(y_tiles, axis=1) + dvec_ref[...] * xs
    gated = y * _silu(z_ref[0])
    y_ref[0] = _rmsnorm(gated, ng_ref[...]).astype(y_ref.dtype)

    @pl.when(chunk == pl.num_programs(1) - 1)
    def _():
        hfin_ref[0] = state_ref[...]


def _ssd_seq(xbc, z, dt, lw, *, n_heads):
    b, t, cd = xbc.shape
    d_inner = z.shape[2]
    lc = SSD_CHUNK
    assert t % lc == 0 and n_heads % 2 == 0 and n_heads % SSM_G == 0 and n_heads <= LANES
    taps = lw["ssm_cw"].shape[0]
    assert taps - 1 <= SSM_HALO
    tile = lambda w: pl.BlockSpec((1, lc, w), lambda i, j: (i, j, 0))
    return pl.pallas_call(
        functools.partial(_ssd_seq_kernel, n_heads=n_heads),
        grid=(b, t // lc),
        in_specs=[tile(cd), tile(d_inner), tile(LANES),
                  _const_spec((taps, cd)), _const_spec((1, cd)), _const_spec((1, LANES)),
                  _const_spec((1, LANES)), _const_spec((1, d_inner)), _const_spec((1, d_inner))],
        out_specs=[tile(d_inner), pl.BlockSpec((1, d_inner, SSM_N), lambda i, j: (i, 0, 0))],
        out_shape=[jax.ShapeDtypeStruct((b, t, d_inner), BF16),
                   jax.ShapeDtypeStruct((b, d_inner, SSM_N), F32)],
        scratch_shapes=[pltpu.VMEM((SSM_HALO + lc, cd), F32), pltpu.VMEM((d_inner, SSM_N), F32)],
        compiler_params=_params("arbitrary", "arbitrary"),
        name="ssd_seq",
    )(xbc, z, dt, lw["ssm_cw"], lw["ssm_cb"], lw["dt_bias"], lw["a_log"], lw["d_vec"], lw["ssm_ng"])


def _stacked_matrices(w):
    n_in = w["w_in"].shape[2]
    n_pad = n_in - w["ssm_dt_bias"].shape[1] + LANES
    return dict(
        wg1=w["ffn1_wg"].astype(BF16), wu1=w["ffn1_wu"].astype(BF16), wd1=w["ffn1_wd"].astype(BF16),
        w_in=jnp.pad(w["w_in"], ((0, 0), (0, 0), (0, n_pad - n_in))).astype(BF16), w_out=w["w_out"].astype(BF16),
        wg2=w["ffn2_wg"].astype(BF16), wu2=w["ffn2_wu"].astype(BF16), wd2=w["ffn2_wd"].astype(BF16))


def _layer_weights(l, w, stacked):
    c_conv = w["conv_dw_w"].shape[2]
    a_w = (w["w_out"].shape[1] - c_conv - w["ssm_norm_g"].shape[1])
    d_inner = w["ssm_norm_g"].shape[1]
    cd = w["ssm_conv_w"].shape[2]
    n_ssm_heads = w["ssm_dt_bias"].shape[1]
    off_q = 2 * c_conv
    off_k = off_q + a_w
    off_v = off_k + a_w
    off_z = off_v + a_w
    off_xbc = off_z + d_inner
    off_dt = off_xbc + cd
    n_in = off_dt + n_ssm_heads
    assert w["w_in"].shape[2] == n_in
    n_pad = off_dt + LANES
    row = lambda a: a[l].reshape(1, -1).astype(F32)
    pad_row = lambda a: jnp.pad(a[l].astype(F32), (0, LANES - a.shape[1])).reshape(1, LANES)
    assert stacked["w_in"].shape[2] == n_pad
    return dict(
        stacked, layer=l,
        offs=(c_conv, off_q, off_k, off_v, off_z, off_xbc, off_dt, n_pad),
        n1=row(w["ffn1_norm"]), n_mix=row(w["mix_norm"]),
        qg=jnp.tile(w["q_norm_g"][l], a_w // DH).reshape(1, a_w),
        kg=jnp.tile(w["k_norm_g"][l], a_w // DH).reshape(1, a_w),
        conv_w=w["conv_dw_w"][l], conv_b=row(w["conv_dw_b"]), ln_g=row(w["conv_ln_g"]), ln_b=row(w["conv_ln_b"]),
        ssm_cw=w["ssm_conv_w"][l], ssm_cb=row(w["ssm_conv_b"]),
        dt_bias=pad_row(w["ssm_dt_bias"]), a_log=pad_row(w["ssm_a_log"]),
        d_vec=jnp.repeat(w["ssm_d"][l], SSM_P).reshape(1, d_inner), ssm_ng=row(w["ssm_norm_g"]),
        n2=row(w["ffn2_norm"]), n_fin=row(w["final_norm"]),
        n_attn_heads=a_w // DH, n_ssm_heads=n_ssm_heads,
    )


def _row_tile(rows, want=512):
    tm = min(want, rows)
    assert rows % tm == 0 and tm % SUBLANES == 0
    return tm


def _layer_seq(x, lw, rel_bias):
    b, t, d = x.shape
    n_ah, n_sh = lw["n_attn_heads"], lw["n_ssm_heads"]
    tm = _row_tile(b * t)
    assert t % tm == 0 and tm % MOBA_BLOCK == 0
    x1, u, q, k, v, z, xbc, dt, kb, vt, km = _ffn_proj(x.reshape(b * t, d), lw, tm=tm, for_seq_attn=True)
    seq = lambda a: a.reshape(b, t, a.shape[-1])
    u, xbc = seq(u), seq(xbc)
    conv_out = _conv_seq(u, lw)
    attn_out = _attn_seq(seq(q), seq(kb), vt, km.reshape(b, t // MOBA_BLOCK, -1), rel_bias, n_heads=n_ah)
    ssm_out, h_fin = _ssd_seq(xbc, seq(z), seq(dt), lw, n_heads=n_sh)
    flat = lambda a: a.reshape(b * t, a.shape[-1])
    y = _out_ffn(x1, flat(conv_out), flat(attn_out), flat(ssm_out), lw, tm=tm)
    conv_taps = lw["conv_w"].shape[0]
    ssm_taps = lw["ssm_cw"].shape[0]
    states = (k.reshape(b, t, n_ah, DH), v.reshape(b, t, n_ah, DH), u[:, t - (conv_taps - 1):],
              xbc[:, t - (ssm_taps - 1):], h_fin.reshape(b, n_sh, SSM_P, SSM_N))
    return y.reshape(b, t, d), states


def _conv_step_kernel(u_ref, st_ref, w_ref, b_ref, g_ref, lb_ref, o_ref):
    past = st_ref.shape[0]
    acc = w_ref[past:past + 1, :] * u_ref[...]
    for k in range(past):
        acc = acc + w_ref[k:k + 1, :] * st_ref[k]
    o_ref[...] = _silu(_layernorm(acc + b_ref[...], g_ref[...], lb_ref[...])).astype(o_ref.dtype)


def _conv_step(u, state_t, lw):
    return pl.pallas_call(
        _conv_step_kernel,
        out_shape=jax.ShapeDtypeStruct(u.shape, BF16),
        compiler_params=pltpu.CompilerParams(vmem_limit_bytes=VMEM_LIMIT_BYTES),
        name="conv_step",
    )(u, state_t, lw["conv_w"], lw["conv_b"], lw["ln_g"], lw["ln_b"])


SSD_STEP_ROWS = 8


def _ssd_step_kernel(xn_ref, st_ref, z_ref, dt_ref, h0_ref, cw_ref, cb_ref, dtb_ref, alog_ref, dvec_ref, ng_ref,
                     y_ref, h1_ref, *, n_heads):
    rows = xn_ref.shape[0]
    n = SSM_N
    d_inner = n_heads * SSM_P
    group_rows = (n_heads // SSM_G) * SSM_P
    past = st_ref.shape[0]
    acc = cw_ref[past:past + 1, :] * xn_ref[...]
    for k in range(past):
        acc = acc + cw_ref[k:k + 1, :] * st_ref[k]
    xc = _silu(acc + cb_ref[...])
    xs = xc[:, :d_inner]
    bm = xc[:, d_inner:d_inner + SSM_G * n]
    cm = xc[:, d_inner + SSM_G * n:]
    dt = _softplus(dt_ref[...] + dtb_ref[...])
    dta = dt * (-jnp.exp(alog_ref[...]))
    er = lax.broadcasted_iota(jnp.int32, (LANES, d_inner), 0)
    ec = lax.broadcasted_iota(jnp.int32, (LANES, d_inner), 1) // SSM_P
    expand = jnp.where(er == ec, 1.0, 0.0).astype(BF16)
    xdt = xs * _dot_exact_rhs(dt, expand)
    dec = jnp.exp(_dot_exact_rhs(dta, expand))
    pad = jnp.zeros((LANES - rows, d_inner), F32)
    xdt_t = jnp.concatenate([xdt, pad], axis=0).T
    dec_t = jnp.concatenate([dec, pad], axis=0).T
    lane = lax.broadcasted_iota(jnp.int32, (d_inner, LANES), 1)
    y_t = jnp.zeros((d_inner, LANES), F32)
    for b in range(rows):
        xcol = xdt_t[:, b:b + 1]
        dcol = dec_t[:, b:b + 1]
        ycols = []
        for g in range(SSM_G):
            rs = slice(g * group_rows, (g + 1) * group_rows)
            h0 = h0_ref[b, rs, :]
            brow = bm[b:b + 1, g * n:(g + 1) * n]
            crow = cm[b:b + 1, g * n:(g + 1) * n]
            cb = jnp.sum(crow * brow, axis=1, keepdims=True)
            h1_ref[b, rs, :] = h0 * dcol[rs] + xcol[rs] * brow
            ycols.append(jnp.sum(h0 * crow, axis=1, keepdims=True) * dcol[rs] + cb * xcol[rs])
        y_t = jnp.where(lane == b, jnp.concatenate(ycols, axis=0), y_t)
    y = y_t.T[:rows] + dvec_ref[...] * xs
    gated = y * _silu(z_ref[...])
    y_ref[...] = _rmsnorm(gated, ng_ref[...]).astype(y_ref.dtype)


def _ssd_step(xn, state_t, z, dt, h0, lw, *, n_heads):
    db, cd = xn.shape
    d_inner = z.shape[1]
    rows = SSD_STEP_ROWS
    assert db % rows == 0
    past = state_t.shape[0]
    row = lambda w: pl.BlockSpec((rows, w), lambda i: (i, 0))
    return pl.pallas_call(
        functools.partial(_ssd_step_kernel, n_heads=n_heads),
        grid=(db // rows,),
        in_specs=[row(cd), pl.BlockSpec((past, rows, cd), lambda i: (0, i, 0)), row(d_inner), row(LANES),
                  pl.BlockSpec((rows, d_inner, SSM_N), lambda i: (i, 0, 0)),
                  _const_spec((past + 1, cd)), _const_spec((1, cd)), _const_spec((1, LANES)),
                  _const_spec((1, LANES)), _const_spec((1, d_inner)), _const_spec((1, d_inner))],
        out_specs=[row(d_inner), pl.BlockSpec((rows, d_inner, SSM_N), lambda i: (i, 0, 0))],
        out_shape=[jax.ShapeDtypeStruct((db, d_inner), BF16), jax.ShapeDtypeStruct(h0.shape, F32)],
        compiler_params=_params("arbitrary"),
        name="ssd_step",
    )(xn, state_t, z, dt, h0, lw["ssm_cw"], lw["ssm_cb"], lw["dt_bias"], lw["a_log"], lw["d_vec"], lw["ssm_ng"])


GATE_PAGES = 32


def _gate_pages_kernel(pt_ref, q_ref, *refs, n_blocks):
    del pt_ref
    page_refs, idx_ref, kmean_ref = refs[:-2], refs[-2], refs[-1]
    per_block = MOBA_BLOCK // PAGE_SIZE
    blocks = len(page_refs) // per_block
    chunk = pl.program_id(1)
    n_heads = kmean_ref.shape[0]

    @pl.when(chunk == 0)
    def _():
        kmean_ref[...] = jnp.zeros(kmean_ref.shape, F32)

    lane3 = lax.broadcasted_iota(jnp.int32, kmean_ref.shape, 2)
    km = kmean_ref[...]
    for blk in range(blocks):
        s = None
        for r in range(per_block):
            page = page_refs[blk * per_block + r][...]
            s = page if s is None else s + page
        col = jnp.sum(s, axis=2, keepdims=True) * (1.0 / MOBA_BLOCK)
        km = jnp.where(lane3 == chunk * blocks + blk, col, km)
    kmean_ref[...] = km

    @pl.when(chunk == pl.num_programs(1) - 1)
    def _():
        lane = lax.broadcasted_iota(jnp.int32, (SUBLANES, LANES), 1)
        for h in range(n_heads):
            q_hi, q_mid, q_lo = _split3(jnp.broadcast_to(q_ref[0, h], (SUBLANES, DH)))
            k_hi, k_mid, k_lo = _split3(kmean_ref[h])
            gate = (_dot(q_hi, k_hi) + _dot(q_hi, k_mid) + _dot(q_mid, k_hi)
                    + _dot(q_hi, k_lo) + _dot(q_lo, k_hi) + _dot(q_mid, k_mid))
            gate = jnp.where(lane < n_blocks, gate, NEG_INF)
            picks = jnp.zeros((SUBLANES, LANES), jnp.int32)
            for rank in range(MOBA_TOPK):
                mx = jnp.max(gate, axis=1, keepdims=True)
                idx = jnp.min(jnp.where(gate == mx, lane, LANES), axis=1, keepdims=True)
                picks = jnp.where(lane == rank, idx, picks)
                gate = jnp.where(lane == idx, NEG_INF, gate)
            idx_ref[0, h] = picks


def _gate_pages(q4, cache_t, layer, page_table_flat, n_pages):
    db, n_heads = q4.shape[:2]
    per_block = MOBA_BLOCK // PAGE_SIZE
    n_blocks = n_pages // per_block
    pg = min(GATE_PAGES, n_pages)
    assert n_pages % pg == 0 and pg % per_block == 0 and MOBA_TOPK <= n_blocks <= LANES

    def page_spec(j):
        return pl.BlockSpec((None, None, n_heads, DH, PAGE_SIZE),
                            lambda b, i, pt: (layer, pt[b * n_pages + i * pg + j], 0, 0, 0))

    return pl.pallas_call(
        functools.partial(_gate_pages_kernel, n_blocks=n_blocks),
        grid_spec=pltpu.PrefetchScalarGridSpec(
            num_scalar_prefetch=1, grid=(db, n_pages // pg),
            in_specs=[pl.BlockSpec((1, n_heads, 1, DH), lambda b, i, pt: (b, 0, 0, 0))]
            + [page_spec(j) for j in range(pg)],
            out_specs=pl.BlockSpec((1, n_heads, SUBLANES, LANES), lambda b, i, pt: (b, 0, 0, 0)),
            scratch_shapes=[pltpu.VMEM((n_heads, DH, LANES), F32)]),
        out_shape=jax.ShapeDtypeStruct((db, n_heads, SUBLANES, LANES), jnp.int32),
        compiler_params=_params("arbitrary", "arbitrary"),
        name="gate_pages",
    )(page_table_flat, q4, *([cache_t] * pg))


def _attn_step_kernel(pt_ref, sel_ref, relb_ref, q_ref, kn_ref, vn_ref, *refs, n_heads, past_len):
    del pt_ref
    per_block = MOBA_BLOCK // PAGE_SIZE
    n_pg = MOBA_TOPK * per_block
    k_pages, v_pages, o_ref = refs[:n_pg], refs[n_pg:2 * n_pg], refs[2 * n_pg]
    b, h = pl.program_id(0), pl.program_id(1)
    qh = q_ref[...] * (1.0 / math.sqrt(DH))
    q8 = jnp.broadcast_to(qh, (SUBLANES, DH)).astype(BF16)
    key_off = lax.broadcasted_iota(jnp.int32, (1, PAGE_SIZE), 1)
    s_own = jnp.sum(qh * kn_ref[...], axis=1, keepdims=True) + relb_ref[0, h]
    logits = []
    for j in range(MOBA_TOPK):
        blk = sel_ref[(b * n_heads + h) * MOBA_TOPK + j]
        for r in range(per_block):
            dist = past_len - (blk * MOBA_BLOCK + r * PAGE_SIZE + key_off)
            kt = k_pages[j * per_block + r][...].astype(BF16)
            logits.append(_dot(q8, kt)[0:1, :] + _rel_bias_tile(relb_ref, h, dist))
    m = s_own
    for s in logits:
        m = jnp.maximum(m, jnp.max(s, axis=1, keepdims=True))
    p_own = jnp.exp(s_own - m)
    l = p_own
    acc = p_own * vn_ref[...]
    for page, s in enumerate(logits):
        p = jnp.exp(s - m)
        l = l + jnp.sum(p, axis=1, keepdims=True)
        vt = v_pages[page][...].astype(BF16)
        acc = acc + _dot_t(jnp.broadcast_to(p, (SUBLANES, PAGE_SIZE)).astype(BF16), vt)[0:1, :]
    o_ref[...] = jnp.broadcast_to(acc / l, (SUBLANES, DH))


def _attn_step(q4, kn4, vn4, cache_kt, cache_vt, layer, page_table_flat, sel_flat, rel_bias, *, n_pages):
    db, n_heads = q4.shape[:2]
    per_block = MOBA_BLOCK // PAGE_SIZE
    past_len = n_pages * PAGE_SIZE
    assert past_len % MOBA_BLOCK == 0 and past_len // MOBA_BLOCK >= MOBA_TOPK

    def page_spec(j, r):
        def index_map(b, h, pt, sel):
            blk = sel[(b * n_heads + h) * MOBA_TOPK + j]
            return (layer, pt[b * n_pages + blk * per_block + r], h, 0, 0)
        return pl.BlockSpec((None, None, None, DH, PAGE_SIZE), index_map)

    page_specs = [page_spec(j, r) for j in range(MOBA_TOPK) for r in range(per_block)]
    tok = pl.BlockSpec((None, None, 1, DH), lambda b, h, pt, sel: (b, h, 0, 0))
    n_pg = len(page_specs)
    return pl.pallas_call(
        functools.partial(_attn_step_kernel, n_heads=n_heads, past_len=past_len),
        grid_spec=pltpu.PrefetchScalarGridSpec(
            num_scalar_prefetch=2, grid=(db, n_heads),
            in_specs=[pl.BlockSpec(memory_space=pltpu.SMEM), tok, tok, tok] + page_specs + page_specs,
            out_specs=pl.BlockSpec((None, None, SUBLANES, DH), lambda b, h, pt, sel: (b, h, 0, 0))),
        out_shape=jax.ShapeDtypeStruct((db, n_heads, SUBLANES, DH), F32),
        compiler_params=_params("arbitrary", "arbitrary"),
        name="attn_step",
    )(page_table_flat, sel_flat, rel_bias, q4, kn4, vn4, *([cache_kt] * n_pg), *([cache_vt] * n_pg))


def _layer_step(x, conv_state, ssm_conv_state, ssm_state, cache_kt, cache_vt, layer, page_table, lw, rel_bias):
    db, t, d = x.shape
    assert t == 1
    n_ah, n_sh = lw["n_attn_heads"], lw["n_ssm_heads"]
    tm = _row_tile(db)
    x1, u, q, k, v, z, xbc, dt = _ffn_proj(x.reshape(db, d), lw, tm=tm, for_seq_attn=False)
    conv_out = _conv_step(u, conv_state.transpose(1, 0, 2), lw)
    n_pages = page_table.shape[1]
    pt_flat = page_table.reshape(-1)
    q4 = q.reshape(db, n_ah, 1, DH)
    picks = _gate_pages(q4, cache_kt, layer, pt_flat, n_pages)
    sel_flat = picks[:, :, 0, :MOBA_TOPK].reshape(-1)
    attn = _attn_step(q4, k.reshape(db, n_ah, 1, DH), v.reshape(db, n_ah, 1, DH), cache_kt, cache_vt, layer,
                      pt_flat, sel_flat, rel_bias, n_pages=n_pages)
    attn_out = attn[:, :, 0, :].reshape(db, n_ah * DH)
    ssm_out, h_new = _ssd_step(xbc, ssm_conv_state.transpose(1, 0, 2), z, dt,
                               ssm_state.reshape(db, n_sh * SSM_P, SSM_N), lw, n_heads=n_sh)
    y = _out_ffn(x1, conv_out, attn_out.astype(BF16), ssm_out, lw, tm=tm)
    states = (k.reshape(db, 1, n_ah, DH), v.reshape(db, 1, n_ah, DH),
              jnp.concatenate([conv_state[:, 1:], u[:, None, :]], axis=1),
              jnp.concatenate([ssm_conv_state[:, 1:], xbc[:, None, :]], axis=1),
              h_new.reshape(ssm_state.shape))
    return y.reshape(db, 1, d), states


def kernel(x_prompt, x_sample, cache_k, cache_v, state_conv, state_ssm_conv, state_ssm, page_table, rel_bias, ffn1_norm, ffn1_wg, ffn1_wu, ffn1_wd, mix_norm, w_in, w_out, conv_dw_w, conv_dw_b, conv_ln_g, conv_ln_b, q_norm_g, k_norm_g, ssm_conv_w, ssm_conv_b, ssm_dt_bias, ssm_a_log, ssm_d, ssm_norm_g, ffn2_norm, ffn2_wg, ffn2_wu, ffn2_wd, final_norm):
    w = dict(ffn1_norm=ffn1_norm, ffn1_wg=ffn1_wg, ffn1_wu=ffn1_wu, ffn1_wd=ffn1_wd, mix_norm=mix_norm,
             w_in=w_in, w_out=w_out, conv_dw_w=conv_dw_w, conv_dw_b=conv_dw_b, conv_ln_g=conv_ln_g,
             conv_ln_b=conv_ln_b, q_norm_g=q_norm_g, k_norm_g=k_norm_g, ssm_conv_w=ssm_conv_w,
             ssm_conv_b=ssm_conv_b, ssm_dt_bias=ssm_dt_bias, ssm_a_log=ssm_a_log, ssm_d=ssm_d,
             ssm_norm_g=ssm_norm_g, ffn2_norm=ffn2_norm, ffn2_wg=ffn2_wg, ffn2_wu=ffn2_wu, ffn2_wd=ffn2_wd,
             final_norm=final_norm)
    depth = w_in.shape[0]
    y_p, y_s = x_prompt, x_sample
    st_p, st_s = [], []
    cache_kt = cache_k.transpose(0, 1, 3, 4, 2)
    cache_vt = cache_v.transpose(0, 1, 3, 4, 2)
    stacked = _stacked_matrices(w)
    for l in range(depth):
        lw = _layer_weights(l, w, stacked)
        y_p, st = _layer_seq(y_p, lw, rel_bias)
        st_p.append(st)
        y_s, st = _layer_step(y_s, state_conv[l], state_ssm_conv[l], state_ssm[l], cache_kt, cache_vt, l,
                              page_table, lw, rel_bias)
        st_s.append(st)
    stack = lambda sts, i: jnp.stack([s[i] for s in sts])
    return (y_p, y_s) + tuple(stack(st_p, i) for i in range(5)) + tuple(stack(st_s, i) for i in range(5))
```

```python
import functools
import math

import jax
import jax.numpy as jnp
from jax import lax
from jax.experimental import pallas as pl
from jax.experimental.pallas import tpu as pltpu

F32 = jnp.float32
BF16 = jnp.bfloat16

DH = 64
MOBA_BLOCK = 256
MOBA_TOPK = 3
NUM_BUCKETS = 32
REL_MAX_DIST = 128
SSM_P = 64
SSM_G = 2
SSM_N = 128
SSD_CHUNK = 256
PAGE_SIZE = 128

LANES = 128
SUBLANES = 8
VMEM_LIMIT_BYTES = 56 * 1024 * 1024
BF16_SUBLANES = 16

VT_ONES = BF16_SUBLANES
VT_HEAD_ROWS = DH + VT_ONES
LOG2E = math.log2(math.e)

NEG_INF = float("-inf")


def _params(*semantics):
    return pltpu.CompilerParams(dimension_semantics=semantics, vmem_limit_bytes=VMEM_LIMIT_BYTES)


def _dot(a, b):
    return jnp.dot(a, b, preferred_element_type=F32)


def _dot_t(a, b):
    return lax.dot_general(a, b, (((1,), (1,)), ((), ())), preferred_element_type=F32)


def _split3(x):
    hi = x.astype(BF16)
    r1 = x - hi.astype(F32)
    mid = r1.astype(BF16)
    lo = (r1 - mid.astype(F32)).astype(BF16)
    return hi, mid, lo


def _dot_exact_rhs(x, sel):
    hi, mid, lo = _split3(x)
    return _dot(hi, sel) + _dot(mid, sel) + _dot(lo, sel)


def _dot_exact_lhs(sel, x):
    hi, mid, lo = _split3(x)
    return _dot(sel, hi) + _dot(sel, mid) + _dot(sel, lo)


def _sigmoid(x):
    return 1.0 / (1.0 + jnp.exp(-x))


def _silu(x):
    return x * _sigmoid(x)


def _rmsnorm(x, g, eps=1e-6):
    return x * lax.rsqrt(jnp.mean(x * x, axis=-1, keepdims=True) + eps) * g


def _const_spec(shape):
    nd = len(shape)
    return pl.BlockSpec(shape, lambda *_: (0,) * nd, pipeline_mode=pl.Buffered(1))


def _layer_spec(stacked, layer):
    return pl.BlockSpec((None,) + stacked.shape[1:], lambda *_: (layer, 0, 0), pipeline_mode=pl.Buffered(1))


def _swiglu(hb, wg_ref, wu_ref, wd_ref, fchunk):
    d_ff = wg_ref.shape[1]
    acc = None
    for c in range(d_ff // fchunk):
        sl = slice(c * fchunk, (c + 1) * fchunk)
        g = _dot(hb, wg_ref[:, sl])
        u = _dot(hb, wu_ref[:, sl])
        a = (_silu(g) * u).astype(BF16)
        part = _dot(a, wd_ref[sl, :])
        acc = part if acc is None else acc + part
    return acc


def _head_sumsq_matrix(width):
    r = lax.broadcasted_iota(jnp.int32, (width, width), 0) // DH
    c = lax.broadcasted_iota(jnp.int32, (width, width), 1) // DH
    return jnp.where(r == c, 1.0, 0.0).astype(BF16)


def _head_rmsnorm(p, g_row, eps=1e-6):
    ss = _dot_exact_rhs(p * p, _head_sumsq_matrix(p.shape[1]))
    return p * lax.rsqrt(ss * (1.0 / DH) + eps) * g_row


def _ffn_proj_kernel(x_ref, n1_ref, wg_ref, wu_ref, wd_ref, n2_ref, win_ref, qg_ref, kg_ref,
                     x1_ref, u_ref, q_ref, k_ref, v_ref, z_ref, xbc_ref, dt_ref,
                     *maybe_attn_refs, fchunk, offs):
    c_conv, off_q, off_k, off_v, off_z, off_xbc, off_dt, n_pad = offs
    x = x_ref[...]
    hb = _rmsnorm(x, n1_ref[...]).astype(BF16)
    x1 = x + 0.5 * _swiglu(hb, wg_ref, wu_ref, wd_ref, fchunk)
    x1_ref[...] = x1
    h2 = _rmsnorm(x1, n2_ref[...]).astype(BF16)
    pg = _dot(h2, win_ref[:, 0:off_q])
    u_ref[...] = pg[:, :c_conv] * _sigmoid(pg[:, c_conv:])
    q_ref[...] = _head_rmsnorm(_dot(h2, win_ref[:, off_q:off_k]), qg_ref[...])
    k = _head_rmsnorm(_dot(h2, win_ref[:, off_k:off_v]), kg_ref[...])
    k_ref[...] = k
    v = _dot(h2, win_ref[:, off_v:off_z])
    v_ref[...] = v
    z_ref[...] = _dot(h2, win_ref[:, off_z:off_xbc])
    xbc_ref[...] = _dot(h2, win_ref[:, off_xbc:off_dt])
    dt_ref[...] = _dot(h2, win_ref[:, off_dt:n_pad])
    if maybe_attn_refs:
        kb_ref, vt_ref, km_ref = maybe_attn_refs
        kb_ref[...] = k.astype(BF16)
        vt = v.T.astype(BF16)
        ones = jnp.ones((VT_ONES, vt.shape[1]), BF16)
        vt_ref[...] = jnp.concatenate(
            [part for h in range(vt.shape[0] // DH) for part in (vt[h * DH:(h + 1) * DH], ones)], axis=0)
        nb = k.shape[0] // MOBA_BLOCK
        km_ref[0] = jnp.mean(k.reshape(nb, MOBA_BLOCK, k.shape[1]), axis=1)


def _ffn_proj(x, lw, *, tm, for_seq_attn):
    r, d = x.shape
    d_ff = lw["wg1"].shape[2]
    layer = lw["layer"]
    offs = lw["offs"]
    c_conv, off_q, off_k, off_v, off_z, off_xbc, off_dt, n_pad = offs
    a_w = off_k - off_q
    widths = dict(u=c_conv, q=a_w, k=a_w, v=a_w, z=off_xbc - off_z, xbc=off_dt - off_xbc, dt=n_pad - off_dt)
    row = lambda w: pl.BlockSpec((tm, w), lambda i: (i, 0))
    out_shape = [jax.ShapeDtypeStruct((r, d), F32),
                 jax.ShapeDtypeStruct((r, widths["u"]), F32),
                 jax.ShapeDtypeStruct((r, a_w), F32),
                 jax.ShapeDtypeStruct((r, a_w), F32),
                 jax.ShapeDtypeStruct((r, a_w), F32),
                 jax.ShapeDtypeStruct((r, widths["z"]), F32),
                 jax.ShapeDtypeStruct((r, widths["xbc"]), F32),
                 jax.ShapeDtypeStruct((r, widths["dt"]), F32)]
    out_specs = [row(d), row(widths["u"]), row(a_w), row(a_w), row(a_w),
                 row(widths["z"]), row(widths["xbc"]), row(widths["dt"])]
    if for_seq_attn:
        nb = tm // MOBA_BLOCK
        vt_rows = (a_w // DH) * VT_HEAD_ROWS
        out_shape += [jax.ShapeDtypeStruct((r, a_w), BF16), jax.ShapeDtypeStruct((vt_rows, r), BF16),
                      jax.ShapeDtypeStruct((r // tm, nb, a_w), F32)]
        out_specs += [row(a_w), pl.BlockSpec((vt_rows, tm), lambda i: (0, i)),
                      pl.BlockSpec((1, nb, a_w), lambda i: (i, 0, 0))]
    fchunk = 256 if d_ff % 256 == 0 else d_ff
    outs = pl.pallas_call(
        functools.partial(_ffn_proj_kernel, fchunk=fchunk, offs=offs),
        grid=(r // tm,),
        in_specs=[row(d), _const_spec((1, d)), _layer_spec(lw["wg1"], layer), _layer_spec(lw["wu1"], layer),
                  _layer_spec(lw["wd1"], layer), _const_spec((1, d)), _layer_spec(lw["w_in"], layer),
                  _const_spec((1, a_w)), _const_spec((1, a_w))],
        out_specs=out_specs,
        out_shape=out_shape,
        compiler_params=_params("arbitrary"),
        name="ffn_proj",
    )(x, lw["n1"], lw["wg1"], lw["wu1"], lw["wd1"], lw["n_mix"], lw["w_in"], lw["qg"], lw["kg"])
    return outs


def _out_ffn_kernel(x1_ref, a_ref, b_ref, c_ref, wo_ref, n_ref, wg_ref, wu_ref, wd_ref, fn_ref, y_ref,
                    *, fchunk):
    wa, wb = a_ref.shape[1], b_ref.shape[1]
    mix = (_dot(a_ref[...], wo_ref[0:wa, :]) + _dot(b_ref[...], wo_ref[wa:wa + wb, :])
           + _dot(c_ref[...], wo_ref[wa + wb:, :]))
    x2 = x1_ref[...] + mix
    hb = _rmsnorm(x2, n_ref[...]).astype(BF16)
    x3 = x2 + 0.5 * _swiglu(hb, wg_ref, wu_ref, wd_ref, fchunk)
    y_ref[...] = _rmsnorm(x3, fn_ref[...])


def _out_ffn(x1, conv_out, attn_out, ssm_out, lw, *, tm):
    r, d = x1.shape
    d_ff = lw["wg2"].shape[2]
    layer = lw["layer"]
    row = lambda w: pl.BlockSpec((tm, w), lambda i: (i, 0))
    fchunk = 256 if d_ff % 256 == 0 else d_ff
    return pl.pallas_call(
        functools.partial(_out_ffn_kernel, fchunk=fchunk),
        grid=(r // tm,),
        in_specs=[row(d), row(conv_out.shape[1]), row(attn_out.shape[1]), row(ssm_out.shape[1]),
                  _layer_spec(lw["w_out"], layer), _const_spec((1, d)), _layer_spec(lw["wg2"], layer),
                  _layer_spec(lw["wu2"], layer), _layer_spec(lw["wd2"], layer), _const_spec((1, d))],
        out_specs=row(d),
        out_shape=jax.ShapeDtypeStruct((r, d), F32),
        compiler_params=_params("arbitrary"),
        name="out_ffn",
    )(x1, conv_out, attn_out, ssm_out, lw["w_out"], lw["n2"], lw["wg2"], lw["wu2"], lw["wd2"], lw["n_fin"])


CONV_HALO = 32
CONV_SUB = 128


def _layernorm(c, g, b, eps=1e-5):
    xc = c - jnp.mean(c, axis=-1, keepdims=True)
    var = jnp.mean(xc * xc, axis=-1, keepdims=True)
    return xc * lax.rsqrt(var + eps) * g + b


def _conv_seq_kernel(u_ref, w_ref, b_ref, g_ref, lb_ref, o_ref, buf_ref, shift_ref, *, taps):
    tc = u_ref.shape[1]

    @pl.when(pl.program_id(1) == 0)
    def _():
        buf_ref[0:CONV_HALO, :] = jnp.zeros((CONV_HALO, buf_ref.shape[1]), F32)

    buf_ref[CONV_HALO:CONV_HALO + tc, :] = u_ref[0]
    first = CONV_HALO - (taps - 1)
    residues = sorted({(first + k) % SUBLANES for k in range(taps)})
    for r in residues:
        rows = max(first + k - r for k in range(taps) if (first + k) % SUBLANES == r) + tc
        shift_ref[r, 0:rows, :] = buf_ref[r:r + rows, :]
    for s in range(tc // CONV_SUB):
        acc = None
        for k in range(taps):
            r = (first + k) % SUBLANES
            off = s * CONV_SUB + first + k - r
            term = w_ref[k:k + 1, :] * shift_ref[r, off:off + CONV_SUB, :]
            acc = term if acc is None else acc + term
        c = acc + b_ref[...]
        o_ref[0, s * CONV_SUB:(s + 1) * CONV_SUB, :] = _silu(_layernorm(c, g_ref[...], lb_ref[...])).astype(o_ref.dtype)
    buf_ref[0:CONV_HALO, :] = buf_ref[tc:tc + CONV_HALO, :]


def _conv_seq(u, lw, *, tc=256):
    b, t, c = u.shape
    taps = lw["conv_w"].shape[0]
    assert taps - 1 <= CONV_HALO and t % tc == 0 and tc % CONV_SUB == 0
    return pl.pallas_call(
        functools.partial(_conv_seq_kernel, taps=taps),
        grid=(b, t // tc),
        in_specs=[pl.BlockSpec((1, tc, c), lambda i, j: (i, j, 0)),
                  _const_spec((taps, c)), _const_spec((1, c)), _const_spec((1, c)), _const_spec((1, c))],
        out_specs=pl.BlockSpec((1, tc, c), lambda i, j: (i, j, 0)),
        out_shape=jax.ShapeDtypeStruct((b, t, c), BF16),
        scratch_shapes=[pltpu.VMEM((CONV_HALO + tc, c), F32), pltpu.VMEM((SUBLANES, CONV_HALO + tc, c), F32)],
        compiler_params=_params("arbitrary", "arbitrary"),
        name="conv_seq",
    )(u, lw["conv_w"], lw["conv_b"], lw["ln_g"], lw["ln_b"])


def _rel_bias_tile(relb_ref, head, dist):
    max_exact = NUM_BUCKETS // 2
    n = jnp.maximum(dist, 0)
    nf = jnp.maximum(n, 1).astype(F32)
    large = max_exact + (jnp.log(nf / max_exact) / math.log(REL_MAX_DIST / max_exact)
                         * (NUM_BUCKETS - max_exact)).astype(jnp.int32)
    large = jnp.minimum(large, NUM_BUCKETS - 1)
    bucket = jnp.where(n < max_exact, n, large)
    out = jnp.zeros(dist.shape, F32)
    for b in range(NUM_BUCKETS):
        out = jnp.where(bucket == b, relb_ref[b, head], out)
    return out


def _moba_select_t(gate_t, n_eligible):
    nb = gate_t.shape[0]
    row = lax.broadcasted_iota(jnp.int32, gate_t.shape, 0).astype(F32)
    eligible = row < n_eligible.astype(F32)
    gate_t = jnp.where(eligible, gate_t, NEG_INF)
    sel = jnp.full(gate_t.shape, NEG_INF, F32)
    for _ in range(MOBA_TOPK):
        mx = jnp.max(gate_t, axis=0, keepdims=True)
        idx = jnp.min(jnp.where(gate_t == mx, row, float(nb)), axis=0, keepdims=True)
        pick = row == idx
        sel = jnp.where(pick, jnp.where(eligible, 0.0, sel), sel)
        gate_t = jnp.where(pick, NEG_INF, gate_t)
    return sel


def _attn_seq_kernel(relb_ref, q_ref, km_ref, k_ref, vt_ref, o_ref,
                     bias_own_ref, bias_prev_ref, kmbd_ref, selt_ref, acc_ref, s0_ref, s1_ref, *, n_heads, n_blocks):
    bs = MOBA_BLOCK
    width = n_heads * DH
    cur = pl.program_id(1)

    @pl.when((pl.program_id(0) == 0) & (cur == 0))
    def _():
        ki = lax.broadcasted_iota(jnp.int32, (bs, bs), 0)
        qi = lax.broadcasted_iota(jnp.int32, (bs, bs), 1)
        for h in range(n_heads):
            own = _rel_bias_tile(relb_ref, h, qi - ki) * LOG2E
            bias_own_ref[h] = jnp.where(qi >= ki, own, NEG_INF)
            bias_prev_ref[h] = _rel_bias_tile(relb_ref, h, qi - ki + bs) * LOG2E

    @pl.when(cur == 0)
    def _():
        kmbd_ref[...] = jnp.zeros(kmbd_ref.shape, F32)
        km = km_ref[0]
        km_lane = lax.broadcasted_iota(jnp.int32, km.shape, 1)
        for h in range(n_heads):
            kmbd_ref[h * n_blocks:(h + 1) * n_blocks, :] = jnp.where(
                (km_lane >= h * DH) & (km_lane < (h + 1) * DH), km, 0.0)

    qt = q_ref[0].T
    q_hi, q_mid, q_lo = _split3(qt)
    km_hi, km_mid, km_lo = _split3(kmbd_ref[...])
    gate_t = (_dot(km_hi, q_hi) + _dot(km_mid, q_hi) + _dot(km_hi, q_mid)
              + _dot(km_lo, q_hi) + _dot(km_hi, q_lo) + _dot(km_mid, q_mid))
    for h in range(n_heads):
        rows = slice(h * n_blocks, (h + 1) * n_blocks)
        selt_ref[rows, :] = _moba_select_t(gate_t[rows], cur)
    qs = qt * (LOG2E / math.sqrt(DH))
    row_w = lax.broadcasted_iota(jnp.int32, (width, bs), 0)
    qht = [jnp.where((row_w >= h * DH) & (row_w < (h + 1) * DH), qs, 0.0).astype(BF16) for h in range(n_heads)]
    far_bias = [relb_ref[NUM_BUCKETS - 1, h] * LOG2E for h in range(n_heads)]

    def scores(blk):
        kb = k_ref[0, pl.ds(pl.multiple_of(blk * bs, bs), bs), :]
        return [_dot(kb, qht[h]) for h in range(n_heads)]

    def softmax_pv(blk, s_all, ms, bias):
        start = pl.multiple_of(blk * bs, bs)
        out = []
        for h in range(n_heads):
            m = ms[h]
            s = s_all[h]()
            bh = bias(h)
            if bh.shape[0] == 1:
                m_new = jnp.maximum(m, jnp.max(s, axis=0, keepdims=True) + bh)
                p = jnp.exp2(s + (bh - m_new))
            else:
                s = s + bh
                m_new = jnp.maximum(m, jnp.max(s, axis=0, keepdims=True))
                p = jnp.exp2(s - m_new)
            alpha = jnp.exp2(m - m_new)
            vth = vt_ref[h * VT_HEAD_ROWS:(h + 1) * VT_HEAD_ROWS, pl.ds(start, bs)]
            acc_ref[h] = alpha * acc_ref[h] + _dot(vth, p.astype(BF16))
            out.append(m_new)
        return tuple(out)

    def sel_row(h, j):
        return selt_ref[pl.ds(h * n_blocks + j, 1), :]

    n_far = jnp.maximum(cur - 1, 0)
    last_far = jnp.maximum(n_far - 1, 0)
    prev = jnp.maximum(cur - 1, 0)
    s_own, s_prev = scores(cur), scores(prev)

    acc_ref[...] = jnp.zeros(acc_ref.shape, F32)
    ms = (jnp.full((1, bs), NEG_INF, F32),) * n_heads
    ms = softmax_pv(cur, [lambda s=s: s for s in s_own], ms, lambda h: bias_own_ref[h])
    ms = softmax_pv(prev, [lambda s=s: s for s in s_prev], ms,
                    lambda h: bias_prev_ref[h] + jnp.where(cur >= 1, sel_row(h, prev), NEG_INF))

    def put_scores(s_ref, blk):
        for h, s in enumerate(scores(jnp.minimum(blk, last_far))):
            s_ref[h] = s

    def far_bias_row(h, blk):
        row = sel_row(h, jnp.minimum(blk, last_far)) + far_bias[h]
        return jnp.where(blk < n_far, row, NEG_INF)

    def far_pair(i, ms):
        b0, b1 = 2 * i, 2 * i + 1
        put_scores(s1_ref, b1)
        ms = softmax_pv(b0, [lambda h=h: s0_ref[h] for h in range(n_heads)], ms, lambda h: far_bias_row(h, b0))
        put_scores(s0_ref, b1 + 1)
        b1c = jnp.minimum(b1, last_far)
        return softmax_pv(b1c, [lambda h=h: s1_ref[h] for h in range(n_heads)], ms, lambda h: far_bias_row(h, b1))

    put_scores(s0_ref, 0)
    lax.fori_loop(0, (n_far + 1) // 2, far_pair, ms)
    out_t = jnp.concatenate([acc_ref[h, :DH] * (1.0 / acc_ref[h, DH:DH + 1]) for h in range(n_heads)], axis=0)
    o_ref[0] = out_t.T.astype(o_ref.dtype)


def _attn_seq(q, kb, vt, kmean, rel_bias, *, n_heads):
    b, t, w = q.shape
    nb = t // MOBA_BLOCK
    assert vt.shape == (n_heads * VT_HEAD_ROWS, b * t)
    assert t % MOBA_BLOCK == 0 and nb * n_heads <= LANES and nb % SUBLANES == 0
    return pl.pallas_call(
        functools.partial(_attn_seq_kernel, n_heads=n_heads, n_blocks=nb),
        grid=(b, nb),
        in_specs=[pl.BlockSpec(memory_space=pltpu.SMEM),
                  pl.BlockSpec((1, MOBA_BLOCK, w), lambda i, j: (i, j, 0)),
                  pl.BlockSpec((1, nb, w), lambda i, j: (i, 0, 0)),
                  pl.BlockSpec((1, t, w), lambda i, j: (i, 0, 0)),
                  pl.BlockSpec((n_heads * VT_HEAD_ROWS, t), lambda i, j: (0, i))],
        out_specs=pl.BlockSpec((1, MOBA_BLOCK, w), lambda i, j: (i, j, 0)),
        out_shape=jax.ShapeDtypeStruct((b, t, w), BF16),
        scratch_shapes=[pltpu.VMEM((n_heads, MOBA_BLOCK, MOBA_BLOCK), F32),
                        pltpu.VMEM((n_heads, MOBA_BLOCK, MOBA_BLOCK), F32),
                        pltpu.VMEM((LANES, w), F32),
                        pltpu.VMEM((LANES, MOBA_BLOCK), F32),
                        pltpu.VMEM((n_heads, VT_HEAD_ROWS, MOBA_BLOCK), F32),
                        pltpu.VMEM((n_heads, MOBA_BLOCK, MOBA_BLOCK), F32),
                        pltpu.VMEM((n_heads, MOBA_BLOCK, MOBA_BLOCK), F32)],
        compiler_params=_params("arbitrary", "arbitrary"),
        name="attn_seq",
    )(rel_bias, q, kmean, kb, vt)


SSM_HALO = 8


def _softplus(x):
    return jnp.maximum(x, 0.0) + jnp.log1p(jnp.exp(-jnp.abs(x)))


def _ssd_seq_kernel(xbc_ref, z_ref, dt_ref, cw_ref, cb_ref, dtb_ref, alog_ref, dvec_ref, ng_ref,
                    y_ref, hfin_ref, buf_ref, state_ref, *, n_heads):
    lc = SSD_CHUNK
    n = SSM_N
    d_inner = n_heads * SSM_P
    pair_w = 2 * SSM_P
    heads_per_group = n_heads // SSM_G
    chunk = pl.program_id(1)

    @pl.when(chunk == 0)
    def _():
        buf_ref[0:SSM_HALO, :] = jnp.zeros((SSM_HALO, buf_ref.shape[1]), F32)
        state_ref[...] = jnp.zeros(state_ref.shape, F32)

    buf_ref[SSM_HALO:SSM_HALO + lc, :] = xbc_ref[0]
    taps = cw_ref.shape[0]
    first = SSM_HALO - (taps - 1)
    acc = None
    for k in range(taps):
        term = cw_ref[k:k + 1, :] * buf_ref[first + k:first + k + lc, :]
        acc = term if acc is None else acc + term
    xc = _silu(acc + cb_ref[...])
    buf_ref[0:SSM_HALO, :] = buf_ref[lc:lc + SSM_HALO, :]
    xs = xc[:, :d_inner]
    bm = xc[:, d_inner:d_inner + SSM_G * n]
    cm = xc[:, d_inner + SSM_G * n:]

    dt = _softplus(dt_ref[0] + dtb_ref[...])
    head_lane = lax.broadcasted_iota(jnp.int32, (lc, LANES), 1) < n_heads
    dta = jnp.where(head_lane, dt * (-jnp.exp(alog_ref[...])), 0.0)
    ri = lax.broadcasted_iota(jnp.int32, (lc, lc), 0)
    ci = lax.broadcasted_iota(jnp.int32, (lc, lc), 1)
    causal = ri >= ci
    acum = _dot_exact_lhs(jnp.where(causal, 1.0, 0.0).astype(BF16), dta)
    acum2 = acum * LOG2E
    acum2_t = acum2.T
    er = lax.broadcasted_iota(jnp.int32, (LANES, d_inner), 0)
    ec = lax.broadcasted_iota(jnp.int32, (LANES, d_inner), 1) // SSM_P
    expand = jnp.where(er == ec, 1.0, 0.0).astype(BF16)
    last = acum[lc - 1:lc, :]
    dt_x = _dot_exact_rhs(dt, expand)
    grow_x = _dot_exact_rhs(jnp.exp(acum), expand)
    toend_x = _dot_exact_rhs(jnp.exp(last - acum), expand)
    xdt = xs * dt_x
    xdt_b = xdt.astype(BF16)
    xte = xdt * toend_x

    cg_b, bg_b, cb = [], [], []
    for g in range(SSM_G):
        cg_b.append(cm[:, g * n:(g + 1) * n].astype(BF16))
        bg_b.append(bm[:, g * n:(g + 1) * n].astype(BF16))
        cb.append(_dot_t(cg_b[g], bg_b[g]))

    lane_p = lax.broadcasted_iota(jnp.int32, (lc, pair_w), 1)
    row_p = lax.broadcasted_iota(jnp.int32, (pair_w, n), 0)
    y_tiles = []
    for pair in range(n_heads // 2):
        sl = slice(pair * pair_w, (pair + 1) * pair_w)
        g = (2 * pair) // heads_per_group
        x_pair = xdt_b[:, sl]
        st_old = state_ref[sl, :]
        y_inter = _dot_t(cg_b[g], st_old.astype(BF16)) * grow_x[:, sl]
        decays = [jnp.exp(last[:, 2 * pair + half:2 * pair + half + 1]) for half in range(2)]
        state_ref[sl, :] = (st_old * jnp.where(row_p < SSM_P, decays[0], decays[1])
                            + _dot(xte[:, sl].T.astype(BF16), bg_b[g]))
        y_intra = None
        for half in range(2):
            h = 2 * pair + half
            col = acum2[:, h:h + 1]
            row = acum2_t[h:h + 1, :]
            scores = jnp.where(causal, cb[g] * jnp.exp2(col - row), 0.0).astype(BF16)
            in_half = (lane_p >= half * SSM_P) & (lane_p < (half + 1) * SSM_P)
            part = _dot(scores, jnp.where(in_half, x_pair, jnp.zeros_like(x_pair)))
            y_intra = part if y_intra is None else y_intra + part
        y_tiles.append(y_intra + y_inter)
    y = jnp.concatenate(y_tiles, axis=1) + dvec_ref[...] * xs
    gated = y * _silu(z_ref[0])
    y_ref[0] = _rmsnorm(gated, ng_ref[...]).astype(y_ref.dtype)

    @pl.when(chunk == pl.num_programs(1) - 1)
    def _():
        hfin_ref[0] = state_ref[...]


def _ssd_seq(xbc, z, dt, lw, *, n_heads):
    b, t, cd = xbc.shape
    d_inner = z.shape[2]
    lc = SSD_CHUNK
    assert t % lc == 0 and n_heads % (2 * SSM_G) == 0 and n_heads <= LANES
    taps = lw["ssm_cw"].shape[0]
    assert taps - 1 <= SSM_HALO
    tile = lambda w: pl.BlockSpec((1, lc, w), lambda i, j: (i, j, 0))
    return pl.pallas_call(
        functools.partial(_ssd_seq_kernel, n_heads=n_heads),
        grid=(b, t // lc),
        in_specs=[tile(cd), tile(d_inner), tile(LANES),
                  _const_spec((taps, cd)), _const_spec((1, cd)), _const_spec((1, LANES)),
                  _const_spec((1, LANES)), _const_spec((1, d_inner)), _const_spec((1, d_inner))],
        out_specs=[tile(d_inner), pl.BlockSpec((1, d_inner, SSM_N), lambda i, j: (i, 0, 0))],
        out_shape=[jax.ShapeDtypeStruct((b, t, d_inner), BF16),
                   jax.ShapeDtypeStruct((b, d_inner, SSM_N), F32)],
        scratch_shapes=[pltpu.VMEM((SSM_HALO + lc, cd), F32), pltpu.VMEM((d_inner, SSM_N), F32)],
        compiler_params=_params("arbitrary", "arbitrary"),
        name="ssd_seq",
    )(xbc, z, dt, lw["ssm_cw"], lw["ssm_cb"], lw["dt_bias"], lw["a_log"], lw["d_vec"], lw["ssm_ng"])


def _stacked_matrices(w):
    n_in = w["w_in"].shape[2]
    n_pad = n_in - w["ssm_dt_bias"].shape[1] + LANES
    return dict(
        wg1=w["ffn1_wg"].astype(BF16), wu1=w["ffn1_wu"].astype(BF16), wd1=w["ffn1_wd"].astype(BF16),
        w_in=jnp.pad(w["w_in"], ((0, 0), (0, 0), (0, n_pad - n_in))).astype(BF16), w_out=w["w_out"].astype(BF16),
        wg2=w["ffn2_wg"].astype(BF16), wu2=w["ffn2_wu"].astype(BF16), wd2=w["ffn2_wd"].astype(BF16))


def _layer_weights(l, w, stacked):
    c_conv = w["conv_dw_w"].shape[2]
    a_w = (w["w_out"].shape[1] - c_conv - w["ssm_norm_g"].shape[1])
    d_inner = w["ssm_norm_g"].shape[1]
    cd = w["ssm_conv_w"].shape[2]
    n_ssm_heads = w["ssm_dt_bias"].shape[1]
    off_q = 2 * c_conv
    off_k = off_q + a_w
    off_v = off_k + a_w
    off_z = off_v + a_w
    off_xbc = off_z + d_inner
    off_dt = off_xbc + cd
    n_in = off_dt + n_ssm_heads
    assert w["w_in"].shape[2] == n_in
    n_pad = off_dt + LANES
    row = lambda a: a[l].reshape(1, -1).astype(F32)
    pad_row = lambda a: jnp.pad(a[l].astype(F32), (0, LANES - a.shape[1])).reshape(1, LANES)
    assert stacked["w_in"].shape[2] == n_pad
    return dict(
        stacked, layer=l,
        offs=(c_conv, off_q, off_k, off_v, off_z, off_xbc, off_dt, n_pad),
        n1=row(w["ffn1_norm"]), n_mix=row(w["mix_norm"]),
        qg=jnp.tile(w["q_norm_g"][l], a_w // DH).reshape(1, a_w),
        kg=jnp.tile(w["k_norm_g"][l], a_w // DH).reshape(1, a_w),
        conv_w=w["conv_dw_w"][l], conv_b=row(w["conv_dw_b"]), ln_g=row(w["conv_ln_g"]), ln_b=row(w["conv_ln_b"]),
        ssm_cw=w["ssm_conv_w"][l], ssm_cb=row(w["ssm_conv_b"]),
        dt_bias=pad_row(w["ssm_dt_bias"]), a_log=pad_row(w["ssm_a_log"]),
        d_vec=jnp.repeat(w["ssm_d"][l], SSM_P).reshape(1, d_inner), ssm_ng=row(w["ssm_norm_g"]),
        n2=row(w["ffn2_norm"]), n_fin=row(w["final_norm"]),
        n_attn_heads=a_w // DH, n_ssm_heads=n_ssm_heads,
    )


def _row_tile(rows, want=512):
    tm = min(want, rows)
    assert rows % tm == 0 and tm % SUBLANES == 0
    return tm


def _layer_seq(x, lw, rel_bias):
    b, t, d = x.shape
    n_ah, n_sh = lw["n_attn_heads"], lw["n_ssm_heads"]
    tm = _row_tile(b * t)
    assert t % tm == 0 and tm % MOBA_BLOCK == 0
    x1, u, q, k, v, z, xbc, dt, kb, vt, km = _ffn_proj(x.reshape(b * t, d), lw, tm=tm, for_seq_attn=True)
    seq = lambda a: a.reshape(b, t, a.shape[-1])
    u, xbc = seq(u), seq(xbc)
    conv_out = _conv_seq(u, lw)
    attn_out = _attn_seq(seq(q), seq(kb), vt, km.reshape(b, t // MOBA_BLOCK, -1), rel_bias, n_heads=n_ah)
    ssm_out, h_fin = _ssd_seq(xbc, seq(z), seq(dt), lw, n_heads=n_sh)
    flat = lambda a: a.reshape(b * t, a.shape[-1])
    y = _out_ffn(x1, flat(conv_out), flat(attn_out), flat(ssm_out), lw, tm=tm)
    conv_taps = lw["conv_w"].shape[0]
    ssm_taps = lw["ssm_cw"].shape[0]
    states = (k.reshape(b, t, n_ah, DH), v.reshape(b, t, n_ah, DH), u[:, t - (conv_taps - 1):],
              xbc[:, t - (ssm_taps - 1):], h_fin.reshape(b, n_sh, SSM_P, SSM_N))
    return y.reshape(b, t, d), states


def _conv_step_kernel(u_ref, st_ref, w_ref, b_ref, g_ref, lb_ref, o_ref):
    past = st_ref.shape[0]
    acc = w_ref[past:past + 1, :] * u_ref[...]
    for k in range(past):
        acc = acc + w_ref[k:k + 1, :] * st_ref[k]
    o_ref[...] = _silu(_layernorm(acc + b_ref[...], g_ref[...], lb_ref[...])).astype(o_ref.dtype)


def _conv_step(u, state_t, lw):
    return pl.pallas_call(
        _conv_step_kernel,
        out_shape=jax.ShapeDtypeStruct(u.shape, BF16),
        compiler_params=pltpu.CompilerParams(vmem_limit_bytes=VMEM_LIMIT_BYTES),
        name="conv_step",
    )(u, state_t, lw["conv_w"], lw["conv_b"], lw["ln_g"], lw["ln_b"])


SSD_STEP_ROWS = 8


def _ssd_step_kernel(xn_ref, st_ref, z_ref, dt_ref, h0_ref, cw_ref, cb_ref, dtb_ref, alog_ref, dvec_ref, ng_ref,
                     y_ref, h1_ref, *, n_heads):
    rows = xn_ref.shape[0]
    n = SSM_N
    d_inner = n_heads * SSM_P
    group_rows = (n_heads // SSM_G) * SSM_P
    past = st_ref.shape[0]
    acc = cw_ref[past:past + 1, :] * xn_ref[...]
    for k in range(past):
        acc = acc + cw_ref[k:k + 1, :] * st_ref[k]
    xc = _silu(acc + cb_ref[...])
    xs = xc[:, :d_inner]
    bm = xc[:, d_inner:d_inner + SSM_G * n]
    cm = xc[:, d_inner + SSM_G * n:]
    dt = _softplus(dt_ref[...] + dtb_ref[...])
    dta = dt * (-jnp.exp(alog_ref[...]))
    er = lax.broadcasted_iota(jnp.int32, (LANES, d_inner), 0)
    ec = lax.broadcasted_iota(jnp.int32, (LANES, d_inner), 1) // SSM_P
    expand = jnp.where(er == ec, 1.0, 0.0).astype(BF16)
    xdt = xs * _dot_exact_rhs(dt, expand)
    dec = jnp.exp(_dot_exact_rhs(dta, expand))
    pad = jnp.zeros((LANES - rows, d_inner), F32)
    xdt_t = jnp.concatenate([xdt, pad], axis=0).T
    dec_t = jnp.concatenate([dec, pad], axis=0).T
    lane = lax.broadcasted_iota(jnp.int32, (d_inner, LANES), 1)
    y_t = jnp.zeros((d_inner, LANES), F32)
    for b in range(rows):
        xcol = xdt_t[:, b:b + 1]
        dcol = dec_t[:, b:b + 1]
        ycols = []
        for g in range(SSM_G):
            rs = slice(g * group_rows, (g + 1) * group_rows)
            h0 = h0_ref[b, rs, :]
            brow = bm[b:b + 1, g * n:(g + 1) * n]
            crow = cm[b:b + 1, g * n:(g + 1) * n]
            cb = jnp.sum(crow * brow, axis=1, keepdims=True)
            h1_ref[b, rs, :] = h0 * dcol[rs] + xcol[rs] * brow
            ycols.append(jnp.sum(h0 * crow, axis=1, keepdims=True) * dcol[rs] + cb * xcol[rs])
        y_t = jnp.where(lane == b, jnp.concatenate(ycols, axis=0), y_t)
    y = y_t.T[:rows] + dvec_ref[...] * xs
    gated = y * _silu(z_ref[...])
    y_ref[...] = _rmsnorm(gated, ng_ref[...]).astype(y_ref.dtype)


def _ssd_step(xn, state_t, z, dt, h0_all, layer, lw, *, n_heads):
    db, cd = xn.shape
    d_inner = z.shape[1]
    rows = SSD_STEP_ROWS
    assert db % rows == 0
    past = state_t.shape[0]
    row = lambda w: pl.BlockSpec((rows, w), lambda i: (i, 0))
    return pl.pallas_call(
        functools.partial(_ssd_step_kernel, n_heads=n_heads),
        grid=(db // rows,),
        in_specs=[row(cd), pl.BlockSpec((past, rows, cd), lambda i: (0, i, 0)), row(d_inner), row(LANES),
                  pl.BlockSpec((None, rows, d_inner, SSM_N), lambda i: (layer, i, 0, 0)),
                  _const_spec((past + 1, cd)), _const_spec((1, cd)), _const_spec((1, LANES)),
                  _const_spec((1, LANES)), _const_spec((1, d_inner)), _const_spec((1, d_inner))],
        out_specs=[row(d_inner), pl.BlockSpec((rows, d_inner, SSM_N), lambda i: (i, 0, 0))],
        out_shape=[jax.ShapeDtypeStruct((db, d_inner), BF16), jax.ShapeDtypeStruct(h0_all.shape[1:], F32)],
        compiler_params=_params("arbitrary"),
        name="ssd_step",
    )(xn, state_t, z, dt, h0_all, lw["ssm_cw"], lw["ssm_cb"], lw["dt_bias"], lw["a_log"], lw["d_vec"], lw["ssm_ng"])


GATE_PAGES = 32


def _gate_pages_kernel(pt_ref, q_ref, *refs, n_blocks):
    del pt_ref
    page_refs, idx_ref, kmean_ref = refs[:-2], refs[-2], refs[-1]
    per_block = MOBA_BLOCK // PAGE_SIZE
    blocks = len(page_refs) // per_block
    chunk = pl.program_id(1)
    n_heads = kmean_ref.shape[0]

    @pl.when(chunk == 0)
    def _():
        kmean_ref[...] = jnp.zeros(kmean_ref.shape, F32)

    lane3 = lax.broadcasted_iota(jnp.int32, kmean_ref.shape, 2)
    km = kmean_ref[...]
    for blk in range(blocks):
        s = None
        for r in range(per_block):
            page = page_refs[blk * per_block + r][...]
            s = page if s is None else s + page
        col = jnp.sum(s, axis=2, keepdims=True) * (1.0 / MOBA_BLOCK)
        km = jnp.where(lane3 == chunk * blocks + blk, col, km)
    kmean_ref[...] = km

    @pl.when(chunk == pl.num_programs(1) - 1)
    def _():
        lane = lax.broadcasted_iota(jnp.int32, (SUBLANES, LANES), 1)
        for h in range(n_heads):
            q_hi, q_mid, q_lo = _split3(jnp.broadcast_to(q_ref[0, h], (SUBLANES, DH)))
            k_hi, k_mid, k_lo = _split3(kmean_ref[h])
            gate = (_dot(q_hi, k_hi) + _dot(q_hi, k_mid) + _dot(q_mid, k_hi)
                    + _dot(q_hi, k_lo) + _dot(q_lo, k_hi) + _dot(q_mid, k_mid))
            gate = jnp.where(lane < n_blocks, gate, NEG_INF)
            picks = jnp.zeros((SUBLANES, LANES), jnp.int32)
            for rank in range(MOBA_TOPK):
                mx = jnp.max(gate, axis=1, keepdims=True)
                idx = jnp.min(jnp.where(gate == mx, lane, LANES), axis=1, keepdims=True)
                picks = jnp.where(lane == rank, idx, picks)
                gate = jnp.where(lane == idx, NEG_INF, gate)
            idx_ref[0, h] = picks


def _gate_pages(q4, cache_t, layer, page_table_flat, n_pages):
    db, n_heads = q4.shape[:2]
    per_block = MOBA_BLOCK // PAGE_SIZE
    n_blocks = n_pages // per_block
    pg = min(GATE_PAGES, n_pages)
    assert n_pages % pg == 0 and pg % per_block == 0 and MOBA_TOPK <= n_blocks <= LANES

    def page_spec(j):
        return pl.BlockSpec((None, None, n_heads, DH, PAGE_SIZE),
                            lambda b, i, pt: (layer, pt[b * n_pages + i * pg + j], 0, 0, 0))

    return pl.pallas_call(
        functools.partial(_gate_pages_kernel, n_blocks=n_blocks),
        grid_spec=pltpu.PrefetchScalarGridSpec(
            num_scalar_prefetch=1, grid=(db, n_pages // pg),
            in_specs=[pl.BlockSpec((1, n_heads, 1, DH), lambda b, i, pt: (b, 0, 0, 0))]
            + [page_spec(j) for j in range(pg)],
            out_specs=pl.BlockSpec((1, n_heads, SUBLANES, LANES), lambda b, i, pt: (b, 0, 0, 0)),
            scratch_shapes=[pltpu.VMEM((n_heads, DH, LANES), F32)]),
        out_shape=jax.ShapeDtypeStruct((db, n_heads, SUBLANES, LANES), jnp.int32),
        compiler_params=_params("arbitrary", "arbitrary"),
        name="gate_pages",
    )(page_table_flat, q4, *([cache_t] * pg))


def _attn_step_kernel(pt_ref, sel_ref, relb_ref, q_ref, kn_ref, vn_ref, *refs, n_heads, past_len):
    del pt_ref
    per_block = MOBA_BLOCK // PAGE_SIZE
    n_pg = MOBA_TOPK * per_block
    k_pages, v_pages, o_ref = refs[:n_pg], refs[n_pg:2 * n_pg], refs[2 * n_pg]
    b, h = pl.program_id(0), pl.program_id(1)
    qh = q_ref[...] * (1.0 / math.sqrt(DH))
    q8 = jnp.broadcast_to(qh, (SUBLANES, DH)).astype(BF16)
    key_off = lax.broadcasted_iota(jnp.int32, (1, PAGE_SIZE), 1)
    s_own = jnp.sum(qh * kn_ref[...], axis=1, keepdims=True) + relb_ref[0, h]
    logits = []
    for j in range(MOBA_TOPK):
        blk = sel_ref[(b * n_heads + h) * MOBA_TOPK + j]
        for r in range(per_block):
            dist = past_len - (blk * MOBA_BLOCK + r * PAGE_SIZE + key_off)
            kt = k_pages[j * per_block + r][...].astype(BF16)
            logits.append(_dot(q8, kt)[0:1, :] + _rel_bias_tile(relb_ref, h, dist))
    m = s_own
    for s in logits:
        m = jnp.maximum(m, jnp.max(s, axis=1, keepdims=True))
    p_own = jnp.exp(s_own - m)
    l = p_own
    acc = p_own * vn_ref[...]
    for page, s in enumerate(logits):
        p = jnp.exp(s - m)
        l = l + jnp.sum(p, axis=1, keepdims=True)
        vt = v_pages[page][...].astype(BF16)
        acc = acc + _dot_t(jnp.broadcast_to(p, (SUBLANES, PAGE_SIZE)).astype(BF16), vt)[0:1, :]
    o_ref[...] = jnp.broadcast_to(acc / l, (SUBLANES, DH))


def _attn_step(q4, kn4, vn4, cache_kt, cache_vt, layer, page_table_flat, sel_flat, rel_bias, *, n_pages):
    db, n_heads = q4.shape[:2]
    per_block = MOBA_BLOCK // PAGE_SIZE
    past_len = n_pages * PAGE_SIZE
    assert past_len % MOBA_BLOCK == 0 and past_len // MOBA_BLOCK >= MOBA_TOPK

    def page_spec(j, r):
        def index_map(b, h, pt, sel):
            blk = sel[(b * n_heads + h) * MOBA_TOPK + j]
            return (layer, pt[b * n_pages + blk * per_block + r], h, 0, 0)
        return pl.BlockSpec((None, None, None, DH, PAGE_SIZE), index_map)

    page_specs = [page_spec(j, r) for j in range(MOBA_TOPK) for r in range(per_block)]
    tok = pl.BlockSpec((None, None, 1, DH), lambda b, h, pt, sel: (b, h, 0, 0))
    n_pg = len(page_specs)
    return pl.pallas_call(
        functools.partial(_attn_step_kernel, n_heads=n_heads, past_len=past_len),
        grid_spec=pltpu.PrefetchScalarGridSpec(
            num_scalar_prefetch=2, grid=(db, n_heads),
            in_specs=[pl.BlockSpec(memory_space=pltpu.SMEM), tok, tok, tok] + page_specs + page_specs,
            out_specs=pl.BlockSpec((None, None, SUBLANES, DH), lambda b, h, pt, sel: (b, h, 0, 0))),
        out_shape=jax.ShapeDtypeStruct((db, n_heads, SUBLANES, DH), F32),
        compiler_params=_params("arbitrary", "arbitrary"),
        name="attn_step",
    )(page_table_flat, sel_flat, rel_bias, q4, kn4, vn4, *([cache_kt] * n_pg), *([cache_vt] * n_pg))


def _layer_step(x, conv_state, ssm_conv_state, ssm_states, cache_kt, cache_vt, layer, page_table, lw, rel_bias):
    db, t, d = x.shape
    assert t == 1
    n_ah, n_sh = lw["n_attn_heads"], lw["n_ssm_heads"]
    tm = _row_tile(db)
    x1, u, q, k, v, z, xbc, dt = _ffn_proj(x.reshape(db, d), lw, tm=tm, for_seq_attn=False)
    conv_out = _conv_step(u, conv_state.transpose(1, 0, 2), lw)
    n_pages = page_table.shape[1]
    pt_flat = page_table.reshape(-1)
    q4 = q.reshape(db, n_ah, 1, DH)
    picks = _gate_pages(q4, cache_kt, layer, pt_flat, n_pages)
    sel_flat = picks[:, :, 0, :MOBA_TOPK].reshape(-1)
    attn = _attn_step(q4, k.reshape(db, n_ah, 1, DH), v.reshape(db, n_ah, 1, DH), cache_kt, cache_vt, layer,
                      pt_flat, sel_flat, rel_bias, n_pages=n_pages)
    attn_out = attn[:, :, 0, :].reshape(db, n_ah * DH)
    depth = ssm_states.shape[0]
    ssm_out, h_new = _ssd_step(xbc, ssm_conv_state.transpose(1, 0, 2), z, dt,
                               ssm_states.reshape(depth, db, n_sh * SSM_P, SSM_N), layer, lw, n_heads=n_sh)
    y = _out_ffn(x1, conv_out, attn_out.astype(BF16), ssm_out, lw, tm=tm)
    states = (k.reshape(db, 1, n_ah, DH), v.reshape(db, 1, n_ah, DH),
              jnp.concatenate([conv_state[:, 1:], u[:, None, :]], axis=1),
              jnp.concatenate([ssm_conv_state[:, 1:], xbc[:, None, :]], axis=1),
              h_new.reshape(ssm_states.shape[1:]))
    return y.reshape(db, 1, d), states


def kernel(x_prompt, x_sample, cache_k, cache_v, state_conv, state_ssm_conv, state_ssm, page_table, rel_bias, ffn1_norm, ffn1_wg, ffn1_wu, ffn1_wd, mix_norm, w_in, w_out, conv_dw_w, conv_dw_b, conv_ln_g, conv_ln_b, q_norm_g, k_norm_g, ssm_conv_w, ssm_conv_b, ssm_dt_bias, ssm_a_log, ssm_d, ssm_norm_g, ffn2_norm, ffn2_wg, ffn2_wu, ffn2_wd, final_norm):
    w = dict(ffn1_norm=ffn1_norm, ffn1_wg=ffn1_wg, ffn1_wu=ffn1_wu, ffn1_wd=ffn1_wd, mix_norm=mix_norm,
             w_in=w_in, w_out=w_out, conv_dw_w=conv_dw_w, conv_dw_b=conv_dw_b, conv_ln_g=conv_ln_g,
             conv_ln_b=conv_ln_b, q_norm_g=q_norm_g, k_norm_g=k_norm_g, ssm_conv_w=ssm_conv_w,
             ssm_conv_b=ssm_conv_b, ssm_dt_bias=ssm_dt_bias, ssm_a_log=ssm_a_log, ssm_d=ssm_d,
             ssm_norm_g=ssm_norm_g, ffn2_norm=ffn2_norm, ffn2_wg=ffn2_wg, ffn2_wu=ffn2_wu, ffn2_wd=ffn2_wd,
             final_norm=final_norm)
    depth = w_in.shape[0]
    y_p, y_s = x_prompt, x_sample
    st_p, st_s = [], []
    cache_kt = cache_k.transpose(0, 1, 3, 4, 2)
    cache_vt = cache_v.transpose(0, 1, 3, 4, 2)
    stacked = _stacked_matrices(w)
    for l in range(depth):
        lw = _layer_weights(l, w, stacked)
        y_p, st = _layer_seq(y_p, lw, rel_bias)
        st_p.append(st)
        y_s, st = _layer_step(y_s, state_conv[l], state_ssm_conv[l], state_ssm, cache_kt, cache_vt, l,
                              page_table, lw, rel_bias)
        st_s.append(st)
    stack = lambda sts, i: jnp.stack([s[i] for s in sts])
    return (y_p, y_s) + tuple(stack(st_p, i) for i in range(5)) + tuple(stack(st_s, i) for i in range(5))
```

```python
import functools
import math

import jax
import jax.numpy as jnp
from jax import lax
from jax.experimental import pallas as pl
from jax.experimental.pallas import tpu as pltpu

F32 = jnp.float32
BF16 = jnp.bfloat16

DH = 64
MOBA_BLOCK = 256
MOBA_TOPK = 3
NUM_BUCKETS = 32
REL_MAX_DIST = 128
SSM_P = 64
SSM_G = 2
SSM_N = 128
SSD_CHUNK = 256
PAGE_SIZE = 128

LANES = 128
SUBLANES = 8
VMEM_LIMIT_BYTES = 56 * 1024 * 1024
BF16_SUBLANES = 16

VT_ONES = BF16_SUBLANES
VT_HEAD_ROWS = DH + VT_ONES
LOG2E = math.log2(math.e)

NEG_INF = float("-inf")


def _params(*semantics):
    return pltpu.CompilerParams(dimension_semantics=semantics, vmem_limit_bytes=VMEM_LIMIT_BYTES)


def _dot(a, b):
    return jnp.dot(a, b, preferred_element_type=F32)


def _dot_t(a, b):
    return lax.dot_general(a, b, (((1,), (1,)), ((), ())), preferred_element_type=F32)


def _split3(x):
    hi = x.astype(BF16)
    r1 = x - hi.astype(F32)
    mid = r1.astype(BF16)
    lo = (r1 - mid.astype(F32)).astype(BF16)
    return hi, mid, lo


def _dot_exact_rhs(x, sel):
    hi, mid, lo = _split3(x)
    return _dot(hi, sel) + _dot(mid, sel) + _dot(lo, sel)


def _dot_exact_lhs(sel, x):
    hi, mid, lo = _split3(x)
    return _dot(sel, hi) + _dot(sel, mid) + _dot(sel, lo)


def _sigmoid(x):
    return 1.0 / (1.0 + jnp.exp(-x))


def _silu(x):
    return x * _sigmoid(x)


def _rmsnorm(x, g, eps=1e-6):
    return x * lax.rsqrt(jnp.mean(x * x, axis=-1, keepdims=True) + eps) * g


def _const_spec(shape):
    nd = len(shape)
    return pl.BlockSpec(shape, lambda *_: (0,) * nd, pipeline_mode=pl.Buffered(1))


def _layer_spec(stacked, layer):
    return pl.BlockSpec((None,) + stacked.shape[1:], lambda *_: (layer, 0, 0), pipeline_mode=pl.Buffered(1))


def _swiglu(hb, wg_ref, wu_ref, wd_ref, fchunk):
    d_ff = wg_ref.shape[1]
    acc = None
    for c in range(d_ff // fchunk):
        sl = slice(c * fchunk, (c + 1) * fchunk)
        g = _dot(hb, wg_ref[:, sl])
        u = _dot(hb, wu_ref[:, sl])
        a = (_silu(g) * u).astype(BF16)
        part = _dot(a, wd_ref[sl, :])
        acc = part if acc is None else acc + part
    return acc


def _head_sumsq_matrix(width):
    r = lax.broadcasted_iota(jnp.int32, (width, width), 0) // DH
    c = lax.broadcasted_iota(jnp.int32, (width, width), 1) // DH
    return jnp.where(r == c, 1.0, 0.0).astype(BF16)


def _head_rmsnorm(p, g_row, eps=1e-6):
    ss = _dot_exact_rhs(p * p, _head_sumsq_matrix(p.shape[1]))
    return p * lax.rsqrt(ss * (1.0 / DH) + eps) * g_row


def _ffn_proj_kernel(x_ref, n1_ref, wg_ref, wu_ref, wd_ref, n2_ref, win_ref, qg_ref, kg_ref, *refs,
                     fchunk, offs, for_seq_attn, n_aliased):
    x1_ref, u_ref, q_ref, k_ref, v_ref, z_ref, xbc_ref, dt_ref, *maybe_attn_refs = refs[n_aliased:]
    c_conv, off_q, off_k, off_v, off_z, off_xbc, off_dt, n_pad = offs
    x = x_ref[...]
    hb = _rmsnorm(x, n1_ref[...]).astype(BF16)
    x1 = x + 0.5 * _swiglu(hb, wg_ref, wu_ref, wd_ref, fchunk)
    x1_ref[...] = x1
    h2 = _rmsnorm(x1, n2_ref[...]).astype(BF16)
    pg = _dot(h2, win_ref[:, 0:off_q])
    u_ref[...] = pg[:, :c_conv] * _sigmoid(pg[:, c_conv:])
    q_ref[...] = _head_rmsnorm(_dot(h2, win_ref[:, off_q:off_k]), qg_ref[...])
    k = _head_rmsnorm(_dot(h2, win_ref[:, off_k:off_v]), kg_ref[...])
    v = _dot(h2, win_ref[:, off_v:off_z])
    z_ref[...] = _dot(h2, win_ref[:, off_z:off_xbc])
    xbc_ref[...] = _dot(h2, win_ref[:, off_xbc:off_dt])
    dt_ref[...] = _dot(h2, win_ref[:, off_dt:n_pad])
    if not for_seq_attn:
        k_ref[...] = k
        v_ref[...] = v
    else:
        v_t = v.T
        k_ref[...] = k.T
        v_ref[...] = v_t
        kb_ref, vt_ref, km_ref = maybe_attn_refs
        kb_ref[...] = k.astype(BF16)
        vt = v_t.astype(BF16)
        ones = jnp.ones((VT_ONES, vt.shape[1]), BF16)
        vt_ref[...] = jnp.concatenate(
            [part for h in range(vt.shape[0] // DH) for part in (vt[h * DH:(h + 1) * DH], ones)], axis=0)
        nb = k.shape[0] // MOBA_BLOCK
        km_ref[0] = jnp.mean(k.reshape(nb, MOBA_BLOCK, k.shape[1]), axis=1)


def _ffn_proj(x, lw, *, tm, seq_batch=None, kv_cache_rows=None):
    for_seq_attn = seq_batch is not None
    r, d = x.shape
    d_ff = lw["wg1"].shape[2]
    layer = lw["layer"]
    offs = lw["offs"]
    c_conv, off_q, off_k, off_v, off_z, off_xbc, off_dt, n_pad = offs
    a_w = off_k - off_q
    widths = dict(u=c_conv, q=a_w, k=a_w, v=a_w, z=off_xbc - off_z, xbc=off_dt - off_xbc, dt=n_pad - off_dt)
    row = lambda w: pl.BlockSpec((tm, w), lambda i: (i, 0))
    out_shape = [jax.ShapeDtypeStruct((r, d), F32),
                 jax.ShapeDtypeStruct((r, widths["u"]), F32),
                 jax.ShapeDtypeStruct((r, a_w), F32),
                 jax.ShapeDtypeStruct((r, a_w), F32),
                 jax.ShapeDtypeStruct((r, a_w), F32),
                 jax.ShapeDtypeStruct((r, widths["z"]), F32),
                 jax.ShapeDtypeStruct((r, widths["xbc"]), F32),
                 jax.ShapeDtypeStruct((r, widths["dt"]), F32)]
    out_specs = [row(d), row(widths["u"]), row(a_w), row(a_w), row(a_w),
                 row(widths["z"]), row(widths["xbc"]), row(widths["dt"])]
    in_specs = [row(d), _const_spec((1, d)), _layer_spec(lw["wg1"], layer), _layer_spec(lw["wu1"], layer),
                _layer_spec(lw["wd1"], layer), _const_spec((1, d)), _layer_spec(lw["w_in"], layer),
                _const_spec((1, a_w)), _const_spec((1, a_w))]
    args = [x, lw["n1"], lw["wg1"], lw["wu1"], lw["wd1"], lw["n_mix"], lw["w_in"], lw["qg"], lw["kg"]]
    aliases = {}
    if for_seq_attn:
        depth = lw["wg1"].shape[0]
        t = r // seq_batch
        assert t % tm == 0
        tiles = t // tm
        kv_spec = pl.BlockSpec((None, None, a_w, tm), lambda i: (layer, i // tiles, 0, i % tiles))
        out_shape[3] = out_shape[4] = jax.ShapeDtypeStruct((depth, seq_batch, a_w, t), F32)
        out_specs[3] = out_specs[4] = kv_spec
        assert all(a.shape == (depth, seq_batch, a_w, t) for a in kv_cache_rows)
        aliases = {len(args): 3, len(args) + 1: 4}
        in_specs += [pl.BlockSpec(memory_space=pl.ANY)] * 2
        args += list(kv_cache_rows)
        nb = tm // MOBA_BLOCK
        vt_rows = (a_w // DH) * VT_HEAD_ROWS
        out_shape += [jax.ShapeDtypeStruct((r, a_w), BF16), jax.ShapeDtypeStruct((vt_rows, r), BF16),
                      jax.ShapeDtypeStruct((r // tm, nb, a_w), F32)]
        out_specs += [row(a_w), pl.BlockSpec((vt_rows, tm), lambda i: (0, i)),
                      pl.BlockSpec((1, nb, a_w), lambda i: (i, 0, 0))]
    fchunk = 256 if d_ff % 256 == 0 else d_ff
    return pl.pallas_call(
        functools.partial(_ffn_proj_kernel, fchunk=fchunk, offs=offs, for_seq_attn=for_seq_attn,
                          n_aliased=len(aliases)),
        grid=(r // tm,),
        in_specs=in_specs,
        out_specs=out_specs,
        out_shape=out_shape,
        input_output_aliases=aliases,
        compiler_params=_params("arbitrary"),
        name="ffn_proj",
    )(*args)


def _out_ffn_kernel(x1_ref, a_ref, b_ref, c_ref, wo_ref, n_ref, wg_ref, wu_ref, wd_ref, fn_ref, y_ref,
                    *, fchunk):
    wa, wb = a_ref.shape[1], b_ref.shape[1]
    mix = (_dot(a_ref[...], wo_ref[0:wa, :]) + _dot(b_ref[...], wo_ref[wa:wa + wb, :])
           + _dot(c_ref[...], wo_ref[wa + wb:, :]))
    x2 = x1_ref[...] + mix
    hb = _rmsnorm(x2, n_ref[...]).astype(BF16)
    x3 = x2 + 0.5 * _swiglu(hb, wg_ref, wu_ref, wd_ref, fchunk)
    y_ref[...] = _rmsnorm(x3, fn_ref[...])


def _out_ffn(x1, conv_out, attn_out, ssm_out, lw, *, tm):
    r, d = x1.shape
    d_ff = lw["wg2"].shape[2]
    layer = lw["layer"]
    row = lambda w: pl.BlockSpec((tm, w), lambda i: (i, 0))
    fchunk = 256 if d_ff % 256 == 0 else d_ff
    return pl.pallas_call(
        functools.partial(_out_ffn_kernel, fchunk=fchunk),
        grid=(r // tm,),
        in_specs=[row(d), row(conv_out.shape[1]), row(attn_out.shape[1]), row(ssm_out.shape[1]),
                  _layer_spec(lw["w_out"], layer), _const_spec((1, d)), _layer_spec(lw["wg2"], layer),
                  _layer_spec(lw["wu2"], layer), _layer_spec(lw["wd2"], layer), _const_spec((1, d))],
        out_specs=row(d),
        out_shape=jax.ShapeDtypeStruct((r, d), F32),
        compiler_params=_params("arbitrary"),
        name="out_ffn",
    )(x1, conv_out, attn_out, ssm_out, lw["w_out"], lw["n2"], lw["wg2"], lw["wu2"], lw["wd2"], lw["n_fin"])


CONV_HALO = 32
CONV_SUB = 128


def _layernorm(c, g, b, eps=1e-5):
    xc = c - jnp.mean(c, axis=-1, keepdims=True)
    var = jnp.mean(xc * xc, axis=-1, keepdims=True)
    return xc * lax.rsqrt(var + eps) * g + b


def _conv_seq_kernel(u_ref, w_ref, b_ref, g_ref, lb_ref, o_ref, buf_ref, shift_ref, *, taps):
    tc = u_ref.shape[1]

    @pl.when(pl.program_id(1) == 0)
    def _():
        buf_ref[0:CONV_HALO, :] = jnp.zeros((CONV_HALO, buf_ref.shape[1]), F32)

    buf_ref[CONV_HALO:CONV_HALO + tc, :] = u_ref[0]
    first = CONV_HALO - (taps - 1)
    residues = sorted({(first + k) % SUBLANES for k in range(taps)})
    for r in residues:
        rows = max(first + k - r for k in range(taps) if (first + k) % SUBLANES == r) + tc
        shift_ref[r, 0:rows, :] = buf_ref[r:r + rows, :]
    for s in range(tc // CONV_SUB):
        acc = None
        for k in range(taps):
            r = (first + k) % SUBLANES
            off = s * CONV_SUB + first + k - r
            term = w_ref[k:k + 1, :] * shift_ref[r, off:off + CONV_SUB, :]
            acc = term if acc is None else acc + term
        c = acc + b_ref[...]
        o_ref[0, s * CONV_SUB:(s + 1) * CONV_SUB, :] = _silu(_layernorm(c, g_ref[...], lb_ref[...])).astype(o_ref.dtype)
    buf_ref[0:CONV_HALO, :] = buf_ref[tc:tc + CONV_HALO, :]


def _conv_seq(u, lw, *, tc=256):
    b, t, c = u.shape
    taps = lw["conv_w"].shape[0]
    assert taps - 1 <= CONV_HALO and t % tc == 0 and tc % CONV_SUB == 0
    return pl.pallas_call(
        functools.partial(_conv_seq_kernel, taps=taps),
        grid=(b, t // tc),
        in_specs=[pl.BlockSpec((1, tc, c), lambda i, j: (i, j, 0)),
                  _const_spec((taps, c)), _const_spec((1, c)), _const_spec((1, c)), _const_spec((1, c))],
        out_specs=pl.BlockSpec((1, tc, c), lambda i, j: (i, j, 0)),
        out_shape=jax.ShapeDtypeStruct((b, t, c), BF16),
        scratch_shapes=[pltpu.VMEM((CONV_HALO + tc, c), F32), pltpu.VMEM((SUBLANES, CONV_HALO + tc, c), F32)],
        compiler_params=_params("arbitrary", "arbitrary"),
        name="conv_seq",
    )(u, lw["conv_w"], lw["conv_b"], lw["ln_g"], lw["ln_b"])


def _rel_bias_tile(relb_ref, head, dist):
    max_exact = NUM_BUCKETS // 2
    n = jnp.maximum(dist, 0)
    nf = jnp.maximum(n, 1).astype(F32)
    large = max_exact + (jnp.log(nf / max_exact) / math.log(REL_MAX_DIST / max_exact)
                         * (NUM_BUCKETS - max_exact)).astype(jnp.int32)
    large = jnp.minimum(large, NUM_BUCKETS - 1)
    bucket = jnp.where(n < max_exact, n, large)
    out = jnp.zeros(dist.shape, F32)
    for b in range(NUM_BUCKETS):
        out = jnp.where(bucket == b, relb_ref[b, head], out)
    return out


def _moba_select_t(gate_t, n_eligible):
    nb = gate_t.shape[0]
    row = lax.broadcasted_iota(jnp.int32, gate_t.shape, 0).astype(F32)
    eligible = row < n_eligible.astype(F32)
    gate_t = jnp.where(eligible, gate_t, NEG_INF)
    sel = jnp.full(gate_t.shape, NEG_INF, F32)
    for _ in range(MOBA_TOPK):
        mx = jnp.max(gate_t, axis=0, keepdims=True)
        idx = jnp.min(jnp.where(gate_t == mx, row, float(nb)), axis=0, keepdims=True)
        pick = row == idx
        sel = jnp.where(pick, jnp.where(eligible, 0.0, sel), sel)
        gate_t = jnp.where(pick, NEG_INF, gate_t)
    return sel


def _attn_seq_kernel(relb_ref, q_ref, km_ref, k_ref, vt_ref, o_ref,
                     bias_own_ref, bias_prev_ref, kmbd_ref, selt_ref, acc_ref, s0_ref, s1_ref, *, n_heads, n_blocks):
    bs = MOBA_BLOCK
    width = n_heads * DH
    cur = pl.program_id(1)

    @pl.when((pl.program_id(0) == 0) & (cur == 0))
    def _():
        ki = lax.broadcasted_iota(jnp.int32, (bs, bs), 0)
        qi = lax.broadcasted_iota(jnp.int32, (bs, bs), 1)
        for h in range(n_heads):
            own = _rel_bias_tile(relb_ref, h, qi - ki) * LOG2E
            bias_own_ref[h] = jnp.where(qi >= ki, own, NEG_INF)
            bias_prev_ref[h] = _rel_bias_tile(relb_ref, h, qi - ki + bs) * LOG2E

    @pl.when(cur == 0)
    def _():
        kmbd_ref[...] = jnp.zeros(kmbd_ref.shape, F32)
        km = km_ref[0]
        km_lane = lax.broadcasted_iota(jnp.int32, km.shape, 1)
        for h in range(n_heads):
            kmbd_ref[h * n_blocks:(h + 1) * n_blocks, :] = jnp.where(
                (km_lane >= h * DH) & (km_lane < (h + 1) * DH), km, 0.0)

    qt = q_ref[0].T
    q_hi, q_mid, q_lo = _split3(qt)
    km_hi, km_mid, km_lo = _split3(kmbd_ref[...])
    gate_t = (_dot(km_hi, q_hi) + _dot(km_mid, q_hi) + _dot(km_hi, q_mid)
              + _dot(km_lo, q_hi) + _dot(km_hi, q_lo) + _dot(km_mid, q_mid))
    for h in range(n_heads):
        rows = slice(h * n_blocks, (h + 1) * n_blocks)
        selt_ref[rows, :] = _moba_select_t(gate_t[rows], cur)
    qs = qt * (LOG2E / math.sqrt(DH))
    row_w = lax.broadcasted_iota(jnp.int32, (width, bs), 0)
    qht = [jnp.where((row_w >= h * DH) & (row_w < (h + 1) * DH), qs, 0.0).astype(BF16) for h in range(n_heads)]
    far_bias = [relb_ref[NUM_BUCKETS - 1, h] * LOG2E for h in range(n_heads)]

    def scores(blk):
        kb = k_ref[0, pl.ds(pl.multiple_of(blk * bs, bs), bs), :]
        return [_dot(kb, qht[h]) for h in range(n_heads)]

    def softmax_pv(blk, s_all, ms, bias):
        start = pl.multiple_of(blk * bs, bs)
        out = []
        for h in range(n_heads):
            m = ms[h]
            s = s_all[h]()
            bh = bias(h)
            if bh.shape[0] == 1:
                m_new = jnp.maximum(m, jnp.max(s, axis=0, keepdims=True) + bh)
                p = jnp.exp2(s + (bh - m_new))
            else:
                s = s + bh
                m_new = jnp.maximum(m, jnp.max(s, axis=0, keepdims=True))
                p = jnp.exp2(s - m_new)
            alpha = jnp.exp2(m - m_new)
            vth = vt_ref[h * VT_HEAD_ROWS:(h + 1) * VT_HEAD_ROWS, pl.ds(start, bs)]
            acc_ref[h] = alpha * acc_ref[h] + _dot(vth, p.astype(BF16))
            out.append(m_new)
        return tuple(out)

    def sel_row(h, j):
        return selt_ref[pl.ds(h * n_blocks + j, 1), :]

    n_far = jnp.maximum(cur - 1, 0)
    last_far = jnp.maximum(n_far - 1, 0)
    prev = jnp.maximum(cur - 1, 0)
    s_own, s_prev = scores(cur), scores(prev)

    acc_ref[...] = jnp.zeros(acc_ref.shape, F32)
    ms = (jnp.full((1, bs), NEG_INF, F32),) * n_heads
    ms = softmax_pv(cur, [lambda s=s: s for s in s_own], ms, lambda h: bias_own_ref[h])
    ms = softmax_pv(prev, [lambda s=s: s for s in s_prev], ms,
                    lambda h: bias_prev_ref[h] + jnp.where(cur >= 1, sel_row(h, prev), NEG_INF))

    def put_scores(s_ref, blk):
        for h, s in enumerate(scores(jnp.minimum(blk, last_far))):
            s_ref[h] = s

    def far_bias_row(h, blk):
        row = sel_row(h, jnp.minimum(blk, last_far)) + far_bias[h]
        return jnp.where(blk < n_far, row, NEG_INF)

    def far_pair(i, ms):
        b0, b1 = 2 * i, 2 * i + 1
        put_scores(s1_ref, b1)
        ms = softmax_pv(b0, [lambda h=h: s0_ref[h] for h in range(n_heads)], ms, lambda h: far_bias_row(h, b0))
        put_scores(s0_ref, b1 + 1)
        b1c = jnp.minimum(b1, last_far)
        return softmax_pv(b1c, [lambda h=h: s1_ref[h] for h in range(n_heads)], ms, lambda h: far_bias_row(h, b1))

    put_scores(s0_ref, 0)
    lax.fori_loop(0, (n_far + 1) // 2, far_pair, ms)
    out_t = jnp.concatenate([acc_ref[h, :DH] * (1.0 / acc_ref[h, DH:DH + 1]) for h in range(n_heads)], axis=0)
    o_ref[0] = out_t.T.astype(o_ref.dtype)


def _attn_seq(q, kb, vt, kmean, rel_bias, *, n_heads):
    b, t, w = q.shape
    nb = t // MOBA_BLOCK
    assert vt.shape == (n_heads * VT_HEAD_ROWS, b * t)
    assert t % MOBA_BLOCK == 0 and nb * n_heads <= LANES and nb % SUBLANES == 0
    return pl.pallas_call(
        functools.partial(_attn_seq_kernel, n_heads=n_heads, n_blocks=nb),
        grid=(b, nb),
        in_specs=[pl.BlockSpec(memory_space=pltpu.SMEM),
                  pl.BlockSpec((1, MOBA_BLOCK, w), lambda i, j: (i, j, 0)),
                  pl.BlockSpec((1, nb, w), lambda i, j: (i, 0, 0)),
                  pl.BlockSpec((1, t, w), lambda i, j: (i, 0, 0)),
                  pl.BlockSpec((n_heads * VT_HEAD_ROWS, t), lambda i, j: (0, i))],
        out_specs=pl.BlockSpec((1, MOBA_BLOCK, w), lambda i, j: (i, j, 0)),
        out_shape=jax.ShapeDtypeStruct((b, t, w), BF16),
        scratch_shapes=[pltpu.VMEM((n_heads, MOBA_BLOCK, MOBA_BLOCK), F32),
                        pltpu.VMEM((n_heads, MOBA_BLOCK, MOBA_BLOCK), F32),
                        pltpu.VMEM((LANES, w), F32),
                        pltpu.VMEM((LANES, MOBA_BLOCK), F32),
                        pltpu.VMEM((n_heads, VT_HEAD_ROWS, MOBA_BLOCK), F32),
                        pltpu.VMEM((n_heads, MOBA_BLOCK, MOBA_BLOCK), F32),
                        pltpu.VMEM((n_heads, MOBA_BLOCK, MOBA_BLOCK), F32)],
        compiler_params=_params("arbitrary", "arbitrary"),
        name="attn_seq",
    )(rel_bias, q, kmean, kb, vt)


SSM_HALO = 8


def _softplus(x):
    return jnp.maximum(x, 0.0) + jnp.log1p(jnp.exp(-jnp.abs(x)))


def _ssd_seq_kernel(xbc_ref, z_ref, dt_ref, cw_ref, cb_ref, dtb_ref, alog_ref, dvec_ref, ng_ref,
                    y_ref, hfin_ref, buf_ref, state_ref, *, n_heads):
    lc = SSD_CHUNK
    n = SSM_N
    d_inner = n_heads * SSM_P
    pair_w = 2 * SSM_P
    heads_per_group = n_heads // SSM_G
    chunk = pl.program_id(1)

    @pl.when(chunk == 0)
    def _():
        buf_ref[0:SSM_HALO, :] = jnp.zeros((SSM_HALO, buf_ref.shape[1]), F32)
        state_ref[...] = jnp.zeros(state_ref.shape, F32)

    buf_ref[SSM_HALO:SSM_HALO + lc, :] = xbc_ref[0]
    taps = cw_ref.shape[0]
    first = SSM_HALO - (taps - 1)
    acc = None
    for k in range(taps):
        term = cw_ref[k:k + 1, :] * buf_ref[first + k:first + k + lc, :]
        acc = term if acc is None else acc + term
    xc = _silu(acc + cb_ref[...])
    buf_ref[0:SSM_HALO, :] = buf_ref[lc:lc + SSM_HALO, :]
    xs = xc[:, :d_inner]
    bm = xc[:, d_inner:d_inner + SSM_G * n]
    cm = xc[:, d_inner + SSM_G * n:]

    dt = _softplus(dt_ref[0] + dtb_ref[...])
    head_lane = lax.broadcasted_iota(jnp.int32, (lc, LANES), 1) < n_heads
    dta = jnp.where(head_lane, dt * (-jnp.exp(alog_ref[...])), 0.0)
    ri = lax.broadcasted_iota(jnp.int32, (lc, lc), 0)
    ci = lax.broadcasted_iota(jnp.int32, (lc, lc), 1)
    causal = ri >= ci
    acum = _dot_exact_lhs(jnp.where(causal, 1.0, 0.0).astype(BF16), dta)
    acum2 = acum * LOG2E
    acum2_t = acum2.T
    er = lax.broadcasted_iota(jnp.int32, (LANES, d_inner), 0)
    ec = lax.broadcasted_iota(jnp.int32, (LANES, d_inner), 1) // SSM_P
    expand = jnp.where(er == ec, 1.0, 0.0).astype(BF16)
    last = acum[lc - 1:lc, :]
    dt_x = _dot_exact_rhs(dt, expand)
    grow_x = _dot_exact_rhs(jnp.exp(acum), expand)
    toend_x = _dot_exact_rhs(jnp.exp(last - acum), expand)
    xdt = xs * dt_x
    xdt_b = xdt.astype(BF16)
    xte = xdt * toend_x

    cg_b, bg_b, cb = [], [], []
    for g in range(SSM_G):
        cg_b.append(cm[:, g * n:(g + 1) * n].astype(BF16))
        bg_b.append(bm[:, g * n:(g + 1) * n].astype(BF16))
        cb.append(_dot_t(cg_b[g], bg_b[g]))

    lane_p = lax.broadcasted_iota(jnp.int32, (lc, pair_w), 1)
    row_p = lax.broadcasted_iota(jnp.int32, (pair_w, n), 0)
    y_tiles = []
    for pair in range(n_heads // 2):
        sl = slice(pair * pair_w, (pair + 1) * pair_w)
        g = (2 * pair) // heads_per_group
        x_pair = xdt_b[:, sl]
        st_old = state_ref[sl, :]
        y_inter = _dot_t(cg_b[g], st_old.astype(BF16)) * grow_x[:, sl]
        decays = [jnp.exp(last[:, 2 * pair + half:2 * pair + half + 1]) for half in range(2)]
        state_ref[sl, :] = (st_old * jnp.where(row_p < SSM_P, decays[0], decays[1])
                            + _dot(xte[:, sl].T.astype(BF16), bg_b[g]))
        y_intra = None
        for half in range(2):
            h = 2 * pair + half
            col = acum2[:, h:h + 1]
            row = acum2_t[h:h + 1, :]
            scores = jnp.where(causal, cb[g] * jnp.exp2(col - row), 0.0).astype(BF16)
            in_half = (lane_p >= half * SSM_P) & (lane_p < (half + 1) * SSM_P)
            part = _dot(scores, jnp.where(in_half, x_pair, jnp.zeros_like(x_pair)))
            y_intra = part if y_intra is None else y_intra + part
        y_tiles.append(y_intra + y_inter)
    y = jnp.concatenate(y_tiles, axis=1) + dvec_ref[...] * xs
    gated = y * _silu(z_ref[0])
    y_ref[0] = _rmsnorm(gated, ng_ref[...]).astype(y_ref.dtype)

    @pl.when(chunk == pl.num_programs(1) - 1)
    def _():
        hfin_ref[0] = state_ref[...]


def _ssd_seq(xbc, z, dt, lw, *, n_heads):
    b, t, cd = xbc.shape
    d_inner = z.shape[2]
    lc = SSD_CHUNK
    assert t % lc == 0 and n_heads % (2 * SSM_G) == 0 and n_heads <= LANES
    taps = lw["ssm_cw"].shape[0]
    assert taps - 1 <= SSM_HALO
    tile = lambda w: pl.BlockSpec((1, lc, w), lambda i, j: (i, j, 0))
    return pl.pallas_call(
        functools.partial(_ssd_seq_kernel, n_heads=n_heads),
        grid=(b, t // lc),
        in_specs=[tile(cd), tile(d_inner), tile(LANES),
                  _const_spec((taps, cd)), _const_spec((1, cd)), _const_spec((1, LANES)),
                  _const_spec((1, LANES)), _const_spec((1, d_inner)), _const_spec((1, d_inner))],
        out_specs=[tile(d_inner), pl.BlockSpec((1, d_inner, SSM_N), lambda i, j: (i, 0, 0))],
        out_shape=[jax.ShapeDtypeStruct((b, t, d_inner), BF16),
                   jax.ShapeDtypeStruct((b, d_inner, SSM_N), F32)],
        scratch_shapes=[pltpu.VMEM((SSM_HALO + lc, cd), F32), pltpu.VMEM((d_inner, SSM_N), F32)],
        compiler_params=_params("arbitrary", "arbitrary"),
        name="ssd_seq",
    )(xbc, z, dt, lw["ssm_cw"], lw["ssm_cb"], lw["dt_bias"], lw["a_log"], lw["d_vec"], lw["ssm_ng"])


def _stacked_matrices(w):
    n_in = w["w_in"].shape[2]
    n_pad = n_in - w["ssm_dt_bias"].shape[1] + LANES
    return dict(
        wg1=w["ffn1_wg"].astype(BF16), wu1=w["ffn1_wu"].astype(BF16), wd1=w["ffn1_wd"].astype(BF16),
        w_in=jnp.pad(w["w_in"], ((0, 0), (0, 0), (0, n_pad - n_in))).astype(BF16), w_out=w["w_out"].astype(BF16),
        wg2=w["ffn2_wg"].astype(BF16), wu2=w["ffn2_wu"].astype(BF16), wd2=w["ffn2_wd"].astype(BF16))


def _layer_weights(l, w, stacked):
    c_conv = w["conv_dw_w"].shape[2]
    a_w = (w["w_out"].shape[1] - c_conv - w["ssm_norm_g"].shape[1])
    d_inner = w["ssm_norm_g"].shape[1]
    cd = w["ssm_conv_w"].shape[2]
    n_ssm_heads = w["ssm_dt_bias"].shape[1]
    off_q = 2 * c_conv
    off_k = off_q + a_w
    off_v = off_k + a_w
    off_z = off_v + a_w
    off_xbc = off_z + d_inner
    off_dt = off_xbc + cd
    n_in = off_dt + n_ssm_heads
    assert w["w_in"].shape[2] == n_in
    n_pad = off_dt + LANES
    row = lambda a: a[l].reshape(1, -1).astype(F32)
    pad_row = lambda a: jnp.pad(a[l].astype(F32), (0, LANES - a.shape[1])).reshape(1, LANES)
    assert stacked["w_in"].shape[2] == n_pad
    return dict(
        stacked, layer=l,
        offs=(c_conv, off_q, off_k, off_v, off_z, off_xbc, off_dt, n_pad),
        n1=row(w["ffn1_norm"]), n_mix=row(w["mix_norm"]),
        qg=jnp.tile(w["q_norm_g"][l], a_w // DH).reshape(1, a_w),
        kg=jnp.tile(w["k_norm_g"][l], a_w // DH).reshape(1, a_w),
        conv_w=w["conv_dw_w"][l], conv_b=row(w["conv_dw_b"]), ln_g=row(w["conv_ln_g"]), ln_b=row(w["conv_ln_b"]),
        ssm_cw=w["ssm_conv_w"][l], ssm_cb=row(w["ssm_conv_b"]),
        dt_bias=pad_row(w["ssm_dt_bias"]), a_log=pad_row(w["ssm_a_log"]),
        d_vec=jnp.repeat(w["ssm_d"][l], SSM_P).reshape(1, d_inner), ssm_ng=row(w["ssm_norm_g"]),
        n2=row(w["ffn2_norm"]), n_fin=row(w["final_norm"]),
        n_attn_heads=a_w // DH, n_ssm_heads=n_ssm_heads,
    )


def _row_tile(rows, want=512):
    tm = min(want, rows)
    assert rows % tm == 0 and tm % SUBLANES == 0
    return tm


def _layer_seq(x, lw, rel_bias, kv_cache_rows):
    b, t, d = x.shape
    n_ah, n_sh = lw["n_attn_heads"], lw["n_ssm_heads"]
    tm = _row_tile(b * t)
    assert t % tm == 0 and tm % MOBA_BLOCK == 0
    x1, u, q, kt_rows, vt_rows, z, xbc, dt, kb, vt, km = _ffn_proj(
        x.reshape(b * t, d), lw, tm=tm, seq_batch=b, kv_cache_rows=kv_cache_rows)
    seq = lambda a: a.reshape(b, t, a.shape[-1])
    u, xbc = seq(u), seq(xbc)
    conv_out = _conv_seq(u, lw)
    attn_out = _attn_seq(seq(q), seq(kb), vt, km.reshape(b, t // MOBA_BLOCK, -1), rel_bias, n_heads=n_ah)
    ssm_out, h_fin = _ssd_seq(xbc, seq(z), seq(dt), lw, n_heads=n_sh)
    flat = lambda a: a.reshape(b * t, a.shape[-1])
    y = _out_ffn(x1, flat(conv_out), flat(attn_out), flat(ssm_out), lw, tm=tm)
    conv_taps = lw["conv_w"].shape[0]
    ssm_taps = lw["ssm_cw"].shape[0]
    states = (u[:, t - (conv_taps - 1):], xbc[:, t - (ssm_taps - 1):], h_fin.reshape(b, n_sh, SSM_P, SSM_N))
    return y.reshape(b, t, d), (kt_rows, vt_rows), states


def _conv_step_kernel(u_ref, st_ref, w_ref, b_ref, g_ref, lb_ref, o_ref):
    past = st_ref.shape[0]
    acc = w_ref[past:past + 1, :] * u_ref[...]
    for k in range(past):
        acc = acc + w_ref[k:k + 1, :] * st_ref[k]
    o_ref[...] = _silu(_layernorm(acc + b_ref[...], g_ref[...], lb_ref[...])).astype(o_ref.dtype)


def _conv_step(u, state_t, lw):
    return pl.pallas_call(
        _conv_step_kernel,
        out_shape=jax.ShapeDtypeStruct(u.shape, BF16),
        compiler_params=pltpu.CompilerParams(vmem_limit_bytes=VMEM_LIMIT_BYTES),
        name="conv_step",
    )(u, state_t, lw["conv_w"], lw["conv_b"], lw["ln_g"], lw["ln_b"])


SSD_STEP_ROWS = 8


def _ssd_step_kernel(xn_ref, st_ref, z_ref, dt_ref, h0_ref, cw_ref, cb_ref, dtb_ref, alog_ref, dvec_ref, ng_ref,
                     y_ref, h1_ref, *, n_heads):
    rows = xn_ref.shape[0]
    n = SSM_N
    d_inner = n_heads * SSM_P
    group_rows = (n_heads // SSM_G) * SSM_P
    past = st_ref.shape[0]
    acc = cw_ref[past:past + 1, :] * xn_ref[...]
    for k in range(past):
        acc = acc + cw_ref[k:k + 1, :] * st_ref[k]
    xc = _silu(acc + cb_ref[...])
    xs = xc[:, :d_inner]
    bm = xc[:, d_inner:d_inner + SSM_G * n]
    cm = xc[:, d_inner + SSM_G * n:]
    dt = _softplus(dt_ref[...] + dtb_ref[...])
    dta = dt * (-jnp.exp(alog_ref[...]))
    er = lax.broadcasted_iota(jnp.int32, (LANES, d_inner), 0)
    ec = lax.broadcasted_iota(jnp.int32, (LANES, d_inner), 1) // SSM_P
    expand = jnp.where(er == ec, 1.0, 0.0).astype(BF16)
    xdt = xs * _dot_exact_rhs(dt, expand)
    dec = jnp.exp(_dot_exact_rhs(dta, expand))
    pad = jnp.zeros((LANES - rows, d_inner), F32)
    xdt_t = jnp.concatenate([xdt, pad], axis=0).T
    dec_t = jnp.concatenate([dec, pad], axis=0).T
    lane = lax.broadcasted_iota(jnp.int32, (d_inner, LANES), 1)
    y_t = jnp.zeros((d_inner, LANES), F32)
    for b in range(rows):
        xcol = xdt_t[:, b:b + 1]
        dcol = dec_t[:, b:b + 1]
        ycols = []
        for g in range(SSM_G):
            rs = slice(g * group_rows, (g + 1) * group_rows)
            h0 = h0_ref[b, rs, :]
            brow = bm[b:b + 1, g * n:(g + 1) * n]
            crow = cm[b:b + 1, g * n:(g + 1) * n]
            cb = jnp.sum(crow * brow, axis=1, keepdims=True)
            h1_ref[b, rs, :] = h0 * dcol[rs] + xcol[rs] * brow
            ycols.append(jnp.sum(h0 * crow, axis=1, keepdims=True) * dcol[rs] + cb * xcol[rs])
        y_t = jnp.where(lane == b, jnp.concatenate(ycols, axis=0), y_t)
    y = y_t.T[:rows] + dvec_ref[...] * xs
    gated = y * _silu(z_ref[...])
    y_ref[...] = _rmsnorm(gated, ng_ref[...]).astype(y_ref.dtype)


def _ssd_step(xn, state_t, z, dt, h0_all, layer, lw, *, n_heads):
    db, cd = xn.shape
    d_inner = z.shape[1]
    rows = SSD_STEP_ROWS
    assert db % rows == 0
    past = state_t.shape[0]
    row = lambda w: pl.BlockSpec((rows, w), lambda i: (i, 0))
    return pl.pallas_call(
        functools.partial(_ssd_step_kernel, n_heads=n_heads),
        grid=(db // rows,),
        in_specs=[row(cd), pl.BlockSpec((past, rows, cd), lambda i: (0, i, 0)), row(d_inner), row(LANES),
                  pl.BlockSpec((None, rows, d_inner, SSM_N), lambda i: (layer, i, 0, 0)),
                  _const_spec((past + 1, cd)), _const_spec((1, cd)), _const_spec((1, LANES)),
                  _const_spec((1, LANES)), _const_spec((1, d_inner)), _const_spec((1, d_inner))],
        out_specs=[row(d_inner), pl.BlockSpec((rows, d_inner, SSM_N), lambda i: (i, 0, 0))],
        out_shape=[jax.ShapeDtypeStruct((db, d_inner), BF16), jax.ShapeDtypeStruct(h0_all.shape[1:], F32)],
        compiler_params=_params("arbitrary"),
        name="ssd_step",
    )(xn, state_t, z, dt, h0_all, lw["ssm_cw"], lw["ssm_cb"], lw["dt_bias"], lw["a_log"], lw["d_vec"], lw["ssm_ng"])


GATE_PAGES = 32


def _gate_pages_kernel(pt_ref, q_ref, *refs, n_blocks):
    del pt_ref
    page_refs, idx_ref, kmean_ref = refs[:-2], refs[-2], refs[-1]
    per_block = MOBA_BLOCK // PAGE_SIZE
    blocks = len(page_refs) // per_block
    chunk = pl.program_id(1)
    n_heads = kmean_ref.shape[0]

    @pl.when(chunk == 0)
    def _():
        kmean_ref[...] = jnp.zeros(kmean_ref.shape, F32)

    lane3 = lax.broadcasted_iota(jnp.int32, kmean_ref.shape, 2)
    km = kmean_ref[...]
    for blk in range(blocks):
        s = None
        for r in range(per_block):
            page = page_refs[blk * per_block + r][...]
            s = page if s is None else s + page
        col = jnp.sum(s, axis=2, keepdims=True) * (1.0 / MOBA_BLOCK)
        km = jnp.where(lane3 == chunk * blocks + blk, col, km)
    kmean_ref[...] = km

    @pl.when(chunk == pl.num_programs(1) - 1)
    def _():
        lane = lax.broadcasted_iota(jnp.int32, (SUBLANES, LANES), 1)
        for h in range(n_heads):
            q_hi, q_mid, q_lo = _split3(jnp.broadcast_to(q_ref[0, h], (SUBLANES, DH)))
            k_hi, k_mid, k_lo = _split3(kmean_ref[h])
            gate = (_dot(q_hi, k_hi) + _dot(q_hi, k_mid) + _dot(q_mid, k_hi)
                    + _dot(q_hi, k_lo) + _dot(q_lo, k_hi) + _dot(q_mid, k_mid))
            gate = jnp.where(lane < n_blocks, gate, NEG_INF)
            picks = jnp.zeros((SUBLANES, LANES), jnp.int32)
            for rank in range(MOBA_TOPK):
                mx = jnp.max(gate, axis=1, keepdims=True)
                idx = jnp.min(jnp.where(gate == mx, lane, LANES), axis=1, keepdims=True)
                picks = jnp.where(lane == rank, idx, picks)
                gate = jnp.where(lane == idx, NEG_INF, gate)
            idx_ref[0, h] = picks


def _gate_pages(q4, cache_t, layer, page_table_flat, n_pages):
    db, n_heads = q4.shape[:2]
    per_block = MOBA_BLOCK // PAGE_SIZE
    n_blocks = n_pages // per_block
    pg = min(GATE_PAGES, n_pages)
    assert n_pages % pg == 0 and pg % per_block == 0 and MOBA_TOPK <= n_blocks <= LANES

    def page_spec(j):
        return pl.BlockSpec((None, None, n_heads, DH, PAGE_SIZE),
                            lambda b, i, pt: (layer, pt[b * n_pages + i * pg + j], 0, 0, 0))

    return pl.pallas_call(
        functools.partial(_gate_pages_kernel, n_blocks=n_blocks),
        grid_spec=pltpu.PrefetchScalarGridSpec(
            num_scalar_prefetch=1, grid=(db, n_pages // pg),
            in_specs=[pl.BlockSpec((1, n_heads, 1, DH), lambda b, i, pt: (b, 0, 0, 0))]
            + [page_spec(j) for j in range(pg)],
            out_specs=pl.BlockSpec((1, n_heads, SUBLANES, LANES), lambda b, i, pt: (b, 0, 0, 0)),
            scratch_shapes=[pltpu.VMEM((n_heads, DH, LANES), F32)]),
        out_shape=jax.ShapeDtypeStruct((db, n_heads, SUBLANES, LANES), jnp.int32),
        compiler_params=_params("arbitrary", "arbitrary"),
        name="gate_pages",
    )(page_table_flat, q4, *([cache_t] * pg))


def _attn_step_kernel(pt_ref, sel_ref, relb_ref, q_ref, kn_ref, vn_ref, *refs, n_heads, past_len):
    del pt_ref
    per_block = MOBA_BLOCK // PAGE_SIZE
    n_pg = MOBA_TOPK * per_block
    k_pages, v_pages, o_ref = refs[:n_pg], refs[n_pg:2 * n_pg], refs[2 * n_pg]
    b, h = pl.program_id(0), pl.program_id(1)
    qh = q_ref[...] * (1.0 / math.sqrt(DH))
    q8 = jnp.broadcast_to(qh, (SUBLANES, DH)).astype(BF16)
    key_off = lax.broadcasted_iota(jnp.int32, (1, PAGE_SIZE), 1)
    s_own = jnp.sum(qh * kn_ref[...], axis=1, keepdims=True) + relb_ref[0, h]
    logits = []
    for j in range(MOBA_TOPK):
        blk = sel_ref[(b * n_heads + h) * MOBA_TOPK + j]
        for r in range(per_block):
            dist = past_len - (blk * MOBA_BLOCK + r * PAGE_SIZE + key_off)
            kt = k_pages[j * per_block + r][...].astype(BF16)
            logits.append(_dot(q8, kt)[0:1, :] + _rel_bias_tile(relb_ref, h, dist))
    m = s_own
    for s in logits:
        m = jnp.maximum(m, jnp.max(s, axis=1, keepdims=True))
    p_own = jnp.exp(s_own - m)
    l = p_own
    acc = p_own * vn_ref[...]
    for page, s in enumerate(logits):
        p = jnp.exp(s - m)
        l = l + jnp.sum(p, axis=1, keepdims=True)
        vt = v_pages[page][...].astype(BF16)
        acc = acc + _dot_t(jnp.broadcast_to(p, (SUBLANES, PAGE_SIZE)).astype(BF16), vt)[0:1, :]
    o_ref[...] = jnp.broadcast_to(acc / l, (SUBLANES, DH))


def _attn_step(q4, kn4, vn4, cache_kt, cache_vt, layer, page_table_flat, sel_flat, rel_bias, *, n_pages):
    db, n_heads = q4.shape[:2]
    per_block = MOBA_BLOCK // PAGE_SIZE
    past_len = n_pages * PAGE_SIZE
    assert past_len % MOBA_BLOCK == 0 and past_len // MOBA_BLOCK >= MOBA_TOPK

    def page_spec(j, r):
        def index_map(b, h, pt, sel):
            blk = sel[(b * n_heads + h) * MOBA_TOPK + j]
            return (layer, pt[b * n_pages + blk * per_block + r], h, 0, 0)
        return pl.BlockSpec((None, None, None, DH, PAGE_SIZE), index_map)

    page_specs = [page_spec(j, r) for j in range(MOBA_TOPK) for r in range(per_block)]
    tok = pl.BlockSpec((None, None, 1, DH), lambda b, h, pt, sel: (b, h, 0, 0))
    n_pg = len(page_specs)
    return pl.pallas_call(
        functools.partial(_attn_step_kernel, n_heads=n_heads, past_len=past_len),
        grid_spec=pltpu.PrefetchScalarGridSpec(
            num_scalar_prefetch=2, grid=(db, n_heads),
            in_specs=[pl.BlockSpec(memory_space=pltpu.SMEM), tok, tok, tok] + page_specs + page_specs,
            out_specs=pl.BlockSpec((None, None, SUBLANES, DH), lambda b, h, pt, sel: (b, h, 0, 0))),
        out_shape=jax.ShapeDtypeStruct((db, n_heads, SUBLANES, DH), F32),
        compiler_params=_params("arbitrary", "arbitrary"),
        name="attn_step",
    )(page_table_flat, sel_flat, rel_bias, q4, kn4, vn4, *([cache_kt] * n_pg), *([cache_vt] * n_pg))


def _layer_step(x, conv_state, ssm_conv_state, ssm_states, cache_kt, cache_vt, layer, page_table, lw, rel_bias):
    db, t, d = x.shape
    assert t == 1
    n_ah, n_sh = lw["n_attn_heads"], lw["n_ssm_heads"]
    tm = _row_tile(db)
    x1, u, q, k, v, z, xbc, dt = _ffn_proj(x.reshape(db, d), lw, tm=tm)
    conv_out = _conv_step(u, conv_state.transpose(1, 0, 2), lw)
    n_pages = page_table.shape[1]
    pt_flat = page_table.reshape(-1)
    q4 = q.reshape(db, n_ah, 1, DH)
    picks = _gate_pages(q4, cache_kt, layer, pt_flat, n_pages)
    sel_flat = picks[:, :, 0, :MOBA_TOPK].reshape(-1)
    attn = _attn_step(q4, k.reshape(db, n_ah, 1, DH), v.reshape(db, n_ah, 1, DH), cache_kt, cache_vt, layer,
                      pt_flat, sel_flat, rel_bias, n_pages=n_pages)
    attn_out = attn[:, :, 0, :].reshape(db, n_ah * DH)
    depth = ssm_states.shape[0]
    ssm_out, h_new = _ssd_step(xbc, ssm_conv_state.transpose(1, 0, 2), z, dt,
                               ssm_states.reshape(depth, db, n_sh * SSM_P, SSM_N), layer, lw, n_heads=n_sh)
    y = _out_ffn(x1, conv_out, attn_out.astype(BF16), ssm_out, lw, tm=tm)
    states = (k.reshape(db, 1, n_ah, DH), v.reshape(db, 1, n_ah, DH),
              jnp.concatenate([conv_state[:, 1:], u[:, None, :]], axis=1),
              jnp.concatenate([ssm_conv_state[:, 1:], xbc[:, None, :]], axis=1),
              h_new.reshape(ssm_states.shape[1:]))
    return y.reshape(db, 1, d), states


def kernel(x_prompt, x_sample, cache_k, cache_v, state_conv, state_ssm_conv, state_ssm, page_table, rel_bias, ffn1_norm, ffn1_wg, ffn1_wu, ffn1_wd, mix_norm, w_in, w_out, conv_dw_w, conv_dw_b, conv_ln_g, conv_ln_b, q_norm_g, k_norm_g, ssm_conv_w, ssm_conv_b, ssm_dt_bias, ssm_a_log, ssm_d, ssm_norm_g, ffn2_norm, ffn2_wg, ffn2_wu, ffn2_wd, final_norm):
    w = dict(ffn1_norm=ffn1_norm, ffn1_wg=ffn1_wg, ffn1_wu=ffn1_wu, ffn1_wd=ffn1_wd, mix_norm=mix_norm,
             w_in=w_in, w_out=w_out, conv_dw_w=conv_dw_w, conv_dw_b=conv_dw_b, conv_ln_g=conv_ln_g,
             conv_ln_b=conv_ln_b, q_norm_g=q_norm_g, k_norm_g=k_norm_g, ssm_conv_w=ssm_conv_w,
             ssm_conv_b=ssm_conv_b, ssm_dt_bias=ssm_dt_bias, ssm_a_log=ssm_a_log, ssm_d=ssm_d,
             ssm_norm_g=ssm_norm_g, ffn2_norm=ffn2_norm, ffn2_wg=ffn2_wg, ffn2_wu=ffn2_wu, ffn2_wd=ffn2_wd,
             final_norm=final_norm)
    depth = w_in.shape[0]
    y_p, y_s = x_prompt, x_sample
    st_p, st_s = [], []
    cache_kt = cache_k.transpose(0, 1, 3, 4, 2)
    cache_vt = cache_v.transpose(0, 1, 3, 4, 2)
    stacked = _stacked_matrices(w)
    bp, tp = x_prompt.shape[:2]
    n_ah, a_w = cache_k.shape[3], cache_k.shape[3] * cache_k.shape[4]
    kv_rows = (jnp.zeros((depth, bp, a_w, tp), F32), jnp.zeros((depth, bp, a_w, tp), F32))
    for l in range(depth):
        lw = _layer_weights(l, w, stacked)
        y_p, kv_rows, st = _layer_seq(y_p, lw, rel_bias, kv_rows)
        st_p.append(st)
        y_s, st = _layer_step(y_s, state_conv[l], state_ssm_conv[l], state_ssm, cache_kt, cache_vt, l,
                              page_table, lw, rel_bias)
        st_s.append(st)
    stack = lambda sts, i: jnp.stack([s[i] for s in sts])
    kv_p = tuple(a.reshape(depth, bp, n_ah, DH, tp).transpose(0, 1, 4, 2, 3) for a in kv_rows)
    return ((y_p, y_s) + kv_p + tuple(stack(st_p, i) for i in range(3))
            + tuple(stack(st_s, i) for i in range(5)))
```

```python
import functools
import math

import jax
import jax.numpy as jnp
from jax import lax
from jax.experimental import pallas as pl
from jax.experimental.pallas import tpu as pltpu

F32 = jnp.float32
BF16 = jnp.bfloat16

DH = 64
MOBA_BLOCK = 256
MOBA_TOPK = 3
NUM_BUCKETS = 32
REL_MAX_DIST = 128
SSM_P = 64
SSM_G = 2
SSM_N = 128
SSD_CHUNK = 256
PAGE_SIZE = 128

LANES = 128
SUBLANES = 8
VMEM_LIMIT_BYTES = 56 * 1024 * 1024
BF16_SUBLANES = 16

VT_ONES = BF16_SUBLANES
VT_HEAD_ROWS = DH + VT_ONES
LOG2E = math.log2(math.e)

NEG_INF = float("-inf")


def _params(*semantics):
    return pltpu.CompilerParams(dimension_semantics=semantics, vmem_limit_bytes=VMEM_LIMIT_BYTES)


def _dot(a, b):
    return jnp.dot(a, b, preferred_element_type=F32)


def _dot_t(a, b):
    return lax.dot_general(a, b, (((1,), (1,)), ((), ())), preferred_element_type=F32)


def _split3(x):
    hi = x.astype(BF16)
    r1 = x - hi.astype(F32)
    mid = r1.astype(BF16)
    lo = (r1 - mid.astype(F32)).astype(BF16)
    return hi, mid, lo


def _dot_exact_rhs(x, sel):
    hi, mid, lo = _split3(x)
    return _dot(hi, sel) + _dot(mid, sel) + _dot(lo, sel)


def _dot_exact_lhs(sel, x):
    hi, mid, lo = _split3(x)
    return _dot(sel, hi) + _dot(sel, mid) + _dot(sel, lo)


def _sigmoid(x):
    return 1.0 / (1.0 + jnp.exp(-x))


def _silu(x):
    return x * _sigmoid(x)


def _rmsnorm(x, g, eps=1e-6):
    return x * lax.rsqrt(jnp.mean(x * x, axis=-1, keepdims=True) + eps) * g


def _const_spec(shape):
    nd = len(shape)
    return pl.BlockSpec(shape, lambda *_: (0,) * nd, pipeline_mode=pl.Buffered(1))


def _layer_spec(stacked, layer):
    return pl.BlockSpec((None,) + stacked.shape[1:], lambda *_: (layer, 0, 0), pipeline_mode=pl.Buffered(1))


def _swiglu(hb, wg_ref, wu_ref, wd_ref, fchunk):
    d_ff = wg_ref.shape[1]
    acc = None
    for c in range(d_ff // fchunk):
        sl = slice(c * fchunk, (c + 1) * fchunk)
        g = _dot(hb, wg_ref[:, sl])
        u = _dot(hb, wu_ref[:, sl])
        a = (_silu(g) * u).astype(BF16)
        part = _dot(a, wd_ref[sl, :])
        acc = part if acc is None else acc + part
    return acc


def _head_sumsq_matrix(width):
    r = lax.broadcasted_iota(jnp.int32, (width, width), 0) // DH
    c = lax.broadcasted_iota(jnp.int32, (width, width), 1) // DH
    return jnp.where(r == c, 1.0, 0.0).astype(BF16)


def _head_rmsnorm(p, g_row, eps=1e-6):
    ss = _dot_exact_rhs(p * p, _head_sumsq_matrix(p.shape[1]))
    return p * lax.rsqrt(ss * (1.0 / DH) + eps) * g_row


def _ffn_proj_kernel(x_ref, n1_ref, wg_ref, wu_ref, wd_ref, n2_ref, win_ref, qg_ref, kg_ref, *refs,
                     fchunk, offs, for_seq_attn, n_aliased):
    x1_ref, u_ref, q_ref, k_ref, v_ref, z_ref, xbc_ref, dt_ref, *maybe_attn_refs = refs[n_aliased:]
    c_conv, off_q, off_k, off_v, off_z, off_xbc, off_dt, n_pad = offs
    x = x_ref[...]
    hb = _rmsnorm(x, n1_ref[...]).astype(BF16)
    x1 = x + 0.5 * _swiglu(hb, wg_ref, wu_ref, wd_ref, fchunk)
    x1_ref[...] = x1
    h2 = _rmsnorm(x1, n2_ref[...]).astype(BF16)
    pg = _dot(h2, win_ref[:, 0:off_q])
    u_ref[...] = pg[:, :c_conv] * _sigmoid(pg[:, c_conv:])
    q_ref[...] = _head_rmsnorm(_dot(h2, win_ref[:, off_q:off_k]), qg_ref[...])
    k = _head_rmsnorm(_dot(h2, win_ref[:, off_k:off_v]), kg_ref[...])
    v = _dot(h2, win_ref[:, off_v:off_z])
    z_ref[...] = _dot(h2, win_ref[:, off_z:off_xbc])
    xbc_ref[...] = _dot(h2, win_ref[:, off_xbc:off_dt])
    dt_ref[...] = _dot(h2, win_ref[:, off_dt:n_pad])
    if not for_seq_attn:
        k_ref[...] = k
        v_ref[...] = v
    else:
        v_t = v.T
        k_ref[...] = k.T
        v_ref[...] = v_t
        kb_ref, vt_ref, km_ref = maybe_attn_refs
        kb_ref[...] = k.astype(BF16)
        vt = v_t.astype(BF16)
        ones = jnp.ones((VT_ONES, vt.shape[1]), BF16)
        vt_ref[...] = jnp.concatenate(
            [part for h in range(vt.shape[0] // DH) for part in (vt[h * DH:(h + 1) * DH], ones)], axis=0)
        nb = k.shape[0] // MOBA_BLOCK
        km_ref[0] = jnp.mean(k.reshape(nb, MOBA_BLOCK, k.shape[1]), axis=1)


def _ffn_proj(x, lw, *, tm, seq_batch=None, kv_cache_rows=None):
    for_seq_attn = seq_batch is not None
    r, d = x.shape
    d_ff = lw["wg1"].shape[2]
    layer = lw["layer"]
    offs = lw["offs"]
    c_conv, off_q, off_k, off_v, off_z, off_xbc, off_dt, n_pad = offs
    a_w = off_k - off_q
    widths = dict(u=c_conv, q=a_w, k=a_w, v=a_w, z=off_xbc - off_z, xbc=off_dt - off_xbc, dt=n_pad - off_dt)
    row = lambda w: pl.BlockSpec((tm, w), lambda i: (i, 0))
    out_shape = [jax.ShapeDtypeStruct((r, d), F32),
                 jax.ShapeDtypeStruct((r, widths["u"]), F32),
                 jax.ShapeDtypeStruct((r, a_w), F32),
                 jax.ShapeDtypeStruct((r, a_w), F32),
                 jax.ShapeDtypeStruct((r, a_w), F32),
                 jax.ShapeDtypeStruct((r, widths["z"]), F32),
                 jax.ShapeDtypeStruct((r, widths["xbc"]), F32),
                 jax.ShapeDtypeStruct((r, widths["dt"]), F32)]
    out_specs = [row(d), row(widths["u"]), row(a_w), row(a_w), row(a_w),
                 row(widths["z"]), row(widths["xbc"]), row(widths["dt"])]
    in_specs = [row(d), _const_spec((1, d)), _layer_spec(lw["wg1"], layer), _layer_spec(lw["wu1"], layer),
                _layer_spec(lw["wd1"], layer), _const_spec((1, d)), _layer_spec(lw["w_in"], layer),
                _const_spec((1, a_w)), _const_spec((1, a_w))]
    args = [x, lw["n1"], lw["wg1"], lw["wu1"], lw["wd1"], lw["n_mix"], lw["w_in"], lw["qg"], lw["kg"]]
    aliases = {}
    if for_seq_attn:
        depth = lw["wg1"].shape[0]
        t = r // seq_batch
        assert t % tm == 0
        tiles = t // tm
        kv_spec = pl.BlockSpec((None, None, a_w, tm), lambda i: (layer, i // tiles, 0, i % tiles))
        out_shape[3] = out_shape[4] = jax.ShapeDtypeStruct((depth, seq_batch, a_w, t), F32)
        out_specs[3] = out_specs[4] = kv_spec
        assert all(a.shape == (depth, seq_batch, a_w, t) for a in kv_cache_rows)
        aliases = {len(args): 3, len(args) + 1: 4}
        in_specs += [pl.BlockSpec(memory_space=pl.ANY)] * 2
        args += list(kv_cache_rows)
        nb = tm // MOBA_BLOCK
        vt_rows = (a_w // DH) * VT_HEAD_ROWS
        out_shape += [jax.ShapeDtypeStruct((r, a_w), BF16), jax.ShapeDtypeStruct((vt_rows, r), BF16),
                      jax.ShapeDtypeStruct((r // tm, nb, a_w), F32)]
        out_specs += [row(a_w), pl.BlockSpec((vt_rows, tm), lambda i: (0, i)),
                      pl.BlockSpec((1, nb, a_w), lambda i: (i, 0, 0))]
    fchunk = 256 if d_ff % 256 == 0 else d_ff
    return pl.pallas_call(
        functools.partial(_ffn_proj_kernel, fchunk=fchunk, offs=offs, for_seq_attn=for_seq_attn,
                          n_aliased=len(aliases)),
        grid=(r // tm,),
        in_specs=in_specs,
        out_specs=out_specs,
        out_shape=out_shape,
        input_output_aliases=aliases,
        compiler_params=_params("arbitrary"),
        name="ffn_proj",
    )(*args)


def _out_ffn_kernel(x1_ref, a_ref, b_ref, c_ref, wo_ref, n_ref, wg_ref, wu_ref, wd_ref, fn_ref, y_ref,
                    *, fchunk):
    wa, wb = a_ref.shape[1], b_ref.shape[1]
    mix = (_dot(a_ref[...], wo_ref[0:wa, :]) + _dot(b_ref[...], wo_ref[wa:wa + wb, :])
           + _dot(c_ref[...], wo_ref[wa + wb:, :]))
    x2 = x1_ref[...] + mix
    hb = _rmsnorm(x2, n_ref[...]).astype(BF16)
    x3 = x2 + 0.5 * _swiglu(hb, wg_ref, wu_ref, wd_ref, fchunk)
    y_ref[...] = _rmsnorm(x3, fn_ref[...])


def _out_ffn(x1, conv_out, attn_out, ssm_out, lw, *, tm):
    r, d = x1.shape
    d_ff = lw["wg2"].shape[2]
    layer = lw["layer"]
    row = lambda w: pl.BlockSpec((tm, w), lambda i: (i, 0))
    fchunk = 256 if d_ff % 256 == 0 else d_ff
    return pl.pallas_call(
        functools.partial(_out_ffn_kernel, fchunk=fchunk),
        grid=(r // tm,),
        in_specs=[row(d), row(conv_out.shape[1]), row(attn_out.shape[1]), row(ssm_out.shape[1]),
                  _layer_spec(lw["w_out"], layer), _const_spec((1, d)), _layer_spec(lw["wg2"], layer),
                  _layer_spec(lw["wu2"], layer), _layer_spec(lw["wd2"], layer), _const_spec((1, d))],
        out_specs=row(d),
        out_shape=jax.ShapeDtypeStruct((r, d), F32),
        compiler_params=_params("arbitrary"),
        name="out_ffn",
    )(x1, conv_out, attn_out, ssm_out, lw["w_out"], lw["n2"], lw["wg2"], lw["wu2"], lw["wd2"], lw["n_fin"])


CONV_HALO = 32
CONV_SUB = 128


def _layernorm(c, g, b, eps=1e-5):
    xc = c - jnp.mean(c, axis=-1, keepdims=True)
    var = jnp.mean(xc * xc, axis=-1, keepdims=True)
    return xc * lax.rsqrt(var + eps) * g + b


def _conv_seq_kernel(u_ref, w_ref, b_ref, g_ref, lb_ref, o_ref, buf_ref, shift_ref, *, taps):
    tc = u_ref.shape[1]

    @pl.when(pl.program_id(1) == 0)
    def _():
        buf_ref[0:CONV_HALO, :] = jnp.zeros((CONV_HALO, buf_ref.shape[1]), F32)

    buf_ref[CONV_HALO:CONV_HALO + tc, :] = u_ref[0]
    first = CONV_HALO - (taps - 1)
    residues = sorted({(first + k) % SUBLANES for k in range(taps)})
    for r in residues:
        rows = max(first + k - r for k in range(taps) if (first + k) % SUBLANES == r) + tc
        shift_ref[r, 0:rows, :] = buf_ref[r:r + rows, :]
    for s in range(tc // CONV_SUB):
        acc = None
        for k in range(taps):
            r = (first + k) % SUBLANES
            off = s * CONV_SUB + first + k - r
            term = w_ref[k:k + 1, :] * shift_ref[r, off:off + CONV_SUB, :]
            acc = term if acc is None else acc + term
        c = acc + b_ref[...]
        o_ref[0, s * CONV_SUB:(s + 1) * CONV_SUB, :] = _silu(_layernorm(c, g_ref[...], lb_ref[...])).astype(o_ref.dtype)
    buf_ref[0:CONV_HALO, :] = buf_ref[tc:tc + CONV_HALO, :]


def _conv_seq(u, lw, *, tc=256):
    b, t, c = u.shape
    taps = lw["conv_w"].shape[0]
    assert taps - 1 <= CONV_HALO and t % tc == 0 and tc % CONV_SUB == 0
    return pl.pallas_call(
        functools.partial(_conv_seq_kernel, taps=taps),
        grid=(b, t // tc),
        in_specs=[pl.BlockSpec((1, tc, c), lambda i, j: (i, j, 0)),
                  _const_spec((taps, c)), _const_spec((1, c)), _const_spec((1, c)), _const_spec((1, c))],
        out_specs=pl.BlockSpec((1, tc, c), lambda i, j: (i, j, 0)),
        out_shape=jax.ShapeDtypeStruct((b, t, c), BF16),
        scratch_shapes=[pltpu.VMEM((CONV_HALO + tc, c), F32), pltpu.VMEM((SUBLANES, CONV_HALO + tc, c), F32)],
        compiler_params=_params("arbitrary", "arbitrary"),
        name="conv_seq",
    )(u, lw["conv_w"], lw["conv_b"], lw["ln_g"], lw["ln_b"])


def _rel_bias_tile(relb_ref, head, dist):
    max_exact = NUM_BUCKETS // 2
    n = jnp.maximum(dist, 0)
    nf = jnp.maximum(n, 1).astype(F32)
    large = max_exact + (jnp.log(nf / max_exact) / math.log(REL_MAX_DIST / max_exact)
                         * (NUM_BUCKETS - max_exact)).astype(jnp.int32)
    large = jnp.minimum(large, NUM_BUCKETS - 1)
    bucket = jnp.where(n < max_exact, n, large)
    out = jnp.zeros(dist.shape, F32)
    for b in range(NUM_BUCKETS):
        out = jnp.where(bucket == b, relb_ref[b, head], out)
    return out


def _moba_select_t(gate_t, n_eligible):
    nb = gate_t.shape[0]
    row = lax.broadcasted_iota(jnp.int32, gate_t.shape, 0).astype(F32)
    eligible = row < n_eligible.astype(F32)
    gate_t = jnp.where(eligible, gate_t, NEG_INF)
    sel = jnp.full(gate_t.shape, NEG_INF, F32)
    for _ in range(MOBA_TOPK):
        mx = jnp.max(gate_t, axis=0, keepdims=True)
        idx = jnp.min(jnp.where(gate_t == mx, row, float(nb)), axis=0, keepdims=True)
        pick = row == idx
        sel = jnp.where(pick, jnp.where(eligible, 0.0, sel), sel)
        gate_t = jnp.where(pick, NEG_INF, gate_t)
    return sel


def _attn_seq_kernel(relb_ref, q_ref, km_ref, k_ref, vt_ref, o_ref,
                     bias_own_ref, bias_prev_ref, kmbd_ref, selt_ref, acc_ref, s0_ref, s1_ref, *, n_heads, n_blocks):
    bs = MOBA_BLOCK
    width = n_heads * DH
    cur = pl.program_id(1)

    @pl.when((pl.program_id(0) == 0) & (cur == 0))
    def _():
        ki = lax.broadcasted_iota(jnp.int32, (bs, bs), 0)
        qi = lax.broadcasted_iota(jnp.int32, (bs, bs), 1)
        for h in range(n_heads):
            own = _rel_bias_tile(relb_ref, h, qi - ki) * LOG2E
            bias_own_ref[h] = jnp.where(qi >= ki, own, NEG_INF)
            bias_prev_ref[h] = _rel_bias_tile(relb_ref, h, qi - ki + bs) * LOG2E

    @pl.when(cur == 0)
    def _():
        kmbd_ref[...] = jnp.zeros(kmbd_ref.shape, F32)
        km = km_ref[0]
        km_lane = lax.broadcasted_iota(jnp.int32, km.shape, 1)
        for h in range(n_heads):
            kmbd_ref[h * n_blocks:(h + 1) * n_blocks, :] = jnp.where(
                (km_lane >= h * DH) & (km_lane < (h + 1) * DH), km, 0.0)

    qt = q_ref[0].T
    q_hi, q_mid, q_lo = _split3(qt)
    km_hi, km_mid, km_lo = _split3(kmbd_ref[...])
    gate_t = (_dot(km_hi, q_hi) + _dot(km_mid, q_hi) + _dot(km_hi, q_mid)
              + _dot(km_lo, q_hi) + _dot(km_hi, q_lo) + _dot(km_mid, q_mid))
    for h in range(n_heads):
        rows = slice(h * n_blocks, (h + 1) * n_blocks)
        selt_ref[rows, :] = _moba_select_t(gate_t[rows], cur)
    qs = qt * (LOG2E / math.sqrt(DH))
    row_w = lax.broadcasted_iota(jnp.int32, (width, bs), 0)
    qht = [jnp.where((row_w >= h * DH) & (row_w < (h + 1) * DH), qs, 0.0).astype(BF16) for h in range(n_heads)]
    far_bias = [relb_ref[NUM_BUCKETS - 1, h] * LOG2E for h in range(n_heads)]

    def scores(blk):
        kb = k_ref[0, pl.ds(pl.multiple_of(blk * bs, bs), bs), :]
        return [_dot(kb, qht[h]) for h in range(n_heads)]

    def softmax_pv(blk, s_all, ms, bias):
        start = pl.multiple_of(blk * bs, bs)

        def fold(h, m_new, shift):
            p = jnp.exp2(s_all[h]() + shift)
            alpha = jnp.exp2(ms[h] - m_new)
            vth = vt_ref[h * VT_HEAD_ROWS:(h + 1) * VT_HEAD_ROWS, pl.ds(start, bs)]
            acc_ref[h] = alpha * acc_ref[h] + _dot(vth, p.astype(BF16))

        biases = [bias(h) for h in range(n_heads)]
        m_new = []
        if biases[0].shape[0] == 1:
            for h in range(n_heads):
                m_new.append(jnp.maximum(ms[h], jnp.max(s_all[h](), axis=0, keepdims=True) + biases[h]))
                fold(h, m_new[h], biases[h] - m_new[h])
        else:
            for h in range(n_heads):
                m_new.append(jnp.maximum(ms[h], jnp.max(s_all[h]() + biases[h], axis=0, keepdims=True)))
            for h in range(n_heads):
                fold(h, m_new[h], biases[h] - m_new[h])
        return tuple(m_new)

    def sel_row(h, j):
        return selt_ref[pl.ds(h * n_blocks + j, 1), :]

    n_far = jnp.maximum(cur - 1, 0)
    last_far = jnp.maximum(n_far - 1, 0)
    prev = jnp.maximum(cur - 1, 0)
    s_own, s_prev = scores(cur), scores(prev)

    acc_ref[...] = jnp.zeros(acc_ref.shape, F32)
    ms = (jnp.full((1, bs), NEG_INF, F32),) * n_heads
    ms = softmax_pv(cur, [lambda s=s: s for s in s_own], ms, lambda h: bias_own_ref[h])
    ms = softmax_pv(prev, [lambda s=s: s for s in s_prev], ms,
                    lambda h: bias_prev_ref[h] + jnp.where(cur >= 1, sel_row(h, prev), NEG_INF))

    def put_scores(s_ref, blk):
        for h, s in enumerate(scores(jnp.minimum(blk, last_far))):
            s_ref[h] = s

    def far_bias_row(h, blk):
        row = sel_row(h, jnp.minimum(blk, last_far)) + far_bias[h]
        return jnp.where(blk < n_far, row, NEG_INF)

    def far_pair(i, ms):
        b0, b1 = 2 * i, 2 * i + 1
        put_scores(s1_ref, b1)
        ms = softmax_pv(b0, [lambda h=h: s0_ref[h] for h in range(n_heads)], ms, lambda h: far_bias_row(h, b0))
        put_scores(s0_ref, b1 + 1)
        b1c = jnp.minimum(b1, last_far)
        return softmax_pv(b1c, [lambda h=h: s1_ref[h] for h in range(n_heads)], ms, lambda h: far_bias_row(h, b1))

    put_scores(s0_ref, 0)
    lax.fori_loop(0, (n_far + 1) // 2, far_pair, ms)
    out_t = jnp.concatenate([acc_ref[h, :DH] * (1.0 / acc_ref[h, DH:DH + 1]) for h in range(n_heads)], axis=0)
    o_ref[0] = out_t.T.astype(o_ref.dtype)


def _attn_seq(q, kb, vt, kmean, rel_bias, *, n_heads):
    b, t, w = q.shape
    nb = t // MOBA_BLOCK
    assert vt.shape == (n_heads * VT_HEAD_ROWS, b * t)
    assert t % MOBA_BLOCK == 0 and nb * n_heads <= LANES and nb % SUBLANES == 0
    return pl.pallas_call(
        functools.partial(_attn_seq_kernel, n_heads=n_heads, n_blocks=nb),
        grid=(b, nb),
        in_specs=[pl.BlockSpec(memory_space=pltpu.SMEM),
                  pl.BlockSpec((1, MOBA_BLOCK, w), lambda i, j: (i, j, 0)),
                  pl.BlockSpec((1, nb, w), lambda i, j: (i, 0, 0)),
                  pl.BlockSpec((1, t, w), lambda i, j: (i, 0, 0)),
                  pl.BlockSpec((n_heads * VT_HEAD_ROWS, t), lambda i, j: (0, i))],
        out_specs=pl.BlockSpec((1, MOBA_BLOCK, w), lambda i, j: (i, j, 0)),
        out_shape=jax.ShapeDtypeStruct((b, t, w), BF16),
        scratch_shapes=[pltpu.VMEM((n_heads, MOBA_BLOCK, MOBA_BLOCK), F32),
                        pltpu.VMEM((n_heads, MOBA_BLOCK, MOBA_BLOCK), F32),
                        pltpu.VMEM((LANES, w), F32),
                        pltpu.VMEM((LANES, MOBA_BLOCK), F32),
                        pltpu.VMEM((n_heads, VT_HEAD_ROWS, MOBA_BLOCK), F32),
                        pltpu.VMEM((n_heads, MOBA_BLOCK, MOBA_BLOCK), F32),
                        pltpu.VMEM((n_heads, MOBA_BLOCK, MOBA_BLOCK), F32)],
        compiler_params=_params("arbitrary", "arbitrary"),
        name="attn_seq",
    )(rel_bias, q, kmean, kb, vt)


SSM_HALO = 8


def _softplus(x):
    return jnp.maximum(x, 0.0) + jnp.log1p(jnp.exp(-jnp.abs(x)))


def _ssd_seq_kernel(xbc_ref, z_ref, dt_ref, cw_ref, cb_ref, dtb_ref, alog_ref, dvec_ref, ng_ref,
                    y_ref, hfin_ref, buf_ref, state_ref, *, n_heads):
    lc = SSD_CHUNK
    n = SSM_N
    d_inner = n_heads * SSM_P
    pair_w = 2 * SSM_P
    heads_per_group = n_heads // SSM_G
    chunk = pl.program_id(1)

    @pl.when(chunk == 0)
    def _():
        buf_ref[0:SSM_HALO, :] = jnp.zeros((SSM_HALO, buf_ref.shape[1]), F32)
        state_ref[...] = jnp.zeros(state_ref.shape, F32)

    buf_ref[SSM_HALO:SSM_HALO + lc, :] = xbc_ref[0]
    taps = cw_ref.shape[0]
    first = SSM_HALO - (taps - 1)
    acc = None
    for k in range(taps):
        term = cw_ref[k:k + 1, :] * buf_ref[first + k:first + k + lc, :]
        acc = term if acc is None else acc + term
    xc = _silu(acc + cb_ref[...])
    buf_ref[0:SSM_HALO, :] = buf_ref[lc:lc + SSM_HALO, :]
    xs = xc[:, :d_inner]
    bm = xc[:, d_inner:d_inner + SSM_G * n]
    cm = xc[:, d_inner + SSM_G * n:]

    dt = _softplus(dt_ref[0] + dtb_ref[...])
    head_lane = lax.broadcasted_iota(jnp.int32, (lc, LANES), 1) < n_heads
    dta = jnp.where(head_lane, dt * (-jnp.exp(alog_ref[...])), 0.0)
    ri = lax.broadcasted_iota(jnp.int32, (lc, lc), 0)
    ci = lax.broadcasted_iota(jnp.int32, (lc, lc), 1)
    causal = ri >= ci
    acum = _dot_exact_lhs(jnp.where(causal, 1.0, 0.0).astype(BF16), dta)
    acum2 = acum * LOG2E
    acum2_t = acum2.T
    er = lax.broadcasted_iota(jnp.int32, (LANES, d_inner), 0)
    ec = lax.broadcasted_iota(jnp.int32, (LANES, d_inner), 1) // SSM_P
    expand = jnp.where(er == ec, 1.0, 0.0).astype(BF16)
    last = acum[lc - 1:lc, :]
    dt_x = _dot_exact_rhs(dt, expand)
    grow_x = _dot_exact_rhs(jnp.exp(acum), expand)
    toend_x = _dot_exact_rhs(jnp.exp(last - acum), expand)
    xdt = xs * dt_x
    xdt_b = xdt.astype(BF16)
    xte = xdt * toend_x

    cg_b, bg_b, cb = [], [], []
    for g in range(SSM_G):
        cg_b.append(cm[:, g * n:(g + 1) * n].astype(BF16))
        bg_b.append(bm[:, g * n:(g + 1) * n].astype(BF16))
        cb.append(_dot_t(cg_b[g], bg_b[g]))

    lane_p = lax.broadcasted_iota(jnp.int32, (lc, pair_w), 1)
    row_p = lax.broadcasted_iota(jnp.int32, (pair_w, n), 0)
    y_tiles = []
    for pair in range(n_heads // 2):
        sl = slice(pair * pair_w, (pair + 1) * pair_w)
        g = (2 * pair) // heads_per_group
        x_pair = xdt_b[:, sl]
        st_old = state_ref[sl, :]
        y_inter = _dot_t(cg_b[g], st_old.astype(BF16)) * grow_x[:, sl]
        decays = [jnp.exp(last[:, 2 * pair + half:2 * pair + half + 1]) for half in range(2)]
        state_ref[sl, :] = (st_old * jnp.where(row_p < SSM_P, decays[0], decays[1])
                            + _dot(xte[:, sl].T.astype(BF16), bg_b[g]))
        y_intra = None
        for half in range(2):
            h = 2 * pair + half
            col = acum2[:, h:h + 1]
            row = acum2_t[h:h + 1, :]
            scores = jnp.where(causal, cb[g] * jnp.exp2(col - row), 0.0).astype(BF16)
            in_half = (lane_p >= half * SSM_P) & (lane_p < (half + 1) * SSM_P)
            part = _dot(scores, jnp.where(in_half, x_pair, jnp.zeros_like(x_pair)))
            y_intra = part if y_intra is None else y_intra + part
        y_tiles.append(y_intra + y_inter)
    y = jnp.concatenate(y_tiles, axis=1) + dvec_ref[...] * xs
    gated = y * _silu(z_ref[0])
    y_ref[0] = _rmsnorm(gated, ng_ref[...]).astype(y_ref.dtype)

    @pl.when(chunk == pl.num_programs(1) - 1)
    def _():
        hfin_ref[0] = state_ref[...]


def _ssd_seq(xbc, z, dt, lw, *, n_heads):
    b, t, cd = xbc.shape
    d_inner = z.shape[2]
    lc = SSD_CHUNK
    assert t % lc == 0 and n_heads % (2 * SSM_G) == 0 and n_heads <= LANES
    taps = lw["ssm_cw"].shape[0]
    assert taps - 1 <= SSM_HALO
    tile = lambda w: pl.BlockSpec((1, lc, w), lambda i, j: (i, j, 0))
    return pl.pallas_call(
        functools.partial(_ssd_seq_kernel, n_heads=n_heads),
        grid=(b, t // lc),
        in_specs=[tile(cd), tile(d_inner), tile(LANES),
                  _const_spec((taps, cd)), _const_spec((1, cd)), _const_spec((1, LANES)),
                  _const_spec((1, LANES)), _const_spec((1, d_inner)), _const_spec((1, d_inner))],
        out_specs=[tile(d_inner), pl.BlockSpec((1, d_inner, SSM_N), lambda i, j: (i, 0, 0))],
        out_shape=[jax.ShapeDtypeStruct((b, t, d_inner), BF16),
                   jax.ShapeDtypeStruct((b, d_inner, SSM_N), F32)],
        scratch_shapes=[pltpu.VMEM((SSM_HALO + lc, cd), F32), pltpu.VMEM((d_inner, SSM_N), F32)],
        compiler_params=_params("arbitrary", "arbitrary"),
        name="ssd_seq",
    )(xbc, z, dt, lw["ssm_cw"], lw["ssm_cb"], lw["dt_bias"], lw["a_log"], lw["d_vec"], lw["ssm_ng"])


def _stacked_matrices(w):
    n_in = w["w_in"].shape[2]
    n_pad = n_in - w["ssm_dt_bias"].shape[1] + LANES
    return dict(
        wg1=w["ffn1_wg"].astype(BF16), wu1=w["ffn1_wu"].astype(BF16), wd1=w["ffn1_wd"].astype(BF16),
        w_in=jnp.pad(w["w_in"].astype(BF16), ((0, 0), (0, 0), (0, n_pad - n_in))), w_out=w["w_out"].astype(BF16),
        wg2=w["ffn2_wg"].astype(BF16), wu2=w["ffn2_wu"].astype(BF16), wd2=w["ffn2_wd"].astype(BF16))


def _layer_weights(l, w, stacked):
    c_conv = w["conv_dw_w"].shape[2]
    a_w = (w["w_out"].shape[1] - c_conv - w["ssm_norm_g"].shape[1])
    d_inner = w["ssm_norm_g"].shape[1]
    cd = w["ssm_conv_w"].shape[2]
    n_ssm_heads = w["ssm_dt_bias"].shape[1]
    off_q = 2 * c_conv
    off_k = off_q + a_w
    off_v = off_k + a_w
    off_z = off_v + a_w
    off_xbc = off_z + d_inner
    off_dt = off_xbc + cd
    n_in = off_dt + n_ssm_heads
    assert w["w_in"].shape[2] == n_in
    n_pad = off_dt + LANES
    row = lambda a: a[l].reshape(1, -1).astype(F32)
    pad_row = lambda a: jnp.pad(a[l].astype(F32), (0, LANES - a.shape[1])).reshape(1, LANES)
    assert stacked["w_in"].shape[2] == n_pad
    return dict(
        stacked, layer=l,
        offs=(c_conv, off_q, off_k, off_v, off_z, off_xbc, off_dt, n_pad),
        n1=row(w["ffn1_norm"]), n_mix=row(w["mix_norm"]),
        qg=jnp.tile(w["q_norm_g"][l], a_w // DH).reshape(1, a_w),
        kg=jnp.tile(w["k_norm_g"][l], a_w // DH).reshape(1, a_w),
        conv_w=w["conv_dw_w"][l], conv_b=row(w["conv_dw_b"]), ln_g=row(w["conv_ln_g"]), ln_b=row(w["conv_ln_b"]),
        ssm_cw=w["ssm_conv_w"][l], ssm_cb=row(w["ssm_conv_b"]),
        dt_bias=pad_row(w["ssm_dt_bias"]), a_log=pad_row(w["ssm_a_log"]),
        d_vec=jnp.repeat(w["ssm_d"][l], SSM_P).reshape(1, d_inner), ssm_ng=row(w["ssm_norm_g"]),
        n2=row(w["ffn2_norm"]), n_fin=row(w["final_norm"]),
        n_attn_heads=a_w // DH, n_ssm_heads=n_ssm_heads,
    )


def _row_tile(rows, want=512):
    tm = min(want, rows)
    assert rows % tm == 0 and tm % SUBLANES == 0
    return tm


def _layer_seq(x, lw, rel_bias, kv_cache_rows):
    b, t, d = x.shape
    n_ah, n_sh = lw["n_attn_heads"], lw["n_ssm_heads"]
    tm = _row_tile(b * t)
    assert t % tm == 0 and tm % MOBA_BLOCK == 0
    x1, u, q, kt_rows, vt_rows, z, xbc, dt, kb, vt, km = _ffn_proj(
        x.reshape(b * t, d), lw, tm=tm, seq_batch=b, kv_cache_rows=kv_cache_rows)
    seq = lambda a: a.reshape(b, t, a.shape[-1])
    u, xbc = seq(u), seq(xbc)
    conv_out = _conv_seq(u, lw)
    attn_out = _attn_seq(seq(q), seq(kb), vt, km.reshape(b, t // MOBA_BLOCK, -1), rel_bias, n_heads=n_ah)
    ssm_out, h_fin = _ssd_seq(xbc, seq(z), seq(dt), lw, n_heads=n_sh)
    flat = lambda a: a.reshape(b * t, a.shape[-1])
    y = _out_ffn(x1, flat(conv_out), flat(attn_out), flat(ssm_out), lw, tm=tm)
    conv_taps = lw["conv_w"].shape[0]
    ssm_taps = lw["ssm_cw"].shape[0]
    states = (u[:, t - (conv_taps - 1):], xbc[:, t - (ssm_taps - 1):], h_fin.reshape(b, n_sh, SSM_P, SSM_N))
    return y.reshape(b, t, d), (kt_rows, vt_rows), states


def _conv_step_kernel(u_ref, st_ref, w_ref, b_ref, g_ref, lb_ref, o_ref):
    past = st_ref.shape[0]
    acc = w_ref[past:past + 1, :] * u_ref[...]
    for k in range(past):
        acc = acc + w_ref[k:k + 1, :] * st_ref[k]
    o_ref[...] = _silu(_layernorm(acc + b_ref[...], g_ref[...], lb_ref[...])).astype(o_ref.dtype)


def _conv_step(u, state_t, lw):
    return pl.pallas_call(
        _conv_step_kernel,
        out_shape=jax.ShapeDtypeStruct(u.shape, BF16),
        compiler_params=pltpu.CompilerParams(vmem_limit_bytes=VMEM_LIMIT_BYTES),
        name="conv_step",
    )(u, state_t, lw["conv_w"], lw["conv_b"], lw["ln_g"], lw["ln_b"])


SSD_STEP_ROWS = 8


def _ssd_step_kernel(xn_ref, st_ref, z_ref, dt_ref, h0_ref, cw_ref, cb_ref, dtb_ref, alog_ref, dvec_ref, ng_ref,
                     y_ref, h1_ref, *, n_heads):
    rows = xn_ref.shape[0]
    n = SSM_N
    d_inner = n_heads * SSM_P
    group_rows = (n_heads // SSM_G) * SSM_P
    past = st_ref.shape[0]
    acc = cw_ref[past:past + 1, :] * xn_ref[...]
    for k in range(past):
        acc = acc + cw_ref[k:k + 1, :] * st_ref[k]
    xc = _silu(acc + cb_ref[...])
    xs = xc[:, :d_inner]
    bm = xc[:, d_inner:d_inner + SSM_G * n]
    cm = xc[:, d_inner + SSM_G * n:]
    dt = _softplus(dt_ref[...] + dtb_ref[...])
    dta = dt * (-jnp.exp(alog_ref[...]))
    er = lax.broadcasted_iota(jnp.int32, (LANES, d_inner), 0)
    ec = lax.broadcasted_iota(jnp.int32, (LANES, d_inner), 1) // SSM_P
    expand = jnp.where(er == ec, 1.0, 0.0).astype(BF16)
    xdt = xs * _dot_exact_rhs(dt, expand)
    dec = jnp.exp(_dot_exact_rhs(dta, expand))
    pad = jnp.zeros((LANES - rows, d_inner), F32)
    xdt_t = jnp.concatenate([xdt, pad], axis=0).T
    dec_t = jnp.concatenate([dec, pad], axis=0).T
    lane = lax.broadcasted_iota(jnp.int32, (d_inner, LANES), 1)
    y_t = jnp.zeros((d_inner, LANES), F32)
    for b in range(rows):
        xcol = xdt_t[:, b:b + 1]
        dcol = dec_t[:, b:b + 1]
        ycols = []
        for g in range(SSM_G):
            rs = slice(g * group_rows, (g + 1) * group_rows)
            h0 = h0_ref[b, rs, :]
            brow = bm[b:b + 1, g * n:(g + 1) * n]
            crow = cm[b:b + 1, g * n:(g + 1) * n]
            cb = jnp.sum(crow * brow, axis=1, keepdims=True)
            h1_ref[b, rs, :] = h0 * dcol[rs] + xcol[rs] * brow
            ycols.append(jnp.sum(h0 * crow, axis=1, keepdims=True) * dcol[rs] + cb * xcol[rs])
        y_t = jnp.where(lane == b, jnp.concatenate(ycols, axis=0), y_t)
    y = y_t.T[:rows] + dvec_ref[...] * xs
    gated = y * _silu(z_ref[...])
    y_ref[...] = _rmsnorm(gated, ng_ref[...]).astype(y_ref.dtype)


def _ssd_step(xn, state_t, z, dt, h0_all, layer, lw, *, n_heads):
    db, cd = xn.shape
    d_inner = z.shape[1]
    rows = SSD_STEP_ROWS
    assert db % rows == 0
    past = state_t.shape[0]
    row = lambda w: pl.BlockSpec((rows, w), lambda i: (i, 0))
    return pl.pallas_call(
        functools.partial(_ssd_step_kernel, n_heads=n_heads),
        grid=(db // rows,),
        in_specs=[row(cd), pl.BlockSpec((past, rows, cd), lambda i: (0, i, 0)), row(d_inner), row(LANES),
                  pl.BlockSpec((None, rows, d_inner, SSM_N), lambda i: (layer, i, 0, 0)),
                  _const_spec((past + 1, cd)), _const_spec((1, cd)), _const_spec((1, LANES)),
                  _const_spec((1, LANES)), _const_spec((1, d_inner)), _const_spec((1, d_inner))],
        out_specs=[row(d_inner), pl.BlockSpec((rows, d_inner, SSM_N), lambda i: (i, 0, 0))],
        out_shape=[jax.ShapeDtypeStruct((db, d_inner), BF16), jax.ShapeDtypeStruct(h0_all.shape[1:], F32)],
        compiler_params=_params("arbitrary"),
        name="ssd_step",
    )(xn, state_t, z, dt, h0_all, lw["ssm_cw"], lw["ssm_cb"], lw["dt_bias"], lw["a_log"], lw["d_vec"], lw["ssm_ng"])


GATE_PAGES = 32


def _gate_pages_kernel(pt_ref, q_ref, *refs, n_blocks):
    del pt_ref
    page_refs, idx_ref, kmean_ref = refs[:-2], refs[-2], refs[-1]
    per_block = MOBA_BLOCK // PAGE_SIZE
    blocks = len(page_refs) // per_block
    chunk = pl.program_id(1)
    n_heads = kmean_ref.shape[0]

    @pl.when(chunk == 0)
    def _():
        kmean_ref[...] = jnp.zeros(kmean_ref.shape, F32)

    lane3 = lax.broadcasted_iota(jnp.int32, kmean_ref.shape, 2)
    km = kmean_ref[...]
    for blk in range(blocks):
        s = None
        for r in range(per_block):
            page = page_refs[blk * per_block + r][...]
            s = page if s is None else s + page
        col = jnp.sum(s, axis=2, keepdims=True) * (1.0 / MOBA_BLOCK)
        km = jnp.where(lane3 == chunk * blocks + blk, col, km)
    kmean_ref[...] = km

    @pl.when(chunk == pl.num_programs(1) - 1)
    def _():
        lane = lax.broadcasted_iota(jnp.int32, (SUBLANES, LANES), 1)
        for h in range(n_heads):
            q_hi, q_mid, q_lo = _split3(jnp.broadcast_to(q_ref[0, h], (SUBLANES, DH)))
            k_hi, k_mid, k_lo = _split3(kmean_ref[h])
            gate = (_dot(q_hi, k_hi) + _dot(q_hi, k_mid) + _dot(q_mid, k_hi)
                    + _dot(q_hi, k_lo) + _dot(q_lo, k_hi) + _dot(q_mid, k_mid))
            gate = jnp.where(lane < n_blocks, gate, NEG_INF)
            picks = jnp.zeros((SUBLANES, LANES), jnp.int32)
            for rank in range(MOBA_TOPK):
                mx = jnp.max(gate, axis=1, keepdims=True)
                idx = jnp.min(jnp.where(gate == mx, lane, LANES), axis=1, keepdims=True)
                picks = jnp.where(lane == rank, idx, picks)
                gate = jnp.where(lane == idx, NEG_INF, gate)
            idx_ref[0, h] = picks


def _gate_pages(q4, cache_t, layer, page_table_flat, n_pages):
    db, n_heads = q4.shape[:2]
    per_block = MOBA_BLOCK // PAGE_SIZE
    n_blocks = n_pages // per_block
    pg = min(GATE_PAGES, n_pages)
    assert n_pages % pg == 0 and pg % per_block == 0 and MOBA_TOPK <= n_blocks <= LANES

    def page_spec(j):
        return pl.BlockSpec((None, None, n_heads, DH, PAGE_SIZE),
                            lambda b, i, pt: (layer, pt[b * n_pages + i * pg + j], 0, 0, 0))

    return pl.pallas_call(
        functools.partial(_gate_pages_kernel, n_blocks=n_blocks),
        grid_spec=pltpu.PrefetchScalarGridSpec(
            num_scalar_prefetch=1, grid=(db, n_pages // pg),
            in_specs=[pl.BlockSpec((1, n_heads, 1, DH), lambda b, i, pt: (b, 0, 0, 0))]
            + [page_spec(j) for j in range(pg)],
            out_specs=pl.BlockSpec((1, n_heads, SUBLANES, LANES), lambda b, i, pt: (b, 0, 0, 0)),
            scratch_shapes=[pltpu.VMEM((n_heads, DH, LANES), F32)]),
        out_shape=jax.ShapeDtypeStruct((db, n_heads, SUBLANES, LANES), jnp.int32),
        compiler_params=_params("arbitrary", "arbitrary"),
        name="gate_pages",
    )(page_table_flat, q4, *([cache_t] * pg))


def _attn_step_kernel(pt_ref, sel_ref, relb_ref, q_ref, kn_ref, vn_ref, *refs, n_heads, past_len):
    del pt_ref
    per_block = MOBA_BLOCK // PAGE_SIZE
    n_pg = MOBA_TOPK * per_block
    k_pages, v_pages, o_ref = refs[:n_pg], refs[n_pg:2 * n_pg], refs[2 * n_pg]
    b, h = pl.program_id(0), pl.program_id(1)
    qh = q_ref[...] * (1.0 / math.sqrt(DH))
    q8 = jnp.broadcast_to(qh, (SUBLANES, DH)).astype(BF16)
    key_off = lax.broadcasted_iota(jnp.int32, (1, PAGE_SIZE), 1)
    s_own = jnp.sum(qh * kn_ref[...], axis=1, keepdims=True) + relb_ref[0, h]
    logits = []
    for j in range(MOBA_TOPK):
        blk = sel_ref[(b * n_heads + h) * MOBA_TOPK + j]
        for r in range(per_block):
            dist = past_len - (blk * MOBA_BLOCK + r * PAGE_SIZE + key_off)
            kt = k_pages[j * per_block + r][...].astype(BF16)
            logits.append(_dot(q8, kt)[0:1, :] + _rel_bias_tile(relb_ref, h, dist))
    m = s_own
    for s in logits:
        m = jnp.maximum(m, jnp.max(s, axis=1, keepdims=True))
    p_own = jnp.exp(s_own - m)
    l = p_own
    acc = p_own * vn_ref[...]
    for page, s in enumerate(logits):
        p = jnp.exp(s - m)
        l = l + jnp.sum(p, axis=1, keepdims=True)
        vt = v_pages[page][...].astype(BF16)
        acc = acc + _dot_t(jnp.broadcast_to(p, (SUBLANES, PAGE_SIZE)).astype(BF16), vt)[0:1, :]
    o_ref[...] = jnp.broadcast_to(acc / l, (SUBLANES, DH))


def _attn_step(q4, kn4, vn4, cache_kt, cache_vt, layer, page_table_flat, sel_flat, rel_bias, *, n_pages):
    db, n_heads = q4.shape[:2]
    per_block = MOBA_BLOCK // PAGE_SIZE
    past_len = n_pages * PAGE_SIZE
    assert past_len % MOBA_BLOCK == 0 and past_len // MOBA_BLOCK >= MOBA_TOPK

    def page_spec(j, r):
        def index_map(b, h, pt, sel):
            blk = sel[(b * n_heads + h) * MOBA_TOPK + j]
            return (layer, pt[b * n_pages + blk * per_block + r], h, 0, 0)
        return pl.BlockSpec((None, None, None, DH, PAGE_SIZE), index_map)

    page_specs = [page_spec(j, r) for j in range(MOBA_TOPK) for r in range(per_block)]
    tok = pl.BlockSpec((None, None, 1, DH), lambda b, h, pt, sel: (b, h, 0, 0))
    n_pg = len(page_specs)
    return pl.pallas_call(
        functools.partial(_attn_step_kernel, n_heads=n_heads, past_len=past_len),
        grid_spec=pltpu.PrefetchScalarGridSpec(
            num_scalar_prefetch=2, grid=(db, n_heads),
            in_specs=[pl.BlockSpec(memory_space=pltpu.SMEM), tok, tok, tok] + page_specs + page_specs,
            out_specs=pl.BlockSpec((None, None, SUBLANES, DH), lambda b, h, pt, sel: (b, h, 0, 0))),
        out_shape=jax.ShapeDtypeStruct((db, n_heads, SUBLANES, DH), F32),
        compiler_params=_params("arbitrary", "arbitrary"),
        name="attn_step",
    )(page_table_flat, sel_flat, rel_bias, q4, kn4, vn4, *([cache_kt] * n_pg), *([cache_vt] * n_pg))


def _layer_step(x, conv_state, ssm_conv_state, ssm_states, cache_kt, cache_vt, layer, page_table, lw, rel_bias):
    db, t, d = x.shape
    assert t == 1
    n_ah, n_sh = lw["n_attn_heads"], lw["n_ssm_heads"]
    tm = _row_tile(db)
    x1, u, q, k, v, z, xbc, dt = _ffn_proj(x.reshape(db, d), lw, tm=tm)
    conv_out = _conv_step(u, conv_state.transpose(1, 0, 2), lw)
    n_pages = page_table.shape[1]
    pt_flat = page_table.reshape(-1)
    q4 = q.reshape(db, n_ah, 1, DH)
    picks = _gate_pages(q4, cache_kt, layer, pt_flat, n_pages)
    sel_flat = picks[:, :, 0, :MOBA_TOPK].reshape(-1)
    attn = _attn_step(q4, k.reshape(db, n_ah, 1, DH), v.reshape(db, n_ah, 1, DH), cache_kt, cache_vt, layer,
                      pt_flat, sel_flat, rel_bias, n_pages=n_pages)
    attn_out = attn[:, :, 0, :].reshape(db, n_ah * DH)
    depth = ssm_states.shape[0]
    ssm_out, h_new = _ssd_step(xbc, ssm_conv_state.transpose(1, 0, 2), z, dt,
                               ssm_states.reshape(depth, db, n_sh * SSM_P, SSM_N), layer, lw, n_heads=n_sh)
    y = _out_ffn(x1, conv_out, attn_out.astype(BF16), ssm_out, lw, tm=tm)
    states = (k.reshape(db, 1, n_ah, DH), v.reshape(db, 1, n_ah, DH),
              jnp.concatenate([conv_state[:, 1:], u[:, None, :]], axis=1),
              jnp.concatenate([ssm_conv_state[:, 1:], xbc[:, None, :]], axis=1),
              h_new.reshape(ssm_states.shape[1:]))
    return y.reshape(db, 1, d), states


def kernel(x_prompt, x_sample, cache_k, cache_v, state_conv, state_ssm_conv, state_ssm, page_table, rel_bias, ffn1_norm, ffn1_wg, ffn1_wu, ffn1_wd, mix_norm, w_in, w_out, conv_dw_w, conv_dw_b, conv_ln_g, conv_ln_b, q_norm_g, k_norm_g, ssm_conv_w, ssm_conv_b, ssm_dt_bias, ssm_a_log, ssm_d, ssm_norm_g, ffn2_norm, ffn2_wg, ffn2_wu, ffn2_wd, final_norm):
    w = dict(ffn1_norm=ffn1_norm, ffn1_wg=ffn1_wg, ffn1_wu=ffn1_wu, ffn1_wd=ffn1_wd, mix_norm=mix_norm,
             w_in=w_in, w_out=w_out, conv_dw_w=conv_dw_w, conv_dw_b=conv_dw_b, conv_ln_g=conv_ln_g,
             conv_ln_b=conv_ln_b, q_norm_g=q_norm_g, k_norm_g=k_norm_g, ssm_conv_w=ssm_conv_w,
             ssm_conv_b=ssm_conv_b, ssm_dt_bias=ssm_dt_bias, ssm_a_log=ssm_a_log, ssm_d=ssm_d,
             ssm_norm_g=ssm_norm_g, ffn2_norm=ffn2_norm, ffn2_wg=ffn2_wg, ffn2_wu=ffn2_wu, ffn2_wd=ffn2_wd,
             final_norm=final_norm)
    depth = w_in.shape[0]
    y_p, y_s = x_prompt, x_sample
    st_p, st_s = [], []
    cache_kt = cache_k.transpose(0, 1, 3, 4, 2)
    cache_vt = cache_v.transpose(0, 1, 3, 4, 2)
    stacked = _stacked_matrices(w)
    bp, tp = x_prompt.shape[:2]
    n_ah, a_w = cache_k.shape[3], cache_k.shape[3] * cache_k.shape[4]
    kv_rows = (jnp.zeros((depth, bp, a_w, tp), F32), jnp.zeros((depth, bp, a_w, tp), F32))
    for l in range(depth):
        lw = _layer_weights(l, w, stacked)
        y_p, kv_rows, st = _layer_seq(y_p, lw, rel_bias, kv_rows)
        st_p.append(st)
        y_s, st = _layer_step(y_s, state_conv[l], state_ssm_conv[l], state_ssm, cache_kt, cache_vt, l,
                              page_table, lw, rel_bias)
        st_s.append(st)
    stack = lambda sts, i: jnp.stack([s[i] for s in sts])
    kv_p = tuple(a.reshape(depth, bp, n_ah, DH, tp).transpose(0, 1, 4, 2, 3) for a in kv_rows)
    return ((y_p, y_s) + kv_p + tuple(stack(st_p, i) for i in range(3))
            + tuple(stack(st_s, i) for i in range(5)))
```

```python
import functools
import math

import jax
import jax.numpy as jnp
from jax import lax
from jax.experimental import pallas as pl
from jax.experimental.pallas import tpu as pltpu

F32 = jnp.float32
BF16 = jnp.bfloat16

DH = 64
MOBA_BLOCK = 256
MOBA_TOPK = 3
NUM_BUCKETS = 32
REL_MAX_DIST = 128
SSM_P = 64
SSM_G = 2
SSM_N = 128
SSD_CHUNK = 256
PAGE_SIZE = 128

LANES = 128
SUBLANES = 8
VMEM_LIMIT_BYTES = 56 * 1024 * 1024
BF16_SUBLANES = 16

VT_ONES = BF16_SUBLANES
VT_HEAD_ROWS = DH + VT_ONES
LOG2E = math.log2(math.e)

NEG_INF = float("-inf")


def _params(*semantics):
    return pltpu.CompilerParams(dimension_semantics=semantics, vmem_limit_bytes=VMEM_LIMIT_BYTES)


def _dot(a, b):
    return jnp.dot(a, b, preferred_element_type=F32)


def _dot_t(a, b):
    return lax.dot_general(a, b, (((1,), (1,)), ((), ())), preferred_element_type=F32)


def _split3(x):
    hi = x.astype(BF16)
    r1 = x - hi.astype(F32)
    mid = r1.astype(BF16)
    lo = (r1 - mid.astype(F32)).astype(BF16)
    return hi, mid, lo


def _dot_exact_rhs(x, sel):
    hi, mid, lo = _split3(x)
    return _dot(hi, sel) + _dot(mid, sel) + _dot(lo, sel)


def _dot_exact_lhs(sel, x):
    hi, mid, lo = _split3(x)
    return _dot(sel, hi) + _dot(sel, mid) + _dot(sel, lo)


def _sigmoid(x):
    return 1.0 / (1.0 + jnp.exp(-x))


def _silu(x):
    return x * _sigmoid(x)


def _rmsnorm(x, g, eps=1e-6):
    return x * lax.rsqrt(jnp.mean(x * x, axis=-1, keepdims=True) + eps) * g


def _const_spec(shape):
    nd = len(shape)
    return pl.BlockSpec(shape, lambda *_: (0,) * nd, pipeline_mode=pl.Buffered(1))


def _layer_spec(stacked, layer):
    return pl.BlockSpec((None,) + stacked.shape[1:], lambda *_: (layer, 0, 0), pipeline_mode=pl.Buffered(1))


def _swiglu(hb, wg_ref, wu_ref, wd_ref, fchunk):
    d_ff = wg_ref.shape[1]
    acc = None
    for c in range(d_ff // fchunk):
        sl = slice(c * fchunk, (c + 1) * fchunk)
        g = _dot(hb, wg_ref[:, sl])
        u = _dot(hb, wu_ref[:, sl])
        a = (_silu(g) * u).astype(BF16)
        part = _dot(a, wd_ref[sl, :])
        acc = part if acc is None else acc + part
    return acc


def _head_sumsq_matrix(width):
    r = lax.broadcasted_iota(jnp.int32, (width, width), 0) // DH
    c = lax.broadcasted_iota(jnp.int32, (width, width), 1) // DH
    return jnp.where(r == c, 1.0, 0.0).astype(BF16)


def _head_rmsnorm(p, g_row, eps=1e-6):
    ss = _dot_exact_rhs(p * p, _head_sumsq_matrix(p.shape[1]))
    return p * lax.rsqrt(ss * (1.0 / DH) + eps) * g_row


def _ffn_proj_kernel(x_ref, n1_ref, wg_ref, wu_ref, wd_ref, n2_ref, win_ref, qg_ref, kg_ref, *refs,
                     fchunk, offs, for_seq_attn, n_aliased):
    x1_ref, u_ref, q_ref, k_ref, v_ref, z_ref, xbc_ref, dt_ref, *maybe_attn_refs = refs[n_aliased:]
    c_conv, off_q, off_k, off_v, off_z, off_xbc, off_dt, n_pad = offs
    x = x_ref[...]
    hb = _rmsnorm(x, n1_ref[...]).astype(BF16)
    x1 = x + 0.5 * _swiglu(hb, wg_ref, wu_ref, wd_ref, fchunk)
    x1_ref[...] = x1
    h2 = _rmsnorm(x1, n2_ref[...]).astype(BF16)
    pg = _dot(h2, win_ref[:, 0:off_q])
    u_ref[...] = pg[:, :c_conv] * _sigmoid(pg[:, c_conv:])
    q_ref[...] = _head_rmsnorm(_dot(h2, win_ref[:, off_q:off_k]), qg_ref[...])
    k = _head_rmsnorm(_dot(h2, win_ref[:, off_k:off_v]), kg_ref[...])
    v = _dot(h2, win_ref[:, off_v:off_z])
    z_ref[...] = _dot(h2, win_ref[:, off_z:off_xbc])
    xbc_ref[...] = _dot(h2, win_ref[:, off_xbc:off_dt])
    dt_ref[...] = _dot(h2, win_ref[:, off_dt:n_pad])
    if not for_seq_attn:
        k_ref[...] = k
        v_ref[...] = v
    else:
        v_t = v.T
        k_ref[...] = k.T
        v_ref[...] = v_t
        kb_ref, vt_ref, km_ref = maybe_attn_refs
        kb_ref[...] = k.astype(BF16)
        vt = v_t.astype(BF16)
        ones = jnp.ones((VT_ONES, vt.shape[1]), BF16)
        vt_ref[...] = jnp.concatenate(
            [part for h in range(vt.shape[0] // DH) for part in (vt[h * DH:(h + 1) * DH], ones)], axis=0)
        nb = k.shape[0] // MOBA_BLOCK
        km_ref[0] = jnp.mean(k.reshape(nb, MOBA_BLOCK, k.shape[1]), axis=1)


def _ffn_proj(x, lw, *, tm, seq_batch=None, kv_cache_rows=None):
    for_seq_attn = seq_batch is not None
    r, d = x.shape
    d_ff = lw["wg1"].shape[2]
    layer = lw["layer"]
    offs = lw["offs"]
    c_conv, off_q, off_k, off_v, off_z, off_xbc, off_dt, n_pad = offs
    a_w = off_k - off_q
    widths = dict(u=c_conv, q=a_w, k=a_w, v=a_w, z=off_xbc - off_z, xbc=off_dt - off_xbc, dt=n_pad - off_dt)
    row = lambda w: pl.BlockSpec((tm, w), lambda i: (i, 0))
    out_shape = [jax.ShapeDtypeStruct((r, d), F32),
                 jax.ShapeDtypeStruct((r, widths["u"]), F32),
                 jax.ShapeDtypeStruct((r, a_w), F32),
                 jax.ShapeDtypeStruct((r, a_w), F32),
                 jax.ShapeDtypeStruct((r, a_w), F32),
                 jax.ShapeDtypeStruct((r, widths["z"]), F32),
                 jax.ShapeDtypeStruct((r, widths["xbc"]), F32),
                 jax.ShapeDtypeStruct((r, widths["dt"]), F32)]
    out_specs = [row(d), row(widths["u"]), row(a_w), row(a_w), row(a_w),
                 row(widths["z"]), row(widths["xbc"]), row(widths["dt"])]
    in_specs = [row(d), _const_spec((1, d)), _layer_spec(lw["wg1"], layer), _layer_spec(lw["wu1"], layer),
                _layer_spec(lw["wd1"], layer), _const_spec((1, d)), _layer_spec(lw["w_in"], layer),
                _const_spec((1, a_w)), _const_spec((1, a_w))]
    args = [x, lw["n1"], lw["wg1"], lw["wu1"], lw["wd1"], lw["n_mix"], lw["w_in"], lw["qg"], lw["kg"]]
    aliases = {}
    if for_seq_attn:
        depth = lw["wg1"].shape[0]
        t = r // seq_batch
        assert t % tm == 0
        tiles = t // tm
        kv_spec = pl.BlockSpec((None, None, a_w, tm), lambda i: (layer, i // tiles, 0, i % tiles))
        out_shape[3] = out_shape[4] = jax.ShapeDtypeStruct((depth, seq_batch, a_w, t), F32)
        out_specs[3] = out_specs[4] = kv_spec
        assert all(a.shape == (depth, seq_batch, a_w, t) for a in kv_cache_rows)
        aliases = {len(args): 3, len(args) + 1: 4}
        in_specs += [pl.BlockSpec(memory_space=pl.ANY)] * 2
        args += list(kv_cache_rows)
        nb = tm // MOBA_BLOCK
        vt_rows = (a_w // DH) * VT_HEAD_ROWS
        out_shape += [jax.ShapeDtypeStruct((r, a_w), BF16), jax.ShapeDtypeStruct((vt_rows, r), BF16),
                      jax.ShapeDtypeStruct((r // tm, nb, a_w), F32)]
        out_specs += [row(a_w), pl.BlockSpec((vt_rows, tm), lambda i: (0, i)),
                      pl.BlockSpec((1, nb, a_w), lambda i: (i, 0, 0))]
    fchunk = 256 if d_ff % 256 == 0 else d_ff
    return pl.pallas_call(
        functools.partial(_ffn_proj_kernel, fchunk=fchunk, offs=offs, for_seq_attn=for_seq_attn,
                          n_aliased=len(aliases)),
        grid=(r // tm,),
        in_specs=in_specs,
        out_specs=out_specs,
        out_shape=out_shape,
        input_output_aliases=aliases,
        compiler_params=_params("arbitrary"),
        name="ffn_proj",
    )(*args)


def _out_ffn_kernel(x1_ref, a_ref, b_ref, c_ref, wo_ref, n_ref, wg_ref, wu_ref, wd_ref, fn_ref, y_ref,
                    *, fchunk):
    wa, wb = a_ref.shape[1], b_ref.shape[1]
    mix = (_dot(a_ref[...], wo_ref[0:wa, :]) + _dot(b_ref[...], wo_ref[wa:wa + wb, :])
           + _dot(c_ref[...], wo_ref[wa + wb:, :]))
    x2 = x1_ref[...] + mix
    hb = _rmsnorm(x2, n_ref[...]).astype(BF16)
    x3 = x2 + 0.5 * _swiglu(hb, wg_ref, wu_ref, wd_ref, fchunk)
    y_ref[...] = _rmsnorm(x3, fn_ref[...])


def _out_ffn(x1, conv_out, attn_out, ssm_out, lw, *, tm):
    r, d = x1.shape
    d_ff = lw["wg2"].shape[2]
    layer = lw["layer"]
    row = lambda w: pl.BlockSpec((tm, w), lambda i: (i, 0))
    fchunk = 256 if d_ff % 256 == 0 else d_ff
    return pl.pallas_call(
        functools.partial(_out_ffn_kernel, fchunk=fchunk),
        grid=(r // tm,),
        in_specs=[row(d), row(conv_out.shape[1]), row(attn_out.shape[1]), row(ssm_out.shape[1]),
                  _layer_spec(lw["w_out"], layer), _const_spec((1, d)), _layer_spec(lw["wg2"], layer),
                  _layer_spec(lw["wu2"], layer), _layer_spec(lw["wd2"], layer), _const_spec((1, d))],
        out_specs=row(d),
        out_shape=jax.ShapeDtypeStruct((r, d), F32),
        compiler_params=_params("arbitrary"),
        name="out_ffn",
    )(x1, conv_out, attn_out, ssm_out, lw["w_out"], lw["n2"], lw["wg2"], lw["wu2"], lw["wd2"], lw["n_fin"])


CONV_HALO = 32
CONV_SUB = 128


def _layernorm(c, g, b, eps=1e-5):
    xc = c - jnp.mean(c, axis=-1, keepdims=True)
    var = jnp.mean(xc * xc, axis=-1, keepdims=True)
    return xc * lax.rsqrt(var + eps) * g + b


def _conv_seq_kernel(u_ref, w_ref, b_ref, g_ref, lb_ref, o_ref, buf_ref, shift_ref, *, taps):
    tc = u_ref.shape[1]

    @pl.when(pl.program_id(1) == 0)
    def _():
        buf_ref[0:CONV_HALO, :] = jnp.zeros((CONV_HALO, buf_ref.shape[1]), F32)

    buf_ref[CONV_HALO:CONV_HALO + tc, :] = u_ref[0]
    first = CONV_HALO - (taps - 1)
    residues = sorted({(first + k) % SUBLANES for k in range(taps)})
    for r in residues:
        rows = max(first + k - r for k in range(taps) if (first + k) % SUBLANES == r) + tc
        shift_ref[r, 0:rows, :] = buf_ref[r:r + rows, :]
    for s in range(tc // CONV_SUB):
        acc = None
        for k in range(taps):
            r = (first + k) % SUBLANES
            off = s * CONV_SUB + first + k - r
            term = w_ref[k:k + 1, :] * shift_ref[r, off:off + CONV_SUB, :]
            acc = term if acc is None else acc + term
        c = acc + b_ref[...]
        o_ref[0, s * CONV_SUB:(s + 1) * CONV_SUB, :] = _silu(_layernorm(c, g_ref[...], lb_ref[...])).astype(o_ref.dtype)
    buf_ref[0:CONV_HALO, :] = buf_ref[tc:tc + CONV_HALO, :]


def _conv_seq(u, lw, *, tc=256):
    b, t, c = u.shape
    taps = lw["conv_w"].shape[0]
    assert taps - 1 <= CONV_HALO and t % tc == 0 and tc % CONV_SUB == 0
    return pl.pallas_call(
        functools.partial(_conv_seq_kernel, taps=taps),
        grid=(b, t // tc),
        in_specs=[pl.BlockSpec((1, tc, c), lambda i, j: (i, j, 0)),
                  _const_spec((taps, c)), _const_spec((1, c)), _const_spec((1, c)), _const_spec((1, c))],
        out_specs=pl.BlockSpec((1, tc, c), lambda i, j: (i, j, 0)),
        out_shape=jax.ShapeDtypeStruct((b, t, c), BF16),
        scratch_shapes=[pltpu.VMEM((CONV_HALO + tc, c), F32), pltpu.VMEM((SUBLANES, CONV_HALO + tc, c), F32)],
        compiler_params=_params("arbitrary", "arbitrary"),
        name="conv_seq",
    )(u, lw["conv_w"], lw["conv_b"], lw["ln_g"], lw["ln_b"])


def _rel_bias_tile(relb_ref, head, dist):
    max_exact = NUM_BUCKETS // 2
    n = jnp.maximum(dist, 0)
    nf = jnp.maximum(n, 1).astype(F32)
    large = max_exact + (jnp.log(nf / max_exact) / math.log(REL_MAX_DIST / max_exact)
                         * (NUM_BUCKETS - max_exact)).astype(jnp.int32)
    large = jnp.minimum(large, NUM_BUCKETS - 1)
    bucket = jnp.where(n < max_exact, n, large)
    out = jnp.zeros(dist.shape, F32)
    for b in range(NUM_BUCKETS):
        out = jnp.where(bucket == b, relb_ref[b, head], out)
    return out


def _moba_select_t(gate_t, n_eligible):
    nb = gate_t.shape[0]
    row = lax.broadcasted_iota(jnp.int32, gate_t.shape, 0).astype(F32)
    eligible = row < n_eligible.astype(F32)
    gate_t = jnp.where(eligible, gate_t, NEG_INF)
    sel = jnp.full(gate_t.shape, NEG_INF, F32)
    for _ in range(MOBA_TOPK):
        mx = jnp.max(gate_t, axis=0, keepdims=True)
        idx = jnp.min(jnp.where(gate_t == mx, row, float(nb)), axis=0, keepdims=True)
        pick = row == idx
        sel = jnp.where(pick, jnp.where(eligible, 0.0, sel), sel)
        gate_t = jnp.where(pick, NEG_INF, gate_t)
    return sel


def _attn_seq_kernel(relb_ref, q_ref, km_ref, k_ref, vt_ref, o_ref,
                     bias_own_ref, bias_prev_ref, kmbd_ref, selt_ref, acc_ref, s0_ref, s1_ref, *, n_heads, n_blocks):
    bs = MOBA_BLOCK
    width = n_heads * DH
    cur = pl.program_id(1)

    @pl.when((pl.program_id(0) == 0) & (cur == 0))
    def _():
        ki = lax.broadcasted_iota(jnp.int32, (bs, bs), 0)
        qi = lax.broadcasted_iota(jnp.int32, (bs, bs), 1)
        for h in range(n_heads):
            own = _rel_bias_tile(relb_ref, h, qi - ki) * LOG2E
            bias_own_ref[h] = jnp.where(qi >= ki, own, NEG_INF)
            bias_prev_ref[h] = _rel_bias_tile(relb_ref, h, qi - ki + bs) * LOG2E

    @pl.when(cur == 0)
    def _():
        kmbd_ref[...] = jnp.zeros(kmbd_ref.shape, F32)
        km = km_ref[0]
        km_lane = lax.broadcasted_iota(jnp.int32, km.shape, 1)
        for h in range(n_heads):
            kmbd_ref[h * n_blocks:(h + 1) * n_blocks, :] = jnp.where(
                (km_lane >= h * DH) & (km_lane < (h + 1) * DH), km, 0.0)

    qt = q_ref[0].T
    q_hi, q_mid, q_lo = _split3(qt)
    km_hi, km_mid, km_lo = _split3(kmbd_ref[...])
    gate_t = (_dot(km_hi, q_hi) + _dot(km_mid, q_hi) + _dot(km_hi, q_mid)
              + _dot(km_lo, q_hi) + _dot(km_hi, q_lo) + _dot(km_mid, q_mid))
    for h in range(n_heads):
        rows = slice(h * n_blocks, (h + 1) * n_blocks)
        selt_ref[rows, :] = _moba_select_t(gate_t[rows], cur)
    qs = qt * (LOG2E / math.sqrt(DH))
    row_w = lax.broadcasted_iota(jnp.int32, (width, bs), 0)
    qht = [jnp.where((row_w >= h * DH) & (row_w < (h + 1) * DH), qs, 0.0).astype(BF16) for h in range(n_heads)]
    far_bias = [relb_ref[NUM_BUCKETS - 1, h] * LOG2E for h in range(n_heads)]

    def scores(blk):
        kb = k_ref[0, pl.ds(pl.multiple_of(blk * bs, bs), bs), :]
        return [_dot(kb, qht[h]) for h in range(n_heads)]

    def softmax_pv(blk, s_all, ms, bias):
        start = pl.multiple_of(blk * bs, bs)

        def fold(h, m_new, shift):
            p = jnp.exp2(s_all[h]() + shift)
            alpha = jnp.exp2(ms[h] - m_new)
            vth = vt_ref[h * VT_HEAD_ROWS:(h + 1) * VT_HEAD_ROWS, pl.ds(start, bs)]
            acc_ref[h] = alpha * acc_ref[h] + _dot(vth, p.astype(BF16))

        biases = [bias(h) for h in range(n_heads)]
        m_new = []
        if biases[0].shape[0] == 1:
            for h in range(n_heads):
                m_new.append(jnp.maximum(ms[h], jnp.max(s_all[h](), axis=0, keepdims=True) + biases[h]))
                fold(h, m_new[h], biases[h] - m_new[h])
        else:
            for h in range(n_heads):
                m_new.append(jnp.maximum(ms[h], jnp.max(s_all[h]() + biases[h], axis=0, keepdims=True)))
            for h in range(n_heads):
                fold(h, m_new[h], biases[h] - m_new[h])
        return tuple(m_new)

    def sel_row(h, j):
        return selt_ref[pl.ds(h * n_blocks + j, 1), :]

    n_far = jnp.maximum(cur - 1, 0)
    last_far = jnp.maximum(n_far - 1, 0)
    prev = jnp.maximum(cur - 1, 0)
    s_own, s_prev = scores(cur), scores(prev)

    acc_ref[...] = jnp.zeros(acc_ref.shape, F32)
    ms = (jnp.full((1, bs), NEG_INF, F32),) * n_heads
    ms = softmax_pv(cur, [lambda s=s: s for s in s_own], ms, lambda h: bias_own_ref[h])
    ms = softmax_pv(prev, [lambda s=s: s for s in s_prev], ms,
                    lambda h: bias_prev_ref[h] + jnp.where(cur >= 1, sel_row(h, prev), NEG_INF))

    def put_scores(s_ref, blk):
        for h, s in enumerate(scores(jnp.minimum(blk, last_far))):
            s_ref[h] = s

    def far_bias_row(h, blk):
        row = sel_row(h, jnp.minimum(blk, last_far)) + far_bias[h]
        return jnp.where(blk < n_far, row, NEG_INF)

    def far_pair(b0, ms):
        b1 = b0 + 1
        put_scores(s1_ref, b1)
        ms = softmax_pv(b0, [lambda h=h: s0_ref[h] for h in range(n_heads)], ms, lambda h: far_bias_row(h, b0))
        put_scores(s0_ref, b1 + 1)
        b1c = jnp.minimum(b1, last_far)
        return softmax_pv(b1c, [lambda h=h: s1_ref[h] for h in range(n_heads)], ms, lambda h: far_bias_row(h, b1))

    put_scores(s0_ref, 0)
    quads = n_far // 4
    ms = lax.fori_loop(0, quads, lambda i, ms: far_pair(4 * i + 2, far_pair(4 * i, ms)), ms)
    lax.fori_loop(0, (n_far - 4 * quads + 1) // 2, lambda i, ms: far_pair(4 * quads + 2 * i, ms), ms)
    out_t = jnp.concatenate([acc_ref[h, :DH] * (1.0 / acc_ref[h, DH:DH + 1]) for h in range(n_heads)], axis=0)
    o_ref[0] = out_t.T.astype(o_ref.dtype)


def _attn_seq(q, kb, vt, kmean, rel_bias, *, n_heads):
    b, t, w = q.shape
    nb = t // MOBA_BLOCK
    assert vt.shape == (n_heads * VT_HEAD_ROWS, b * t)
    assert t % MOBA_BLOCK == 0 and nb * n_heads <= LANES and nb % SUBLANES == 0
    return pl.pallas_call(
        functools.partial(_attn_seq_kernel, n_heads=n_heads, n_blocks=nb),
        grid=(b, nb),
        in_specs=[pl.BlockSpec(memory_space=pltpu.SMEM),
                  pl.BlockSpec((1, MOBA_BLOCK, w), lambda i, j: (i, j, 0)),
                  pl.BlockSpec((1, nb, w), lambda i, j: (i, 0, 0)),
                  pl.BlockSpec((1, t, w), lambda i, j: (i, 0, 0)),
                  pl.BlockSpec((n_heads * VT_HEAD_ROWS, t), lambda i, j: (0, i))],
        out_specs=pl.BlockSpec((1, MOBA_BLOCK, w), lambda i, j: (i, j, 0)),
        out_shape=jax.ShapeDtypeStruct((b, t, w), BF16),
        scratch_shapes=[pltpu.VMEM((n_heads, MOBA_BLOCK, MOBA_BLOCK), F32),
                        pltpu.VMEM((n_heads, MOBA_BLOCK, MOBA_BLOCK), F32),
                        pltpu.VMEM((LANES, w), F32),
                        pltpu.VMEM((LANES, MOBA_BLOCK), F32),
                        pltpu.VMEM((n_heads, VT_HEAD_ROWS, MOBA_BLOCK), F32),
                        pltpu.VMEM((n_heads, MOBA_BLOCK, MOBA_BLOCK), F32),
                        pltpu.VMEM((n_heads, MOBA_BLOCK, MOBA_BLOCK), F32)],
        compiler_params=_params("arbitrary", "arbitrary"),
        name="attn_seq",
    )(rel_bias, q, kmean, kb, vt)


SSM_HALO = 8


def _softplus(x):
    return jnp.maximum(x, 0.0) + jnp.log1p(jnp.exp(-jnp.abs(x)))


def _ssd_seq_kernel(xbc_ref, z_ref, dt_ref, cw_ref, cb_ref, dtb_ref, alog_ref, dvec_ref, ng_ref,
                    y_ref, hfin_ref, buf_ref, state_ref, *, n_heads):
    lc = SSD_CHUNK
    n = SSM_N
    d_inner = n_heads * SSM_P
    pair_w = 2 * SSM_P
    heads_per_group = n_heads // SSM_G
    chunk = pl.program_id(1)

    @pl.when(chunk == 0)
    def _():
        buf_ref[0:SSM_HALO, :] = jnp.zeros((SSM_HALO, buf_ref.shape[1]), F32)
        state_ref[...] = jnp.zeros(state_ref.shape, F32)

    buf_ref[SSM_HALO:SSM_HALO + lc, :] = xbc_ref[0]
    taps = cw_ref.shape[0]
    first = SSM_HALO - (taps - 1)
    acc = None
    for k in range(taps):
        term = cw_ref[k:k + 1, :] * buf_ref[first + k:first + k + lc, :]
        acc = term if acc is None else acc + term
    xc = _silu(acc + cb_ref[...])
    buf_ref[0:SSM_HALO, :] = buf_ref[lc:lc + SSM_HALO, :]
    xs = xc[:, :d_inner]
    bm = xc[:, d_inner:d_inner + SSM_G * n]
    cm = xc[:, d_inner + SSM_G * n:]

    dt = _softplus(dt_ref[0] + dtb_ref[...])
    head_lane = lax.broadcasted_iota(jnp.int32, (lc, LANES), 1) < n_heads
    dta = jnp.where(head_lane, dt * (-jnp.exp(alog_ref[...])), 0.0)
    ri = lax.broadcasted_iota(jnp.int32, (lc, lc), 0)
    ci = lax.broadcasted_iota(jnp.int32, (lc, lc), 1)
    causal = ri >= ci
    acum = _dot_exact_lhs(jnp.where(causal, 1.0, 0.0).astype(BF16), dta)
    acum2 = acum * LOG2E
    acum2_t = acum2.T
    er = lax.broadcasted_iota(jnp.int32, (LANES, d_inner), 0)
    ec = lax.broadcasted_iota(jnp.int32, (LANES, d_inner), 1) // SSM_P
    expand = jnp.where(er == ec, 1.0, 0.0).astype(BF16)
    last = acum[lc - 1:lc, :]
    dt_x = _dot_exact_rhs(dt, expand)
    grow_x = _dot_exact_rhs(jnp.exp(acum), expand)
    toend_x = _dot_exact_rhs(jnp.exp(last - acum), expand)
    xdt = xs * dt_x
    xdt_b = xdt.astype(BF16)
    xte = xdt * toend_x

    cg_b, bg_b, cb = [], [], []
    for g in range(SSM_G):
        cg_b.append(cm[:, g * n:(g + 1) * n].astype(BF16))
        bg_b.append(bm[:, g * n:(g + 1) * n].astype(BF16))
        cb.append(_dot_t(cg_b[g], bg_b[g]))

    lane_p = lax.broadcasted_iota(jnp.int32, (lc, pair_w), 1)
    row_p = lax.broadcasted_iota(jnp.int32, (pair_w, n), 0)
    y_tiles = []
    for pair in range(n_heads // 2):
        sl = slice(pair * pair_w, (pair + 1) * pair_w)
        g = (2 * pair) // heads_per_group
        x_pair = xdt_b[:, sl]
        st_old = state_ref[sl, :]
        y_inter = _dot_t(cg_b[g], st_old.astype(BF16)) * grow_x[:, sl]
        decays = [jnp.exp(last[:, 2 * pair + half:2 * pair + half + 1]) for half in range(2)]
        state_ref[sl, :] = (st_old * jnp.where(row_p < SSM_P, decays[0], decays[1])
                            + _dot(xte[:, sl].T.astype(BF16), bg_b[g]))
        y_intra = None
        for half in range(2):
            h = 2 * pair + half
            col = acum2[:, h:h + 1]
            row = acum2_t[h:h + 1, :]
            scores = jnp.where(causal, cb[g] * jnp.exp2(col - row), 0.0).astype(BF16)
            in_half = (lane_p >= half * SSM_P) & (lane_p < (half + 1) * SSM_P)
            part = _dot(scores, jnp.where(in_half, x_pair, jnp.zeros_like(x_pair)))
            y_intra = part if y_intra is None else y_intra + part
        y_tiles.append(y_intra + y_inter)
    y = jnp.concatenate(y_tiles, axis=1) + dvec_ref[...] * xs
    gated = y * _silu(z_ref[0])
    y_ref[0] = _rmsnorm(gated, ng_ref[...]).astype(y_ref.dtype)

    @pl.when(chunk == pl.num_programs(1) - 1)
    def _():
        hfin_ref[0] = state_ref[...]


def _ssd_seq(xbc, z, dt, lw, *, n_heads):
    b, t, cd = xbc.shape
    d_inner = z.shape[2]
    lc = SSD_CHUNK
    assert t % lc == 0 and n_heads % (2 * SSM_G) == 0 and n_heads <= LANES
    taps = lw["ssm_cw"].shape[0]
    assert taps - 1 <= SSM_HALO
    tile = lambda w: pl.BlockSpec((1, lc, w), lambda i, j: (i, j, 0))
    return pl.pallas_call(
        functools.partial(_ssd_seq_kernel, n_heads=n_heads),
        grid=(b, t // lc),
        in_specs=[tile(cd), tile(d_inner), tile(LANES),
                  _const_spec((taps, cd)), _const_spec((1, cd)), _const_spec((1, LANES)),
                  _const_spec((1, LANES)), _const_spec((1, d_inner)), _const_spec((1, d_inner))],
        out_specs=[tile(d_inner), pl.BlockSpec((1, d_inner, SSM_N), lambda i, j: (i, 0, 0))],
        out_shape=[jax.ShapeDtypeStruct((b, t, d_inner), BF16),
                   jax.ShapeDtypeStruct((b, d_inner, SSM_N), F32)],
        scratch_shapes=[pltpu.VMEM((SSM_HALO + lc, cd), F32), pltpu.VMEM((d_inner, SSM_N), F32)],
        compiler_params=_params("arbitrary", "arbitrary"),
        name="ssd_seq",
    )(xbc, z, dt, lw["ssm_cw"], lw["ssm_cb"], lw["dt_bias"], lw["a_log"], lw["d_vec"], lw["ssm_ng"])


def _stacked_matrices(w):
    n_in = w["w_in"].shape[2]
    n_pad = n_in - w["ssm_dt_bias"].shape[1] + LANES
    return dict(
        wg1=w["ffn1_wg"].astype(BF16), wu1=w["ffn1_wu"].astype(BF16), wd1=w["ffn1_wd"].astype(BF16),
        w_in=jnp.pad(w["w_in"].astype(BF16), ((0, 0), (0, 0), (0, n_pad - n_in))), w_out=w["w_out"].astype(BF16),
        wg2=w["ffn2_wg"].astype(BF16), wu2=w["ffn2_wu"].astype(BF16), wd2=w["ffn2_wd"].astype(BF16))


def _layer_weights(l, w, stacked):
    c_conv = w["conv_dw_w"].shape[2]
    a_w = (w["w_out"].shape[1] - c_conv - w["ssm_norm_g"].shape[1])
    d_inner = w["ssm_norm_g"].shape[1]
    cd = w["ssm_conv_w"].shape[2]
    n_ssm_heads = w["ssm_dt_bias"].shape[1]
    off_q = 2 * c_conv
    off_k = off_q + a_w
    off_v = off_k + a_w
    off_z = off_v + a_w
    off_xbc = off_z + d_inner
    off_dt = off_xbc + cd
    n_in = off_dt + n_ssm_heads
    assert w["w_in"].shape[2] == n_in
    n_pad = off_dt + LANES
    row = lambda a: a[l].reshape(1, -1).astype(F32)
    pad_row = lambda a: jnp.pad(a[l].astype(F32), (0, LANES - a.shape[1])).reshape(1, LANES)
    assert stacked["w_in"].shape[2] == n_pad
    return dict(
        stacked, layer=l,
        offs=(c_conv, off_q, off_k, off_v, off_z, off_xbc, off_dt, n_pad),
        n1=row(w["ffn1_norm"]), n_mix=row(w["mix_norm"]),
        qg=jnp.tile(w["q_norm_g"][l], a_w // DH).reshape(1, a_w),
        kg=jnp.tile(w["k_norm_g"][l], a_w // DH).reshape(1, a_w),
        conv_w=w["conv_dw_w"][l], conv_b=row(w["conv_dw_b"]), ln_g=row(w["conv_ln_g"]), ln_b=row(w["conv_ln_b"]),
        ssm_cw=w["ssm_conv_w"][l], ssm_cb=row(w["ssm_conv_b"]),
        dt_bias=pad_row(w["ssm_dt_bias"]), a_log=pad_row(w["ssm_a_log"]),
        d_vec=jnp.repeat(w["ssm_d"][l], SSM_P).reshape(1, d_inner), ssm_ng=row(w["ssm_norm_g"]),
        n2=row(w["ffn2_norm"]), n_fin=row(w["final_norm"]),
        n_attn_heads=a_w // DH, n_ssm_heads=n_ssm_heads,
    )


def _row_tile(rows, want=512):
    tm = min(want, rows)
    assert rows % tm == 0 and tm % SUBLANES == 0
    return tm


def _layer_seq(x, lw, rel_bias, kv_cache_rows):
    b, t, d = x.shape
    n_ah, n_sh = lw["n_attn_heads"], lw["n_ssm_heads"]
    tm = _row_tile(b * t)
    assert t % tm == 0 and tm % MOBA_BLOCK == 0
    x1, u, q, kt_rows, vt_rows, z, xbc, dt, kb, vt, km = _ffn_proj(
        x.reshape(b * t, d), lw, tm=tm, seq_batch=b, kv_cache_rows=kv_cache_rows)
    seq = lambda a: a.reshape(b, t, a.shape[-1])
    u, xbc = seq(u), seq(xbc)
    conv_out = _conv_seq(u, lw)
    attn_out = _attn_seq(seq(q), seq(kb), vt, km.reshape(b, t // MOBA_BLOCK, -1), rel_bias, n_heads=n_ah)
    ssm_out, h_fin = _ssd_seq(xbc, seq(z), seq(dt), lw, n_heads=n_sh)
    flat = lambda a: a.reshape(b * t, a.shape[-1])
    y = _out_ffn(x1, flat(conv_out), flat(attn_out), flat(ssm_out), lw, tm=tm)
    conv_taps = lw["conv_w"].shape[0]
    ssm_taps = lw["ssm_cw"].shape[0]
    states = (u[:, t - (conv_taps - 1):], xbc[:, t - (ssm_taps - 1):], h_fin.reshape(b, n_sh, SSM_P, SSM_N))
    return y.reshape(b, t, d), (kt_rows, vt_rows), states


def _conv_step_kernel(u_ref, st_ref, w_ref, b_ref, g_ref, lb_ref, o_ref):
    past = st_ref.shape[0]
    acc = w_ref[past:past + 1, :] * u_ref[...]
    for k in range(past):
        acc = acc + w_ref[k:k + 1, :] * st_ref[k]
    o_ref[...] = _silu(_layernorm(acc + b_ref[...], g_ref[...], lb_ref[...])).astype(o_ref.dtype)


def _conv_step(u, state_t, lw):
    return pl.pallas_call(
        _conv_step_kernel,
        out_shape=jax.ShapeDtypeStruct(u.shape, BF16),
        compiler_params=pltpu.CompilerParams(vmem_limit_bytes=VMEM_LIMIT_BYTES),
        name="conv_step",
    )(u, state_t, lw["conv_w"], lw["conv_b"], lw["ln_g"], lw["ln_b"])


SSD_STEP_ROWS = 8


def _ssd_step_kernel(xn_ref, st_ref, z_ref, dt_ref, h0_ref, cw_ref, cb_ref, dtb_ref, alog_ref, dvec_ref, ng_ref,
                     y_ref, h1_ref, *, n_heads):
    rows = xn_ref.shape[0]
    n = SSM_N
    d_inner = n_heads * SSM_P
    group_rows = (n_heads // SSM_G) * SSM_P
    past = st_ref.shape[0]
    acc = cw_ref[past:past + 1, :] * xn_ref[...]
    for k in range(past):
        acc = acc + cw_ref[k:k + 1, :] * st_ref[k]
    xc = _silu(acc + cb_ref[...])
    xs = xc[:, :d_inner]
    bm = xc[:, d_inner:d_inner + SSM_G * n]
    cm = xc[:, d_inner + SSM_G * n:]
    dt = _softplus(dt_ref[...] + dtb_ref[...])
    dta = dt * (-jnp.exp(alog_ref[...]))
    er = lax.broadcasted_iota(jnp.int32, (LANES, d_inner), 0)
    ec = lax.broadcasted_iota(jnp.int32, (LANES, d_inner), 1) // SSM_P
    expand = jnp.where(er == ec, 1.0, 0.0).astype(BF16)
    xdt = xs * _dot_exact_rhs(dt, expand)
    dec = jnp.exp(_dot_exact_rhs(dta, expand))
    pad = jnp.zeros((LANES - rows, d_inner), F32)
    xdt_t = jnp.concatenate([xdt, pad], axis=0).T
    dec_t = jnp.concatenate([dec, pad], axis=0).T
    lane = lax.broadcasted_iota(jnp.int32, (d_inner, LANES), 1)
    y_t = jnp.zeros((d_inner, LANES), F32)
    for b in range(rows):
        xcol = xdt_t[:, b:b + 1]
        dcol = dec_t[:, b:b + 1]
        ycols = []
        for g in range(SSM_G):
            rs = slice(g * group_rows, (g + 1) * group_rows)
            h0 = h0_ref[b, rs, :]
            brow = bm[b:b + 1, g * n:(g + 1) * n]
            crow = cm[b:b + 1, g * n:(g + 1) * n]
            cb = jnp.sum(crow * brow, axis=1, keepdims=True)
            h1_ref[b, rs, :] = h0 * dcol[rs] + xcol[rs] * brow
            ycols.append(jnp.sum(h0 * crow, axis=1, keepdims=True) * dcol[rs] + cb * xcol[rs])
        y_t = jnp.where(lane == b, jnp.concatenate(ycols, axis=0), y_t)
    y = y_t.T[:rows] + dvec_ref[...] * xs
    gated = y * _silu(z_ref[...])
    y_ref[...] = _rmsnorm(gated, ng_ref[...]).astype(y_ref.dtype)


def _ssd_step(xn, state_t, z, dt, h0_all, layer, lw, *, n_heads):
    db, cd = xn.shape
    d_inner = z.shape[1]
    rows = SSD_STEP_ROWS
    assert db % rows == 0
    past = state_t.shape[0]
    row = lambda w: pl.BlockSpec((rows, w), lambda i: (i, 0))
    return pl.pallas_call(
        functools.partial(_ssd_step_kernel, n_heads=n_heads),
        grid=(db // rows,),
        in_specs=[row(cd), pl.BlockSpec((past, rows, cd), lambda i: (0, i, 0)), row(d_inner), row(LANES),
                  pl.BlockSpec((None, rows, d_inner, SSM_N), lambda i: (layer, i, 0, 0)),
                  _const_spec((past + 1, cd)), _const_spec((1, cd)), _const_spec((1, LANES)),
                  _const_spec((1, LANES)), _const_spec((1, d_inner)), _const_spec((1, d_inner))],
        out_specs=[row(d_inner), pl.BlockSpec((rows, d_inner, SSM_N), lambda i: (i, 0, 0))],
        out_shape=[jax.ShapeDtypeStruct((db, d_inner), BF16), jax.ShapeDtypeStruct(h0_all.shape[1:], F32)],
        compiler_params=_params("arbitrary"),
        name="ssd_step",
    )(xn, state_t, z, dt, h0_all, lw["ssm_cw"], lw["ssm_cb"], lw["dt_bias"], lw["a_log"], lw["d_vec"], lw["ssm_ng"])


GATE_PAGES = 32


def _gate_pages_kernel(pt_ref, q_ref, *refs, n_blocks):
    del pt_ref
    page_refs, idx_ref, kmean_ref = refs[:-2], refs[-2], refs[-1]
    per_block = MOBA_BLOCK // PAGE_SIZE
    blocks = len(page_refs) // per_block
    chunk = pl.program_id(1)
    n_heads = kmean_ref.shape[0]

    @pl.when(chunk == 0)
    def _():
        kmean_ref[...] = jnp.zeros(kmean_ref.shape, F32)

    lane3 = lax.broadcasted_iota(jnp.int32, kmean_ref.shape, 2)
    km = kmean_ref[...]
    for blk in range(blocks):
        s = None
        for r in range(per_block):
            page = page_refs[blk * per_block + r][...]
            s = page if s is None else s + page
        col = jnp.sum(s, axis=2, keepdims=True) * (1.0 / MOBA_BLOCK)
        km = jnp.where(lane3 == chunk * blocks + blk, col, km)
    kmean_ref[...] = km

    @pl.when(chunk == pl.num_programs(1) - 1)
    def _():
        lane = lax.broadcasted_iota(jnp.int32, (SUBLANES, LANES), 1)
        for h in range(n_heads):
            q_hi, q_mid, q_lo = _split3(jnp.broadcast_to(q_ref[0, h], (SUBLANES, DH)))
            k_hi, k_mid, k_lo = _split3(kmean_ref[h])
            gate = (_dot(q_hi, k_hi) + _dot(q_hi, k_mid) + _dot(q_mid, k_hi)
                    + _dot(q_hi, k_lo) + _dot(q_lo, k_hi) + _dot(q_mid, k_mid))
            gate = jnp.where(lane < n_blocks, gate, NEG_INF)
            picks = jnp.zeros((SUBLANES, LANES), jnp.int32)
            for rank in range(MOBA_TOPK):
                mx = jnp.max(gate, axis=1, keepdims=True)
                idx = jnp.min(jnp.where(gate == mx, lane, LANES), axis=1, keepdims=True)
                picks = jnp.where(lane == rank, idx, picks)
                gate = jnp.where(lane == idx, NEG_INF, gate)
            idx_ref[0, h] = picks


def _gate_pages(q4, cache_t, layer, page_table_flat, n_pages):
    db, n_heads = q4.shape[:2]
    per_block = MOBA_BLOCK // PAGE_SIZE
    n_blocks = n_pages // per_block
    pg = min(GATE_PAGES, n_pages)
    assert n_pages % pg == 0 and pg % per_block == 0 and MOBA_TOPK <= n_blocks <= LANES

    def page_spec(j):
        return pl.BlockSpec((None, None, n_heads, DH, PAGE_SIZE),
                            lambda b, i, pt: (layer, pt[b * n_pages + i * pg + j], 0, 0, 0))

    return pl.pallas_call(
        functools.partial(_gate_pages_kernel, n_blocks=n_blocks),
        grid_spec=pltpu.PrefetchScalarGridSpec(
            num_scalar_prefetch=1, grid=(db, n_pages // pg),
            in_specs=[pl.BlockSpec((1, n_heads, 1, DH), lambda b, i, pt: (b, 0, 0, 0))]
            + [page_spec(j) for j in range(pg)],
            out_specs=pl.BlockSpec((1, n_heads, SUBLANES, LANES), lambda b, i, pt: (b, 0, 0, 0)),
            scratch_shapes=[pltpu.VMEM((n_heads, DH, LANES), F32)]),
        out_shape=jax.ShapeDtypeStruct((db, n_heads, SUBLANES, LANES), jnp.int32),
        compiler_params=_params("arbitrary", "arbitrary"),
        name="gate_pages",
    )(page_table_flat, q4, *([cache_t] * pg))


def _attn_step_kernel(pt_ref, sel_ref, relb_ref, q_ref, kn_ref, vn_ref, *refs, n_heads, past_len):
    del pt_ref
    per_block = MOBA_BLOCK // PAGE_SIZE
    n_pg = MOBA_TOPK * per_block
    k_pages, v_pages, o_ref = refs[:n_pg], refs[n_pg:2 * n_pg], refs[2 * n_pg]
    b, h = pl.program_id(0), pl.program_id(1)
    qh = q_ref[...] * (1.0 / math.sqrt(DH))
    q8 = jnp.broadcast_to(qh, (SUBLANES, DH)).astype(BF16)
    key_off = lax.broadcasted_iota(jnp.int32, (1, PAGE_SIZE), 1)
    s_own = jnp.sum(qh * kn_ref[...], axis=1, keepdims=True) + relb_ref[0, h]
    logits = []
    for j in range(MOBA_TOPK):
        blk = sel_ref[(b * n_heads + h) * MOBA_TOPK + j]
        for r in range(per_block):
            dist = past_len - (blk * MOBA_BLOCK + r * PAGE_SIZE + key_off)
            kt = k_pages[j * per_block + r][...].astype(BF16)
            logits.append(_dot(q8, kt)[0:1, :] + _rel_bias_tile(relb_ref, h, dist))
    m = s_own
    for s in logits:
        m = jnp.maximum(m, jnp.max(s, axis=1, keepdims=True))
    p_own = jnp.exp(s_own - m)
    l = p_own
    acc = p_own * vn_ref[...]
    for page, s in enumerate(logits):
        p = jnp.exp(s - m)
        l = l + jnp.sum(p, axis=1, keepdims=True)
        vt = v_pages[page][...].astype(BF16)
        acc = acc + _dot_t(jnp.broadcast_to(p, (SUBLANES, PAGE_SIZE)).astype(BF16), vt)[0:1, :]
    o_ref[...] = jnp.broadcast_to(acc / l, (SUBLANES, DH))


def _attn_step(q4, kn4, vn4, cache_kt, cache_vt, layer, page_table_flat, sel_flat, rel_bias, *, n_pages):
    db, n_heads = q4.shape[:2]
    per_block = MOBA_BLOCK // PAGE_SIZE
    past_len = n_pages * PAGE_SIZE
    assert past_len % MOBA_BLOCK == 0 and past_len // MOBA_BLOCK >= MOBA_TOPK

    def page_spec(j, r):
        def index_map(b, h, pt, sel):
            blk = sel[(b * n_heads + h) * MOBA_TOPK + j]
            return (layer, pt[b * n_pages + blk * per_block + r], h, 0, 0)
        return pl.BlockSpec((None, None, None, DH, PAGE_SIZE), index_map)

    page_specs = [page_spec(j, r) for j in range(MOBA_TOPK) for r in range(per_block)]
    tok = pl.BlockSpec((None, None, 1, DH), lambda b, h, pt, sel: (b, h, 0, 0))
    n_pg = len(page_specs)
    return pl.pallas_call(
        functools.partial(_attn_step_kernel, n_heads=n_heads, past_len=past_len),
        grid_spec=pltpu.PrefetchScalarGridSpec(
            num_scalar_prefetch=2, grid=(db, n_heads),
            in_specs=[pl.BlockSpec(memory_space=pltpu.SMEM), tok, tok, tok] + page_specs + page_specs,
            out_specs=pl.BlockSpec((None, None, SUBLANES, DH), lambda b, h, pt, sel: (b, h, 0, 0))),
        out_shape=jax.ShapeDtypeStruct((db, n_heads, SUBLANES, DH), F32),
        compiler_params=_params("arbitrary", "arbitrary"),
        name="attn_step",
    )(page_table_flat, sel_flat, rel_bias, q4, kn4, vn4, *([cache_kt] * n_pg), *([cache_vt] * n_pg))


def _layer_step(x, conv_state, ssm_conv_state, ssm_states, cache_kt, cache_vt, layer, page_table, lw, rel_bias):
    db, t, d = x.shape
    assert t == 1
    n_ah, n_sh = lw["n_attn_heads"], lw["n_ssm_heads"]
    tm = _row_tile(db)
    x1, u, q, k, v, z, xbc, dt = _ffn_proj(x.reshape(db, d), lw, tm=tm)
    conv_out = _conv_step(u, conv_state.transpose(1, 0, 2), lw)
    n_pages = page_table.shape[1]
    pt_flat = page_table.reshape(-1)
    q4 = q.reshape(db, n_ah, 1, DH)
    picks = _gate_pages(q4, cache_kt, layer, pt_flat, n_pages)
    sel_flat = picks[:, :, 0, :MOBA_TOPK].reshape(-1)
    attn = _attn_step(q4, k.reshape(db, n_ah, 1, DH), v.reshape(db, n_ah, 1, DH), cache_kt, cache_vt, layer,
                      pt_flat, sel_flat, rel_bias, n_pages=n_pages)
    attn_out = attn[:, :, 0, :].reshape(db, n_ah * DH)
    depth = ssm_states.shape[0]
    ssm_out, h_new = _ssd_step(xbc, ssm_conv_state.transpose(1, 0, 2), z, dt,
                               ssm_states.reshape(depth, db, n_sh * SSM_P, SSM_N), layer, lw, n_heads=n_sh)
    y = _out_ffn(x1, conv_out, attn_out.astype(BF16), ssm_out, lw, tm=tm)
    states = (k.reshape(db, 1, n_ah, DH), v.reshape(db, 1, n_ah, DH),
              jnp.concatenate([conv_state[:, 1:], u[:, None, :]], axis=1),
              jnp.concatenate([ssm_conv_state[:, 1:], xbc[:, None, :]], axis=1),
              h_new.reshape(ssm_states.shape[1:]))
    return y.reshape(db, 1, d), states


def kernel(x_prompt, x_sample, cache_k, cache_v, state_conv, state_ssm_conv, state_ssm, page_table, rel_bias, ffn1_norm, ffn1_wg, ffn1_wu, ffn1_wd, mix_norm, w_in, w_out, conv_dw_w, conv_dw_b, conv_ln_g, conv_ln_b, q_norm_g, k_norm_g, ssm_conv_w, ssm_conv_b, ssm_dt_bias, ssm_a_log, ssm_d, ssm_norm_g, ffn2_norm, ffn2_wg, ffn2_wu, ffn2_wd, final_norm):
    w = dict(ffn1_norm=ffn1_norm, ffn1_wg=ffn1_wg, ffn1_wu=ffn1_wu, ffn1_wd=ffn1_wd, mix_norm=mix_norm,
             w_in=w_in, w_out=w_out, conv_dw_w=conv_dw_w, conv_dw_b=conv_dw_b, conv_ln_g=conv_ln_g,
             conv_ln_b=conv_ln_b, q_norm_g=q_norm_g, k_norm_g=k_norm_g, ssm_conv_w=ssm_conv_w,
             ssm_conv_b=ssm_conv_b, ssm_dt_bias=ssm_dt_bias, ssm_a_log=ssm_a_log, ssm_d=ssm_d,
             ssm_norm_g=ssm_norm_g, ffn2_norm=ffn2_norm, ffn2_wg=ffn2_wg, ffn2_wu=ffn2_wu, ffn2_wd=ffn2_wd,
             final_norm=final_norm)
    depth = w_in.shape[0]
    y_p, y_s = x_prompt, x_sample
    st_p, st_s = [], []
    cache_kt = cache_k.transpose(0, 1, 3, 4, 2)
    cache_vt = cache_v.transpose(0, 1, 3, 4, 2)
    stacked = _stacked_matrices(w)
    bp, tp = x_prompt.shape[:2]
    n_ah, a_w = cache_k.shape[3], cache_k.shape[3] * cache_k.shape[4]
    kv_rows = (jnp.zeros((depth, bp, a_w, tp), F32), jnp.zeros((depth, bp, a_w, tp), F32))
    for l in range(depth):
        lw = _layer_weights(l, w, stacked)
        y_p, kv_rows, st = _layer_seq(y_p, lw, rel_bias, kv_rows)
        st_p.append(st)
        y_s, st = _layer_step(y_s, state_conv[l], state_ssm_conv[l], state_ssm, cache_kt, cache_vt, l,
                              page_table, lw, rel_bias)
        st_s.append(st)
    stack = lambda sts, i: jnp.stack([s[i] for s in sts])
    kv_p = tuple(a.reshape(depth, bp, n_ah, DH, tp).transpose(0, 1, 4, 2, 3) for a in kv_rows)
    return ((y_p, y_s) + kv_p + tuple(stack(st_p, i) for i in range(3))
            + tuple(stack(st_s, i) for i in range(5)))
```

```python
import functools
import math

import jax
import jax.numpy as jnp
from jax import lax
from jax.experimental import pallas as pl
from jax.experimental.pallas import tpu as pltpu

F32 = jnp.float32
BF16 = jnp.bfloat16

DH = 64
MOBA_BLOCK = 256
MOBA_TOPK = 3
NUM_BUCKETS = 32
REL_MAX_DIST = 128
SSM_P = 64
SSM_G = 2
SSM_N = 128
SSD_CHUNK = 256
PAGE_SIZE = 128

LANES = 128
SUBLANES = 8
VMEM_LIMIT_BYTES = 56 * 1024 * 1024
BF16_SUBLANES = 16

VT_ONES = BF16_SUBLANES
VT_HEAD_ROWS = DH + VT_ONES
LOG2E = math.log2(math.e)

NEG_INF = float("-inf")


def _params(*semantics):
    return pltpu.CompilerParams(dimension_semantics=semantics, vmem_limit_bytes=VMEM_LIMIT_BYTES)


def _dot(a, b):
    return jnp.dot(a, b, preferred_element_type=F32)


def _dot_t(a, b):
    return lax.dot_general(a, b, (((1,), (1,)), ((), ())), preferred_element_type=F32)


def _split3(x):
    hi = x.astype(BF16)
    r1 = x - hi.astype(F32)
    mid = r1.astype(BF16)
    lo = (r1 - mid.astype(F32)).astype(BF16)
    return hi, mid, lo


def _dot_exact_rhs(x, sel, terms=3):
    hi, mid, lo = _split3(x)
    if terms == 2:
        return _dot(hi, sel) + _dot(mid, sel)
    return _dot(hi, sel) + _dot(mid, sel) + _dot(lo, sel)


def _dot_exact_lhs(sel, x):
    hi, mid, lo = _split3(x)
    return _dot(sel, hi) + _dot(sel, mid) + _dot(sel, lo)


def _sigmoid(x):
    return 1.0 / (1.0 + jnp.exp(-x))


def _silu(x):
    return x * _sigmoid(x)


def _rmsnorm(x, g, eps=1e-6):
    return x * lax.rsqrt(jnp.mean(x * x, axis=-1, keepdims=True) + eps) * g


def _const_spec(shape):
    nd = len(shape)
    return pl.BlockSpec(shape, lambda *_: (0,) * nd, pipeline_mode=pl.Buffered(1))


def _layer_spec(stacked, layer):
    return pl.BlockSpec((None,) + stacked.shape[1:], lambda *_: (layer, 0, 0), pipeline_mode=pl.Buffered(1))


def _swiglu(hb, wg_ref, wu_ref, wd_ref, fchunk):
    d_ff = wg_ref.shape[1]
    acc = None
    for c in range(d_ff // fchunk):
        sl = slice(c * fchunk, (c + 1) * fchunk)
        g = _dot(hb, wg_ref[:, sl])
        u = _dot(hb, wu_ref[:, sl])
        a = (_silu(g) * u).astype(BF16)
        part = _dot(a, wd_ref[sl, :])
        acc = part if acc is None else acc + part
    return acc


def _head_sumsq_matrix(width):
    r = lax.broadcasted_iota(jnp.int32, (width, width), 0) // DH
    c = lax.broadcasted_iota(jnp.int32, (width, width), 1) // DH
    return jnp.where(r == c, 1.0, 0.0).astype(BF16)


def _head_rmsnorm(p, g_row, eps=1e-6):
    ss = _dot_exact_rhs(p * p, _head_sumsq_matrix(p.shape[1]))
    return p * lax.rsqrt(ss * (1.0 / DH) + eps) * g_row


def _ffn_proj_kernel(x_ref, n1_ref, wg_ref, wu_ref, wd_ref, n2_ref, win_ref, qg_ref, kg_ref, *refs,
                     fchunk, offs, for_seq_attn, n_aliased):
    x1_ref, u_ref, q_ref, k_ref, v_ref, z_ref, xbc_ref, dt_ref, *maybe_attn_refs = refs[n_aliased:]
    c_conv, off_q, off_k, off_v, off_z, off_xbc, off_dt, n_pad = offs
    x = x_ref[...]
    hb = _rmsnorm(x, n1_ref[...]).astype(BF16)
    x1 = x + 0.5 * _swiglu(hb, wg_ref, wu_ref, wd_ref, fchunk)
    x1_ref[...] = x1
    h2 = _rmsnorm(x1, n2_ref[...]).astype(BF16)
    pg = _dot(h2, win_ref[:, 0:off_q])
    u_ref[...] = pg[:, :c_conv] * _sigmoid(pg[:, c_conv:])
    q_ref[...] = _head_rmsnorm(_dot(h2, win_ref[:, off_q:off_k]), qg_ref[...])
    k = _head_rmsnorm(_dot(h2, win_ref[:, off_k:off_v]), kg_ref[...])
    v = _dot(h2, win_ref[:, off_v:off_z])
    z_ref[...] = _dot(h2, win_ref[:, off_z:off_xbc])
    xbc_ref[...] = _dot(h2, win_ref[:, off_xbc:off_dt])
    dt_ref[...] = _dot(h2, win_ref[:, off_dt:n_pad])
    if not for_seq_attn:
        k_ref[...] = k
        v_ref[...] = v
    else:
        v_t = v.T
        k_ref[...] = k.T
        v_ref[...] = v_t
        kb_ref, vt_ref, km_ref = maybe_attn_refs
        kb_ref[...] = k.astype(BF16)
        vt = v_t.astype(BF16)
        ones = jnp.ones((VT_ONES, vt.shape[1]), BF16)
        vt_ref[...] = jnp.concatenate(
            [part for h in range(vt.shape[0] // DH) for part in (vt[h * DH:(h + 1) * DH], ones)], axis=0)
        nb = k.shape[0] // MOBA_BLOCK
        km_ref[0] = jnp.mean(k.reshape(nb, MOBA_BLOCK, k.shape[1]), axis=1)


def _ffn_proj(x, lw, *, tm, seq_batch=None, kv_cache_rows=None):
    for_seq_attn = seq_batch is not None
    r, d = x.shape
    d_ff = lw["wg1"].shape[2]
    layer = lw["layer"]
    offs = lw["offs"]
    c_conv, off_q, off_k, off_v, off_z, off_xbc, off_dt, n_pad = offs
    a_w = off_k - off_q
    widths = dict(u=c_conv, q=a_w, k=a_w, v=a_w, z=off_xbc - off_z, xbc=off_dt - off_xbc, dt=n_pad - off_dt)
    row = lambda w: pl.BlockSpec((tm, w), lambda i: (i, 0))
    out_shape = [jax.ShapeDtypeStruct((r, d), F32),
                 jax.ShapeDtypeStruct((r, widths["u"]), F32),
                 jax.ShapeDtypeStruct((r, a_w), F32),
                 jax.ShapeDtypeStruct((r, a_w), F32),
                 jax.ShapeDtypeStruct((r, a_w), F32),
                 jax.ShapeDtypeStruct((r, widths["z"]), F32),
                 jax.ShapeDtypeStruct((r, widths["xbc"]), F32),
                 jax.ShapeDtypeStruct((r, widths["dt"]), F32)]
    out_specs = [row(d), row(widths["u"]), row(a_w), row(a_w), row(a_w),
                 row(widths["z"]), row(widths["xbc"]), row(widths["dt"])]
    in_specs = [row(d), _const_spec((1, d)), _layer_spec(lw["wg1"], layer), _layer_spec(lw["wu1"], layer),
                _layer_spec(lw["wd1"], layer), _const_spec((1, d)), _layer_spec(lw["w_in"], layer),
                _const_spec((1, a_w)), _const_spec((1, a_w))]
    args = [x, lw["n1"], lw["wg1"], lw["wu1"], lw["wd1"], lw["n_mix"], lw["w_in"], lw["qg"], lw["kg"]]
    aliases = {}
    if for_seq_attn:
        depth = lw["wg1"].shape[0]
        t = r // seq_batch
        assert t % tm == 0
        tiles = t // tm
        kv_spec = pl.BlockSpec((None, None, a_w, tm), lambda i: (layer, i // tiles, 0, i % tiles))
        out_shape[3] = out_shape[4] = jax.ShapeDtypeStruct((depth, seq_batch, a_w, t), F32)
        out_specs[3] = out_specs[4] = kv_spec
        assert all(a.shape == (depth, seq_batch, a_w, t) for a in kv_cache_rows)
        aliases = {len(args): 3, len(args) + 1: 4}
        in_specs += [pl.BlockSpec(memory_space=pl.ANY)] * 2
        args += list(kv_cache_rows)
        nb = tm // MOBA_BLOCK
        vt_rows = (a_w // DH) * VT_HEAD_ROWS
        out_shape += [jax.ShapeDtypeStruct((r, a_w), BF16), jax.ShapeDtypeStruct((vt_rows, r), BF16),
                      jax.ShapeDtypeStruct((r // tm, nb, a_w), F32)]
        out_specs += [row(a_w), pl.BlockSpec((vt_rows, tm), lambda i: (0, i)),
                      pl.BlockSpec((1, nb, a_w), lambda i: (i, 0, 0))]
    fchunk = 256 if d_ff % 256 == 0 else d_ff
    return pl.pallas_call(
        functools.partial(_ffn_proj_kernel, fchunk=fchunk, offs=offs, for_seq_attn=for_seq_attn,
                          n_aliased=len(aliases)),
        grid=(r // tm,),
        in_specs=in_specs,
        out_specs=out_specs,
        out_shape=out_shape,
        input_output_aliases=aliases,
        compiler_params=_params("arbitrary"),
        name="ffn_proj",
    )(*args)


def _out_ffn_kernel(x1_ref, a_ref, b_ref, c_ref, wo_ref, n_ref, wg_ref, wu_ref, wd_ref, fn_ref, y_ref,
                    *, fchunk):
    wa, wb = a_ref.shape[1], b_ref.shape[1]
    mix = (_dot(a_ref[...], wo_ref[0:wa, :]) + _dot(b_ref[...], wo_ref[wa:wa + wb, :])
           + _dot(c_ref[...], wo_ref[wa + wb:, :]))
    x2 = x1_ref[...] + mix
    hb = _rmsnorm(x2, n_ref[...]).astype(BF16)
    x3 = x2 + 0.5 * _swiglu(hb, wg_ref, wu_ref, wd_ref, fchunk)
    y_ref[...] = _rmsnorm(x3, fn_ref[...])


def _out_ffn(x1, conv_out, attn_out, ssm_out, lw, *, tm):
    r, d = x1.shape
    d_ff = lw["wg2"].shape[2]
    layer = lw["layer"]
    row = lambda w: pl.BlockSpec((tm, w), lambda i: (i, 0))
    fchunk = 256 if d_ff % 256 == 0 else d_ff
    return pl.pallas_call(
        functools.partial(_out_ffn_kernel, fchunk=fchunk),
        grid=(r // tm,),
        in_specs=[row(d), row(conv_out.shape[1]), row(attn_out.shape[1]), row(ssm_out.shape[1]),
                  _layer_spec(lw["w_out"], layer), _const_spec((1, d)), _layer_spec(lw["wg2"], layer),
                  _layer_spec(lw["wu2"], layer), _layer_spec(lw["wd2"], layer), _const_spec((1, d))],
        out_specs=row(d),
        out_shape=jax.ShapeDtypeStruct((r, d), F32),
        compiler_params=_params("arbitrary"),
        name="out_ffn",
    )(x1, conv_out, attn_out, ssm_out, lw["w_out"], lw["n2"], lw["wg2"], lw["wu2"], lw["wd2"], lw["n_fin"])


CONV_HALO = 32
CONV_SUB = 128


def _layernorm(c, g, b, eps=1e-5):
    xc = c - jnp.mean(c, axis=-1, keepdims=True)
    var = jnp.mean(xc * xc, axis=-1, keepdims=True)
    return xc * lax.rsqrt(var + eps) * g + b


def _conv_seq_kernel(u_ref, w_ref, b_ref, g_ref, lb_ref, o_ref, buf_ref, shift_ref, *, taps):
    tc = u_ref.shape[1]

    @pl.when(pl.program_id(1) == 0)
    def _():
        buf_ref[0:CONV_HALO, :] = jnp.zeros((CONV_HALO, buf_ref.shape[1]), F32)

    buf_ref[CONV_HALO:CONV_HALO + tc, :] = u_ref[0]
    first = CONV_HALO - (taps - 1)
    residues = sorted({(first + k) % SUBLANES for k in range(taps)})
    for r in residues:
        rows = max(first + k - r for k in range(taps) if (first + k) % SUBLANES == r) + tc
        shift_ref[r, 0:rows, :] = buf_ref[r:r + rows, :]
    for s in range(tc // CONV_SUB):
        acc = None
        for k in range(taps):
            r = (first + k) % SUBLANES
            off = s * CONV_SUB + first + k - r
            term = w_ref[k:k + 1, :] * shift_ref[r, off:off + CONV_SUB, :]
            acc = term if acc is None else acc + term
        c = acc + b_ref[...]
        o_ref[0, s * CONV_SUB:(s + 1) * CONV_SUB, :] = _silu(_layernorm(c, g_ref[...], lb_ref[...])).astype(o_ref.dtype)
    buf_ref[0:CONV_HALO, :] = buf_ref[tc:tc + CONV_HALO, :]


def _conv_seq(u, lw, *, tc=256):
    b, t, c = u.shape
    taps = lw["conv_w"].shape[0]
    assert taps - 1 <= CONV_HALO and t % tc == 0 and tc % CONV_SUB == 0
    return pl.pallas_call(
        functools.partial(_conv_seq_kernel, taps=taps),
        grid=(b, t // tc),
        in_specs=[pl.BlockSpec((1, tc, c), lambda i, j: (i, j, 0)),
                  _const_spec((taps, c)), _const_spec((1, c)), _const_spec((1, c)), _const_spec((1, c))],
        out_specs=pl.BlockSpec((1, tc, c), lambda i, j: (i, j, 0)),
        out_shape=jax.ShapeDtypeStruct((b, t, c), BF16),
        scratch_shapes=[pltpu.VMEM((CONV_HALO + tc, c), F32), pltpu.VMEM((SUBLANES, CONV_HALO + tc, c), F32)],
        compiler_params=_params("arbitrary", "arbitrary"),
        name="conv_seq",
    )(u, lw["conv_w"], lw["conv_b"], lw["ln_g"], lw["ln_b"])


def _rel_bias_tile(relb_ref, head, dist):
    max_exact = NUM_BUCKETS // 2
    n = jnp.maximum(dist, 0)
    nf = jnp.maximum(n, 1).astype(F32)
    large = max_exact + (jnp.log(nf / max_exact) / math.log(REL_MAX_DIST / max_exact)
                         * (NUM_BUCKETS - max_exact)).astype(jnp.int32)
    large = jnp.minimum(large, NUM_BUCKETS - 1)
    bucket = jnp.where(n < max_exact, n, large)
    out = jnp.zeros(dist.shape, F32)
    for b in range(NUM_BUCKETS):
        out = jnp.where(bucket == b, relb_ref[b, head], out)
    return out


def _moba_select_t(gate_t, n_eligible):
    nb = gate_t.shape[0]
    row = lax.broadcasted_iota(jnp.int32, gate_t.shape, 0).astype(F32)
    eligible = row < n_eligible.astype(F32)
    gate_t = jnp.where(eligible, gate_t, NEG_INF)
    sel = jnp.full(gate_t.shape, NEG_INF, F32)
    for _ in range(MOBA_TOPK):
        mx = jnp.max(gate_t, axis=0, keepdims=True)
        idx = jnp.min(jnp.where(gate_t == mx, row, float(nb)), axis=0, keepdims=True)
        pick = row == idx
        sel = jnp.where(pick, jnp.where(eligible, 0.0, sel), sel)
        gate_t = jnp.where(pick, NEG_INF, gate_t)
    return sel


def _attn_seq_kernel(relb_ref, q_ref, km_ref, k_ref, vt_ref, o_ref,
                     bias_own_ref, bias_prev_ref, kmbd_ref, selt_ref, acc_ref, s0_ref, s1_ref, *, n_heads, n_blocks):
    bs = MOBA_BLOCK
    width = n_heads * DH
    cur = pl.program_id(1)

    @pl.when((pl.program_id(0) == 0) & (cur == 0))
    def _():
        ki = lax.broadcasted_iota(jnp.int32, (bs, bs), 0)
        qi = lax.broadcasted_iota(jnp.int32, (bs, bs), 1)
        for h in range(n_heads):
            own = _rel_bias_tile(relb_ref, h, qi - ki) * LOG2E
            bias_own_ref[h] = jnp.where(qi >= ki, own, NEG_INF)
            bias_prev_ref[h] = _rel_bias_tile(relb_ref, h, qi - ki + bs) * LOG2E

    @pl.when(cur == 0)
    def _():
        kmbd_ref[...] = jnp.zeros(kmbd_ref.shape, F32)
        km = km_ref[0]
        km_lane = lax.broadcasted_iota(jnp.int32, km.shape, 1)
        for h in range(n_heads):
            kmbd_ref[h * n_blocks:(h + 1) * n_blocks, :] = jnp.where(
                (km_lane >= h * DH) & (km_lane < (h + 1) * DH), km, 0.0)

    qt = q_ref[0].T
    q_hi, q_mid, q_lo = _split3(qt)
    km_hi, km_mid, km_lo = _split3(kmbd_ref[...])
    gate_t = (_dot(km_hi, q_hi) + _dot(km_mid, q_hi) + _dot(km_hi, q_mid)
              + _dot(km_lo, q_hi) + _dot(km_hi, q_lo) + _dot(km_mid, q_mid))
    for h in range(n_heads):
        rows = slice(h * n_blocks, (h + 1) * n_blocks)
        selt_ref[rows, :] = _moba_select_t(gate_t[rows], cur)
    qs = qt * (LOG2E / math.sqrt(DH))
    row_w = lax.broadcasted_iota(jnp.int32, (width, bs), 0)
    qht = [jnp.where((row_w >= h * DH) & (row_w < (h + 1) * DH), qs, 0.0).astype(BF16) for h in range(n_heads)]
    far_bias = [relb_ref[NUM_BUCKETS - 1, h] * LOG2E for h in range(n_heads)]

    def scores(blk):
        kb = k_ref[0, pl.ds(pl.multiple_of(blk * bs, bs), bs), :]
        return [_dot(kb, qht[h]) for h in range(n_heads)]

    def softmax_pv(blk, s_all, ms, bias):
        start = pl.multiple_of(blk * bs, bs)

        def fold(h, m_new, shift):
            p = jnp.exp2(s_all[h]() + shift)
            alpha = jnp.exp2(ms[h] - m_new)
            vth = vt_ref[h * VT_HEAD_ROWS:(h + 1) * VT_HEAD_ROWS, pl.ds(start, bs)]
            acc_ref[h] = alpha * acc_ref[h] + _dot(vth, p.astype(BF16))

        biases = [bias(h) for h in range(n_heads)]
        m_new = []
        if biases[0].shape[0] == 1:
            for h in range(n_heads):
                m_new.append(jnp.maximum(ms[h], jnp.max(s_all[h](), axis=0, keepdims=True) + biases[h]))
                fold(h, m_new[h], biases[h] - m_new[h])
        else:
            for h in range(n_heads):
                m_new.append(jnp.maximum(ms[h], jnp.max(s_all[h]() + biases[h], axis=0, keepdims=True)))
            for h in range(n_heads):
                fold(h, m_new[h], biases[h] - m_new[h])
        return tuple(m_new)

    def sel_row(h, j):
        return selt_ref[pl.ds(h * n_blocks + j, 1), :]

    n_far = jnp.maximum(cur - 1, 0)
    last_far = jnp.maximum(n_far - 1, 0)
    prev = jnp.maximum(cur - 1, 0)
    s_own, s_prev = scores(cur), scores(prev)

    acc_ref[...] = jnp.zeros(acc_ref.shape, F32)
    ms = (jnp.full((1, bs), NEG_INF, F32),) * n_heads
    ms = softmax_pv(cur, [lambda s=s: s for s in s_own], ms, lambda h: bias_own_ref[h])
    ms = softmax_pv(prev, [lambda s=s: s for s in s_prev], ms,
                    lambda h: bias_prev_ref[h] + jnp.where(cur >= 1, sel_row(h, prev), NEG_INF))

    def put_scores(s_ref, blk):
        for h, s in enumerate(scores(jnp.minimum(blk, last_far))):
            s_ref[h] = s

    def far_bias_row(h, blk):
        row = sel_row(h, jnp.minimum(blk, last_far)) + far_bias[h]
        return jnp.where(blk < n_far, row, NEG_INF)

    def far_pair(b0, ms):
        b1 = b0 + 1
        put_scores(s1_ref, b1)
        ms = softmax_pv(b0, [lambda h=h: s0_ref[h] for h in range(n_heads)], ms, lambda h: far_bias_row(h, b0))
        put_scores(s0_ref, b1 + 1)
        b1c = jnp.minimum(b1, last_far)
        return softmax_pv(b1c, [lambda h=h: s1_ref[h] for h in range(n_heads)], ms, lambda h: far_bias_row(h, b1))

    put_scores(s0_ref, 0)
    done = 0
    for per_trip in (8, 4):
        trips = (n_far - done) // per_trip

        def trip(i, ms, done=done, per_trip=per_trip):
            for b0 in range(0, per_trip, 2):
                ms = far_pair(done + per_trip * i + b0, ms)
            return ms

        ms = lax.fori_loop(0, trips, trip, ms)
        done = done + per_trip * trips
    lax.fori_loop(0, (n_far - done + 1) // 2, lambda i, ms: far_pair(done + 2 * i, ms), ms)
    out_t = jnp.concatenate([acc_ref[h, :DH] * (1.0 / acc_ref[h, DH:DH + 1]) for h in range(n_heads)], axis=0)
    o_ref[0] = out_t.T.astype(o_ref.dtype)


def _attn_seq(q, kb, vt, kmean, rel_bias, *, n_heads):
    b, t, w = q.shape
    nb = t // MOBA_BLOCK
    assert vt.shape == (n_heads * VT_HEAD_ROWS, b * t)
    assert t % MOBA_BLOCK == 0 and nb * n_heads <= LANES and nb % SUBLANES == 0
    return pl.pallas_call(
        functools.partial(_attn_seq_kernel, n_heads=n_heads, n_blocks=nb),
        grid=(b, nb),
        in_specs=[pl.BlockSpec(memory_space=pltpu.SMEM),
                  pl.BlockSpec((1, MOBA_BLOCK, w), lambda i, j: (i, j, 0)),
                  pl.BlockSpec((1, nb, w), lambda i, j: (i, 0, 0)),
                  pl.BlockSpec((1, t, w), lambda i, j: (i, 0, 0)),
                  pl.BlockSpec((n_heads * VT_HEAD_ROWS, t), lambda i, j: (0, i))],
        out_specs=pl.BlockSpec((1, MOBA_BLOCK, w), lambda i, j: (i, j, 0)),
        out_shape=jax.ShapeDtypeStruct((b, t, w), BF16),
        scratch_shapes=[pltpu.VMEM((n_heads, MOBA_BLOCK, MOBA_BLOCK), F32),
                        pltpu.VMEM((n_heads, MOBA_BLOCK, MOBA_BLOCK), F32),
                        pltpu.VMEM((LANES, w), F32),
                        pltpu.VMEM((LANES, MOBA_BLOCK), F32),
                        pltpu.VMEM((n_heads, VT_HEAD_ROWS, MOBA_BLOCK), F32),
                        pltpu.VMEM((n_heads, MOBA_BLOCK, MOBA_BLOCK), F32),
                        pltpu.VMEM((n_heads, MOBA_BLOCK, MOBA_BLOCK), F32)],
        compiler_params=_params("arbitrary", "arbitrary"),
        name="attn_seq",
    )(rel_bias, q, kmean, kb, vt)


SSM_HALO = 8


def _softplus(x):
    return jnp.maximum(x, 0.0) + jnp.log1p(jnp.exp(-jnp.abs(x)))


def _ssd_seq_kernel(xbc_ref, z_ref, dt_ref, cw_ref, cb_ref, dtb_ref, alog_ref, dvec_ref, ng_ref,
                    y_ref, hfin_ref, buf_ref, state_ref, *, n_heads):
    lc = SSD_CHUNK
    n = SSM_N
    d_inner = n_heads * SSM_P
    pair_w = 2 * SSM_P
    heads_per_group = n_heads // SSM_G
    chunk = pl.program_id(1)

    @pl.when(chunk == 0)
    def _():
        buf_ref[0:SSM_HALO, :] = jnp.zeros((SSM_HALO, buf_ref.shape[1]), F32)
        state_ref[...] = jnp.zeros(state_ref.shape, F32)

    buf_ref[SSM_HALO:SSM_HALO + lc, :] = xbc_ref[0]
    taps = cw_ref.shape[0]
    first = SSM_HALO - (taps - 1)
    acc = None
    for k in range(taps):
        term = cw_ref[k:k + 1, :] * buf_ref[first + k:first + k + lc, :]
        acc = term if acc is None else acc + term
    xc = _silu(acc + cb_ref[...])
    buf_ref[0:SSM_HALO, :] = buf_ref[lc:lc + SSM_HALO, :]
    xs = xc[:, :d_inner]
    bm = xc[:, d_inner:d_inner + SSM_G * n]
    cm = xc[:, d_inner + SSM_G * n:]

    dt = _softplus(dt_ref[0] + dtb_ref[...])
    head_lane = lax.broadcasted_iota(jnp.int32, (lc, LANES), 1) < n_heads
    dta = jnp.where(head_lane, dt * (-jnp.exp(alog_ref[...])), 0.0)
    ri = lax.broadcasted_iota(jnp.int32, (lc, lc), 0)
    ci = lax.broadcasted_iota(jnp.int32, (lc, lc), 1)
    causal = ri >= ci
    acum = _dot_exact_lhs(jnp.where(causal, 1.0, 0.0).astype(BF16), dta)
    acum2 = acum * LOG2E
    acum2_t = acum2.T
    er = lax.broadcasted_iota(jnp.int32, (LANES, d_inner), 0)
    ec = lax.broadcasted_iota(jnp.int32, (LANES, d_inner), 1) // SSM_P
    expand = jnp.where(er == ec, 1.0, 0.0).astype(BF16)
    last = acum[lc - 1:lc, :]
    dt_x = _dot_exact_rhs(dt, expand, terms=2)
    grow_x = _dot_exact_rhs(jnp.exp(acum), expand, terms=2)
    toend_x = _dot_exact_rhs(jnp.exp(last - acum), expand, terms=2)
    xdt = xs * dt_x
    xdt_b = xdt.astype(BF16)
    xte = xdt * toend_x

    cg_b, bg_b, cb = [], [], []
    for g in range(SSM_G):
        cg_b.append(cm[:, g * n:(g + 1) * n].astype(BF16))
        bg_b.append(bm[:, g * n:(g + 1) * n].astype(BF16))
        cb.append(_dot_t(cg_b[g], bg_b[g]))

    lane_p = lax.broadcasted_iota(jnp.int32, (lc, pair_w), 1)
    row_p = lax.broadcasted_iota(jnp.int32, (pair_w, n), 0)
    y_tiles = []
    for pair in range(n_heads // 2):
        sl = slice(pair * pair_w, (pair + 1) * pair_w)
        g = (2 * pair) // heads_per_group
        x_pair = xdt_b[:, sl]
        st_old = state_ref[sl, :]
        y_inter = _dot_t(cg_b[g], st_old.astype(BF16)) * grow_x[:, sl]
        decays = [jnp.exp(last[:, 2 * pair + half:2 * pair + half + 1]) for half in range(2)]
        state_ref[sl, :] = (st_old * jnp.where(row_p < SSM_P, decays[0], decays[1])
                            + _dot(xte[:, sl].T.astype(BF16), bg_b[g]))
        y_intra = None
        for half in range(2):
            h = 2 * pair + half
            col = acum2[:, h:h + 1]
            row = acum2_t[h:h + 1, :]
            scores = jnp.where(causal, cb[g] * jnp.exp2(col - row), 0.0).astype(BF16)
            in_half = (lane_p >= half * SSM_P) & (lane_p < (half + 1) * SSM_P)
            part = _dot(scores, jnp.where(in_half, x_pair, jnp.zeros_like(x_pair)))
            y_intra = part if y_intra is None else y_intra + part
        y_tiles.append(y_intra + y_inter)
    y = jnp.concatenate(y_tiles, axis=1) + dvec_ref[...] * xs
    gated = y * _silu(z_ref[0])
    y_ref[0] = _rmsnorm(gated, ng_ref[...]).astype(y_ref.dtype)

    @pl.when(chunk == pl.num_programs(1) - 1)
    def _():
        hfin_ref[0] = state_ref[...]


def _ssd_seq(xbc, z, dt, lw, *, n_heads):
    b, t, cd = xbc.shape
    d_inner = z.shape[2]
    lc = SSD_CHUNK
    assert t % lc == 0 and n_heads % (2 * SSM_G) == 0 and n_heads <= LANES
    taps = lw["ssm_cw"].shape[0]
    assert taps - 1 <= SSM_HALO
    tile = lambda w: pl.BlockSpec((1, lc, w), lambda i, j: (i, j, 0))
    return pl.pallas_call(
        functools.partial(_ssd_seq_kernel, n_heads=n_heads),
        grid=(b, t // lc),
        in_specs=[tile(cd), tile(d_inner), tile(LANES),
                  _const_spec((taps, cd)), _const_spec((1, cd)), _const_spec((1, LANES)),
                  _const_spec((1, LANES)), _const_spec((1, d_inner)), _const_spec((1, d_inner))],
        out_specs=[tile(d_inner), pl.BlockSpec((1, d_inner, SSM_N), lambda i, j: (i, 0, 0))],
        out_shape=[jax.ShapeDtypeStruct((b, t, d_inner), BF16),
                   jax.ShapeDtypeStruct((b, d_inner, SSM_N), F32)],
        scratch_shapes=[pltpu.VMEM((SSM_HALO + lc, cd), F32), pltpu.VMEM((d_inner, SSM_N), F32)],
        compiler_params=_params("arbitrary", "arbitrary"),
        name="ssd_seq",
    )(xbc, z, dt, lw["ssm_cw"], lw["ssm_cb"], lw["dt_bias"], lw["a_log"], lw["d_vec"], lw["ssm_ng"])


def _stacked_matrices(w):
    n_in = w["w_in"].shape[2]
    n_pad = n_in - w["ssm_dt_bias"].shape[1] + LANES
    return dict(
        wg1=w["ffn1_wg"].astype(BF16), wu1=w["ffn1_wu"].astype(BF16), wd1=w["ffn1_wd"].astype(BF16),
        w_in=jnp.pad(w["w_in"].astype(BF16), ((0, 0), (0, 0), (0, n_pad - n_in))), w_out=w["w_out"].astype(BF16),
        wg2=w["ffn2_wg"].astype(BF16), wu2=w["ffn2_wu"].astype(BF16), wd2=w["ffn2_wd"].astype(BF16))


def _layer_weights(l, w, stacked):
    c_conv = w["conv_dw_w"].shape[2]
    a_w = (w["w_out"].shape[1] - c_conv - w["ssm_norm_g"].shape[1])
    d_inner = w["ssm_norm_g"].shape[1]
    cd = w["ssm_conv_w"].shape[2]
    n_ssm_heads = w["ssm_dt_bias"].shape[1]
    off_q = 2 * c_conv
    off_k = off_q + a_w
    off_v = off_k + a_w
    off_z = off_v + a_w
    off_xbc = off_z + d_inner
    off_dt = off_xbc + cd
    n_in = off_dt + n_ssm_heads
    assert w["w_in"].shape[2] == n_in
    n_pad = off_dt + LANES
    row = lambda a: a[l].reshape(1, -1).astype(F32)
    pad_row = lambda a: jnp.pad(a[l].astype(F32), (0, LANES - a.shape[1])).reshape(1, LANES)
    assert stacked["w_in"].shape[2] == n_pad
    return dict(
        stacked, layer=l,
        offs=(c_conv, off_q, off_k, off_v, off_z, off_xbc, off_dt, n_pad),
        n1=row(w["ffn1_norm"]), n_mix=row(w["mix_norm"]),
        qg=jnp.tile(w["q_norm_g"][l], a_w // DH).reshape(1, a_w),
        kg=jnp.tile(w["k_norm_g"][l], a_w // DH).reshape(1, a_w),
        conv_w=w["conv_dw_w"][l], conv_b=row(w["conv_dw_b"]), ln_g=row(w["conv_ln_g"]), ln_b=row(w["conv_ln_b"]),
        ssm_cw=w["ssm_conv_w"][l], ssm_cb=row(w["ssm_conv_b"]),
        dt_bias=pad_row(w["ssm_dt_bias"]), a_log=pad_row(w["ssm_a_log"]),
        d_vec=jnp.repeat(w["ssm_d"][l], SSM_P).reshape(1, d_inner), ssm_ng=row(w["ssm_norm_g"]),
        n2=row(w["ffn2_norm"]), n_fin=row(w["final_norm"]),
        n_attn_heads=a_w // DH, n_ssm_heads=n_ssm_heads,
    )


def _row_tile(rows, want=512):
    tm = min(want, rows)
    assert rows % tm == 0 and tm % SUBLANES == 0
    return tm


def _layer_seq(x, lw, rel_bias, kv_cache_rows):
    b, t, d = x.shape
    n_ah, n_sh = lw["n_attn_heads"], lw["n_ssm_heads"]
    tm = _row_tile(b * t)
    assert t % tm == 0 and tm % MOBA_BLOCK == 0
    x1, u, q, kt_rows, vt_rows, z, xbc, dt, kb, vt, km = _ffn_proj(
        x.reshape(b * t, d), lw, tm=tm, seq_batch=b, kv_cache_rows=kv_cache_rows)
    seq = lambda a: a.reshape(b, t, a.shape[-1])
    u, xbc = seq(u), seq(xbc)
    conv_out = _conv_seq(u, lw)
    attn_out = _attn_seq(seq(q), seq(kb), vt, km.reshape(b, t // MOBA_BLOCK, -1), rel_bias, n_heads=n_ah)
    ssm_out, h_fin = _ssd_seq(xbc, seq(z), seq(dt), lw, n_heads=n_sh)
    flat = lambda a: a.reshape(b * t, a.shape[-1])
    y = _out_ffn(x1, flat(conv_out), flat(attn_out), flat(ssm_out), lw, tm=tm)
    conv_taps = lw["conv_w"].shape[0]
    ssm_taps = lw["ssm_cw"].shape[0]
    states = (u[:, t - (conv_taps - 1):], xbc[:, t - (ssm_taps - 1):], h_fin.reshape(b, n_sh, SSM_P, SSM_N))
    return y.reshape(b, t, d), (kt_rows, vt_rows), states


def _conv_step_kernel(u_ref, st_ref, w_ref, b_ref, g_ref, lb_ref, o_ref):
    past = st_ref.shape[0]
    acc = w_ref[past:past + 1, :] * u_ref[...]
    for k in range(past):
        acc = acc + w_ref[k:k + 1, :] * st_ref[k]
    o_ref[...] = _silu(_layernorm(acc + b_ref[...], g_ref[...], lb_ref[...])).astype(o_ref.dtype)


def _conv_step(u, state_t, lw):
    return pl.pallas_call(
        _conv_step_kernel,
        out_shape=jax.ShapeDtypeStruct(u.shape, BF16),
        compiler_params=pltpu.CompilerParams(vmem_limit_bytes=VMEM_LIMIT_BYTES),
        name="conv_step",
    )(u, state_t, lw["conv_w"], lw["conv_b"], lw["ln_g"], lw["ln_b"])


SSD_STEP_ROWS = 8


def _ssd_step_kernel(xn_ref, st_ref, z_ref, dt_ref, h0_ref, cw_ref, cb_ref, dtb_ref, alog_ref, dvec_ref, ng_ref,
                     y_ref, h1_ref, *, n_heads):
    rows = xn_ref.shape[0]
    n = SSM_N
    d_inner = n_heads * SSM_P
    group_rows = (n_heads // SSM_G) * SSM_P
    past = st_ref.shape[0]
    acc = cw_ref[past:past + 1, :] * xn_ref[...]
    for k in range(past):
        acc = acc + cw_ref[k:k + 1, :] * st_ref[k]
    xc = _silu(acc + cb_ref[...])
    xs = xc[:, :d_inner]
    bm = xc[:, d_inner:d_inner + SSM_G * n]
    cm = xc[:, d_inner + SSM_G * n:]
    dt = _softplus(dt_ref[...] + dtb_ref[...])
    dta = dt * (-jnp.exp(alog_ref[...]))
    er = lax.broadcasted_iota(jnp.int32, (LANES, d_inner), 0)
    ec = lax.broadcasted_iota(jnp.int32, (LANES, d_inner), 1) // SSM_P
    expand = jnp.where(er == ec, 1.0, 0.0).astype(BF16)
    xdt = xs * _dot_exact_rhs(dt, expand)
    dec = jnp.exp(_dot_exact_rhs(dta, expand))
    pad = jnp.zeros((LANES - rows, d_inner), F32)
    xdt_t = jnp.concatenate([xdt, pad], axis=0).T
    dec_t = jnp.concatenate([dec, pad], axis=0).T
    lane = lax.broadcasted_iota(jnp.int32, (d_inner, LANES), 1)
    y_t = jnp.zeros((d_inner, LANES), F32)
    for b in range(rows):
        xcol = xdt_t[:, b:b + 1]
        dcol = dec_t[:, b:b + 1]
        ycols = []
        for g in range(SSM_G):
            rs = slice(g * group_rows, (g + 1) * group_rows)
            h0 = h0_ref[b, rs, :]
            brow = bm[b:b + 1, g * n:(g + 1) * n]
            crow = cm[b:b + 1, g * n:(g + 1) * n]
            cb = jnp.sum(crow * brow, axis=1, keepdims=True)
            h1_ref[b, rs, :] = h0 * dcol[rs] + xcol[rs] * brow
            ycols.append(jnp.sum(h0 * crow, axis=1, keepdims=True) * dcol[rs] + cb * xcol[rs])
        y_t = jnp.where(lane == b, jnp.concatenate(ycols, axis=0), y_t)
    y = y_t.T[:rows] + dvec_ref[...] * xs
    gated = y * _silu(z_ref[...])
    y_ref[...] = _rmsnorm(gated, ng_ref[...]).astype(y_ref.dtype)


def _ssd_step(xn, state_t, z, dt, h0_all, layer, lw, *, n_heads):
    db, cd = xn.shape
    d_inner = z.shape[1]
    rows = SSD_STEP_ROWS
    assert db % rows == 0
    past = state_t.shape[0]
    row = lambda w: pl.BlockSpec((rows, w), lambda i: (i, 0))
    return pl.pallas_call(
        functools.partial(_ssd_step_kernel, n_heads=n_heads),
        grid=(db // rows,),
        in_specs=[row(cd), pl.BlockSpec((past, rows, cd), lambda i: (0, i, 0)), row(d_inner), row(LANES),
                  pl.BlockSpec((None, rows, d_inner, SSM_N), lambda i: (layer, i, 0, 0)),
                  _const_spec((past + 1, cd)), _const_spec((1, cd)), _const_spec((1, LANES)),
                  _const_spec((1, LANES)), _const_spec((1, d_inner)), _const_spec((1, d_inner))],
        out_specs=[row(d_inner), pl.BlockSpec((rows, d_inner, SSM_N), lambda i: (i, 0, 0))],
        out_shape=[jax.ShapeDtypeStruct((db, d_inner), BF16), jax.ShapeDtypeStruct(h0_all.shape[1:], F32)],
        compiler_params=_params("arbitrary"),
        name="ssd_step",
    )(xn, state_t, z, dt, h0_all, lw["ssm_cw"], lw["ssm_cb"], lw["dt_bias"], lw["a_log"], lw["d_vec"], lw["ssm_ng"])


GATE_PAGES = 32


def _gate_pages_kernel(pt_ref, q_ref, *refs, n_blocks):
    del pt_ref
    page_refs, idx_ref, kmean_ref = refs[:-2], refs[-2], refs[-1]
    per_block = MOBA_BLOCK // PAGE_SIZE
    blocks = len(page_refs) // per_block
    chunk = pl.program_id(1)
    n_heads = kmean_ref.shape[0]

    @pl.when(chunk == 0)
    def _():
        kmean_ref[...] = jnp.zeros(kmean_ref.shape, F32)

    lane3 = lax.broadcasted_iota(jnp.int32, kmean_ref.shape, 2)
    km = kmean_ref[...]
    for blk in range(blocks):
        s = None
        for r in range(per_block):
            page = page_refs[blk * per_block + r][...]
            s = page if s is None else s + page
        col = jnp.sum(s, axis=2, keepdims=True) * (1.0 / MOBA_BLOCK)
        km = jnp.where(lane3 == chunk * blocks + blk, col, km)
    kmean_ref[...] = km

    @pl.when(chunk == pl.num_programs(1) - 1)
    def _():
        lane = lax.broadcasted_iota(jnp.int32, (SUBLANES, LANES), 1)
        for h in range(n_heads):
            q_hi, q_mid, q_lo = _split3(jnp.broadcast_to(q_ref[0, h], (SUBLANES, DH)))
            k_hi, k_mid, k_lo = _split3(kmean_ref[h])
            gate = (_dot(q_hi, k_hi) + _dot(q_hi, k_mid) + _dot(q_mid, k_hi)
                    + _dot(q_hi, k_lo) + _dot(q_lo, k_hi) + _dot(q_mid, k_mid))
            gate = jnp.where(lane < n_blocks, gate, NEG_INF)
            picks = jnp.zeros((SUBLANES, LANES), jnp.int32)
            for rank in range(MOBA_TOPK):
                mx = jnp.max(gate, axis=1, keepdims=True)
                idx = jnp.min(jnp.where(gate == mx, lane, LANES), axis=1, keepdims=True)
                picks = jnp.where(lane == rank, idx, picks)
                gate = jnp.where(lane == idx, NEG_INF, gate)
            idx_ref[0, h] = picks


def _gate_pages(q4, cache_t, layer, page_table_flat, n_pages):
    db, n_heads = q4.shape[:2]
    per_block = MOBA_BLOCK // PAGE_SIZE
    n_blocks = n_pages // per_block
    pg = min(GATE_PAGES, n_pages)
    assert n_pages % pg == 0 and pg % per_block == 0 and MOBA_TOPK <= n_blocks <= LANES

    def page_spec(j):
        return pl.BlockSpec((None, None, n_heads, DH, PAGE_SIZE),
                            lambda b, i, pt: (layer, pt[b * n_pages + i * pg + j], 0, 0, 0))

    return pl.pallas_call(
        functools.partial(_gate_pages_kernel, n_blocks=n_blocks),
        grid_spec=pltpu.PrefetchScalarGridSpec(
            num_scalar_prefetch=1, grid=(db, n_pages // pg),
            in_specs=[pl.BlockSpec((1, n_heads, 1, DH), lambda b, i, pt: (b, 0, 0, 0))]
            + [page_spec(j) for j in range(pg)],
            out_specs=pl.BlockSpec((1, n_heads, SUBLANES, LANES), lambda b, i, pt: (b, 0, 0, 0)),
            scratch_shapes=[pltpu.VMEM((n_heads, DH, LANES), F32)]),
        out_shape=jax.ShapeDtypeStruct((db, n_heads, SUBLANES, LANES), jnp.int32),
        compiler_params=_params("arbitrary", "arbitrary"),
        name="gate_pages",
    )(page_table_flat, q4, *([cache_t] * pg))


def _attn_step_kernel(pt_ref, sel_ref, relb_ref, q_ref, kn_ref, vn_ref, *refs, n_heads, past_len):
    del pt_ref
    per_block = MOBA_BLOCK // PAGE_SIZE
    n_pg = MOBA_TOPK * per_block
    k_pages, v_pages, o_ref = refs[:n_pg], refs[n_pg:2 * n_pg], refs[2 * n_pg]
    b, h = pl.program_id(0), pl.program_id(1)
    qh = q_ref[...] * (1.0 / math.sqrt(DH))
    q8 = jnp.broadcast_to(qh, (SUBLANES, DH)).astype(BF16)
    key_off = lax.broadcasted_iota(jnp.int32, (1, PAGE_SIZE), 1)
    s_own = jnp.sum(qh * kn_ref[...], axis=1, keepdims=True) + relb_ref[0, h]
    logits = []
    for j in range(MOBA_TOPK):
        blk = sel_ref[(b * n_heads + h) * MOBA_TOPK + j]
        for r in range(per_block):
            dist = past_len - (blk * MOBA_BLOCK + r * PAGE_SIZE + key_off)
            kt = k_pages[j * per_block + r][...].astype(BF16)
            logits.append(_dot(q8, kt)[0:1, :] + _rel_bias_tile(relb_ref, h, dist))
    m = s_own
    for s in logits:
        m = jnp.maximum(m, jnp.max(s, axis=1, keepdims=True))
    p_own = jnp.exp(s_own - m)
    l = p_own
    acc = p_own * vn_ref[...]
    for page, s in enumerate(logits):
        p = jnp.exp(s - m)
        l = l + jnp.sum(p, axis=1, keepdims=True)
        vt = v_pages[page][...].astype(BF16)
        acc = acc + _dot_t(jnp.broadcast_to(p, (SUBLANES, PAGE_SIZE)).astype(BF16), vt)[0:1, :]
    o_ref[...] = jnp.broadcast_to(acc / l, (SUBLANES, DH))


def _attn_step(q4, kn4, vn4, cache_kt, cache_vt, layer, page_table_flat, sel_flat, rel_bias, *, n_pages):
    db, n_heads = q4.shape[:2]
    per_block = MOBA_BLOCK // PAGE_SIZE
    past_len = n_pages * PAGE_SIZE
    assert past_len % MOBA_BLOCK == 0 and past_len // MOBA_BLOCK >= MOBA_TOPK

    def page_spec(j, r):
        def index_map(b, h, pt, sel):
            blk = sel[(b * n_heads + h) * MOBA_TOPK + j]
            return (layer, pt[b * n_pages + blk * per_block + r], h, 0, 0)
        return pl.BlockSpec((None, None, None, DH, PAGE_SIZE), index_map)

    page_specs = [page_spec(j, r) for j in range(MOBA_TOPK) for r in range(per_block)]
    tok = pl.BlockSpec((None, None, 1, DH), lambda b, h, pt, sel: (b, h, 0, 0))
    n_pg = len(page_specs)
    return pl.pallas_call(
        functools.partial(_attn_step_kernel, n_heads=n_heads, past_len=past_len),
        grid_spec=pltpu.PrefetchScalarGridSpec(
            num_scalar_prefetch=2, grid=(db, n_heads),
            in_specs=[pl.BlockSpec(memory_space=pltpu.SMEM), tok, tok, tok] + page_specs + page_specs,
            out_specs=pl.BlockSpec((None, None, SUBLANES, DH), lambda b, h, pt, sel: (b, h, 0, 0))),
        out_shape=jax.ShapeDtypeStruct((db, n_heads, SUBLANES, DH), F32),
        compiler_params=_params("arbitrary", "arbitrary"),
        name="attn_step",
    )(page_table_flat, sel_flat, rel_bias, q4, kn4, vn4, *([cache_kt] * n_pg), *([cache_vt] * n_pg))


def _layer_step(x, conv_state, ssm_conv_state, ssm_states, cache_kt, cache_vt, layer, page_table, lw, rel_bias):
    db, t, d = x.shape
    assert t == 1
    n_ah, n_sh = lw["n_attn_heads"], lw["n_ssm_heads"]
    tm = _row_tile(db)
    x1, u, q, k, v, z, xbc, dt = _ffn_proj(x.reshape(db, d), lw, tm=tm)
    conv_out = _conv_step(u, conv_state.transpose(1, 0, 2), lw)
    n_pages = page_table.shape[1]
    pt_flat = page_table.reshape(-1)
    q4 = q.reshape(db, n_ah, 1, DH)
    picks = _gate_pages(q4, cache_kt, layer, pt_flat, n_pages)
    sel_flat = picks[:, :, 0, :MOBA_TOPK].reshape(-1)
    attn = _attn_step(q4, k.reshape(db, n_ah, 1, DH), v.reshape(db, n_ah, 1, DH), cache_kt, cache_vt, layer,
                      pt_flat, sel_flat, rel_bias, n_pages=n_pages)
    attn_out = attn[:, :, 0, :].reshape(db, n_ah * DH)
    depth = ssm_states.shape[0]
    ssm_out, h_new = _ssd_step(xbc, ssm_conv_state.transpose(1, 0, 2), z, dt,
                               ssm_states.reshape(depth, db, n_sh * SSM_P, SSM_N), layer, lw, n_heads=n_sh)
    y = _out_ffn(x1, conv_out, attn_out.astype(BF16), ssm_out, lw, tm=tm)
    states = (k.reshape(db, 1, n_ah, DH), v.reshape(db, 1, n_ah, DH),
              jnp.concatenate([conv_state[:, 1:], u[:, None, :]], axis=1),
              jnp.concatenate([ssm_conv_state[:, 1:], xbc[:, None, :]], axis=1),
              h_new.reshape(ssm_states.shape[1:]))
    return y.reshape(db, 1, d), states


def kernel(x_prompt, x_sample, cache_k, cache_v, state_conv, state_ssm_conv, state_ssm, page_table, rel_bias, ffn1_norm, ffn1_wg, ffn1_wu, ffn1_wd, mix_norm, w_in, w_out, conv_dw_w, conv_dw_b, conv_ln_g, conv_ln_b, q_norm_g, k_norm_g, ssm_conv_w, ssm_conv_b, ssm_dt_bias, ssm_a_log, ssm_d, ssm_norm_g, ffn2_norm, ffn2_wg, ffn2_wu, ffn2_wd, final_norm):
    w = dict(ffn1_norm=ffn1_norm, ffn1_wg=ffn1_wg, ffn1_wu=ffn1_wu, ffn1_wd=ffn1_wd, mix_norm=mix_norm,
             w_in=w_in, w_out=w_out, conv_dw_w=conv_dw_w, conv_dw_b=conv_dw_b, conv_ln_g=conv_ln_g,
             conv_ln_b=conv_ln_b, q_norm_g=q_norm_g, k_norm_g=k_norm_g, ssm_conv_w=ssm_conv_w,
             ssm_conv_b=ssm_conv_b, ssm_dt_bias=ssm_dt_bias, ssm_a_log=ssm_a_log, ssm_d=ssm_d,
             ssm_norm_g=ssm_norm_g, ffn2_norm=ffn2_norm, ffn2_wg=ffn2_wg, ffn2_wu=ffn2_wu, ffn2_wd=ffn2_wd,
             final_norm=final_norm)
    depth = w_in.shape[0]
    y_p, y_s = x_prompt, x_sample
    st_p, st_s = [], []
    cache_kt = cache_k.transpose(0, 1, 3, 4, 2)
    cache_vt = cache_v.transpose(0, 1, 3, 4, 2)
    stacked = _stacked_matrices(w)
    bp, tp = x_prompt.shape[:2]
    n_ah, a_w = cache_k.shape[3], cache_k.shape[3] * cache_k.shape[4]
    kv_rows = (jnp.zeros((depth, bp, a_w, tp), F32), jnp.zeros((depth, bp, a_w, tp), F32))
    for l in range(depth):
        lw = _layer_weights(l, w, stacked)
        y_p, kv_rows, st = _layer_seq(y_p, lw, rel_bias, kv_rows)
        st_p.append(st)
        y_s, st = _layer_step(y_s, state_conv[l], state_ssm_conv[l], state_ssm, cache_kt, cache_vt, l,
                              page_table, lw, rel_bias)
        st_s.append(st)
    stack = lambda sts, i: jnp.stack([s[i] for s in sts])
    kv_p = tuple(a.reshape(depth, bp, n_ah, DH, tp).transpose(0, 1, 4, 2, 3) for a in kv_rows)
    return ((y_p, y_s) + kv_p + tuple(stack(st_p, i) for i in range(3))
            + tuple(stack(st_s, i) for i in range(5)))
```
